```python
import math
import jax, jax.numpy as jnp
from jax import lax
import numpy as np

D_MODEL = 1024
BATCH = 4
SEQ = 4096
DEPTH = 4
DEC_BATCH = 128
DEC_SEQ = 4
PAST_LEN = 2048
PAGE_SIZE = 128

H_M = 4
DH_M = 128
D_M = H_M * DH_M
MCHUNK = 64
HQ = 8
HKV = 2
REP = HQ // HKV
HD = 64
D_A = HQ * HD
CMP_LEN = 32
CMP_STRIDE = 16
CMP_HID = 256
SLC_LEN = 64
N_SEL = 16
WINDOW = 512
QBLK = 128
N_BUCKETS = 32
REL_MAX_DIST = 128
N_EXP = 16
N_GROUPS = 4
EXP_PER_GROUP = N_EXP // N_GROUPS
TOP_K = 2
D_EXP = 256
ALPHA = (2 * DEPTH) ** 0.25
BETA = (8 * DEPTH) ** -0.25
LN_EPS = 1e-5
NEG = -1e30
SPLIT_SIZES = (D_M, D_M, D_M, D_M, H_M, H_M, D_A, 6 * HKV * HD, 3 * HQ, 2 * D_MODEL)
N_IN = sum(SPLIT_SIZES)

kernel_name = "hybrid_mlstm_nsa_moe_step"


def _layer_norm(x, g, b):
    xf = x.astype(jnp.float32)
    mu = xf.mean(-1, keepdims=True)
    var = jnp.square(xf - mu).mean(-1, keepdims=True)
    y = (xf - mu) * lax.rsqrt(var + LN_EPS) * g.astype(jnp.float32) + b.astype(jnp.float32)
    return y.astype(x.dtype)


def _rel_bucket(dist):
    n = jnp.maximum(dist, 0)
    exact = N_BUCKETS // 2
    nf = jnp.maximum(n, 1).astype(jnp.float32)
    large = exact + (jnp.log(nf / exact) / math.log(REL_MAX_DIST / exact) * (N_BUCKETS - exact)).astype(jnp.int32)
    return jnp.where(n < exact, n, jnp.minimum(large, N_BUCKETS - 1))


def _mixer_inputs(x, w_in_l, b_in_l):
    B, T = x.shape[:2]
    z = x @ w_in_l + b_in_l
    mq, mk, mv, mo, mi, mf, aq, akv, ag, gm = jnp.split(z, np.cumsum(SPLIT_SIZES)[:-1].tolist(), axis=-1)
    q = mq.reshape(B, T, H_M, DH_M).astype(jnp.float32)
    k = mk.reshape(B, T, H_M, DH_M).astype(jnp.float32) * DH_M ** -0.5
    v = mv.reshape(B, T, H_M, DH_M).astype(jnp.float32)
    ipre = mi.astype(jnp.float32)
    logf = jax.nn.log_sigmoid(mf.astype(jnp.float32))
    qa = aq.reshape(B, T, HKV, REP, HD)
    kv = akv.reshape(B, T, 6, HKV, HD)
    return (q, k, v, ipre, logf), mo, qa, kv, ag, gm


def _mlstm_chunk(q, k, v, ipre, logf, C0, n0, m0):
    L = q.shape[1]
    F = jnp.cumsum(logf, axis=1).transpose(0, 2, 1)
    ih = ipre.transpose(0, 2, 1)
    causal = np.tril(np.ones((L, L), dtype=bool))
    Dm = jnp.where(causal, F[:, :, :, None] - F[:, :, None, :] + ih[:, :, None, :], NEG)
    b = F + m0[:, :, None]
    m = jnp.maximum(b, Dm.max(-1))
    W = jnp.exp(Dm - m[..., None])
    dec = jnp.exp(b - m)
    S = jnp.einsum('bthd,bshd->bhts', q, k) * W
    num = dec[..., None] * jnp.einsum('bhvk,bthk->bhtv', C0, q) + jnp.einsum('bhts,bshv->bhtv', S, v)
    den = dec * jnp.einsum('bhk,bthk->bht', n0, q) + S.sum(-1)
    h = num / jnp.maximum(jnp.abs(den), jnp.exp(-m))[..., None]
    wL, dL = W[:, :, -1, :], dec[:, :, -1]
    C = dL[..., None, None] * C0 + jnp.einsum('bhs,bshv,bshk->bhvk', wL, v, k)
    n = dL[..., None] * n0 + jnp.einsum('bhs,bshk->bhk', wL, k)
    return h.transpose(0, 2, 1, 3), C, n, m[:, :, -1]


def _mlstm_prompt(q, k, v, ipre, logf):
    B, T = q.shape[:2]
    nc = T // MCHUNK

    def chunks(a):
        return a.reshape((B, nc, MCHUNK) + a.shape[2:]).swapaxes(0, 1)

    init = (jnp.zeros((B, H_M, DH_M, DH_M), jnp.float32), jnp.zeros((B, H_M, DH_M), jnp.float32),
            jnp.zeros((B, H_M), jnp.float32))

    def step(carry, inp):
        h, C, n, m = _mlstm_chunk(*inp, *carry)
        return (C, n, m), h

    (C, n, m), hs = lax.scan(step, init, (chunks(q), chunks(k), chunks(v), chunks(ipre), chunks(logf)))
    return hs.swapaxes(0, 1).reshape(B, T, H_M, DH_M), C, n, m


def _mlstm_branch(h, o_pre, norm_g):
    B, T = h.shape[:2]
    mu = h.mean(-1, keepdims=True)
    var = jnp.square(h - mu).mean(-1, keepdims=True)
    hn = (h - mu) * lax.rsqrt(var + LN_EPS) * norm_g.astype(jnp.float32).reshape(H_M, DH_M)
    return hn.reshape(B, T, D_M) * jax.nn.sigmoid(o_pre.astype(jnp.float32))


def _compress(rows, w1, w2):
    B, Lk = rows.shape[:2]
    n_cmp = (Lk - CMP_LEN) // CMP_STRIDE + 1
    idx = np.arange(n_cmp)[:, None] * CMP_STRIDE + np.arange(CMP_LEN)[None, :]
    blk = rows[:, idx].transpose(0, 1, 3, 2, 4).reshape(B, n_cmp, HKV, CMP_LEN * HD)
    return jax.nn.gelu(blk @ w1) @ w2


def _cmp_slc_attend(q, qpos, k_cmp, v_cmp, k_slc, v_slc, w_cmp1, w_cmp2, rel_bias):
    B, Tq = q.shape[:2]
    Lk = k_cmp.shape[1]
    scale = HD ** -0.5
    ck = _compress(k_cmp, w_cmp1[0], w_cmp2[0]).astype(jnp.float32)
    cv = _compress(v_cmp, w_cmp1[1], w_cmp2[1]).astype(jnp.float32)
    n_cmp = ck.shape[1]
    cmp_start = np.arange(n_cmp) * CMP_STRIDE
    cmp_end = (cmp_start + CMP_LEN - 1).astype(np.int32)
    n_slc = -(-Lk // SLC_LEN)
    slc_start = (np.arange(n_slc) * SLC_LEN).astype(np.int32)
    ov = ((cmp_start[:, None] < slc_start[None, :] + SLC_LEN) &
          (cmp_start[:, None] + CMP_LEN > slc_start[None, :])).astype(np.float32)
    k_sel = min(N_SEL, n_slc)
    pad = n_slc * SLC_LEN - Lk

    def blocks(r):
        r = jnp.pad(r, ((0, 0), (0, pad), (0, 0), (0, 0)))
        return r.reshape(B, n_slc, SLC_LEN, HKV, HD).transpose(0, 3, 1, 2, 4)

    kb, vb = blocks(k_slc), blocks(v_slc)
    rbf = rel_bias.astype(jnp.float32)
    rb = rbf.reshape(N_BUCKETS, HKV, REP)
    bi = jnp.arange(B)[:, None, None, None]
    gi = jnp.arange(HKV)[None, :, None, None]
    qb_len = QBLK if Tq % QBLK == 0 else Tq
    nqb = Tq // qb_len
    qs = q.reshape(B, nqb, qb_len, HKV, REP, HD).swapaxes(0, 1)
    ps = qpos.reshape(nqb, qb_len)

    def one_block(args):
        qb, pb = args
        qb = qb.astype(jnp.float32)
        QB = qb.shape[1]
        s = jnp.einsum('bqgrd,bcgd->bgrqc', qb, ck) * scale
        dist = pb[:, None] - cmp_end[None, :]
        valid = dist >= 0
        bias = rbf[_rel_bucket(dist)].reshape(QB, n_cmp, HKV, REP).transpose(2, 3, 0, 1)
        s = jnp.where(valid, s + bias, NEG)
        p = jax.nn.softmax(s, axis=-1) * valid
        o_cmp = jnp.einsum('bgrqc,bcgd->bqgrd', p, cv)
        imp = jnp.einsum('bgrqc,cj->bgqj', p, ov)
        avail = slc_start[None, :] <= pb[:, None]
        cur = (pb // SLC_LEN)[:, None] == jnp.arange(n_slc)[None, :]
        imp = jnp.where(cur, -NEG, jnp.where(avail, imp, NEG))
        top_v, top_i = lax.top_k(imp, k_sel)
        sel_ok = top_v > 0.5 * NEG
        kg = kb[bi, gi, top_i].astype(jnp.float32)
        vg = vb[bi, gi, top_i].astype(jnp.float32)
        s2 = jnp.einsum('bqgrd,bgqnsd->bgrqns', qb, kg) * scale
        kpos = top_i[..., None] * SLC_LEN + jnp.arange(SLC_LEN)
        dist2 = pb[None, None, :, None, None] - kpos
        valid2 = (dist2 >= 0) & sel_ok[..., None]
        bias2 = jax.vmap(lambda tab, bk: tab[bk], in_axes=(1, 1), out_axes=1)(rb, _rel_bucket(dist2))
        s2 = jnp.where(valid2[:, :, None], s2 + bias2.transpose(0, 1, 5, 2, 3, 4), NEG)
        p2 = jax.nn.softmax(s2.reshape(s2.shape[:4] + (-1,)), axis=-1).reshape(s2.shape)
        o_slc = jnp.einsum('bgrqns,bgqnsd->bqgrd', p2, vg)
        return o_cmp, o_slc

    o_cmp, o_slc = lax.map(one_block, (qs, ps))
    return (o_cmp.swapaxes(0, 1).reshape(B, Tq, HQ, HD), o_slc.swapaxes(0, 1).reshape(B, Tq, HQ, HD))


def _window_prompt(q, k, v, rel_bias):
    B, T = q.shape[:2]
    nb, nw = T // QBLK, WINDOW // QBLK
    kb_len = (nw + 1) * QBLK

    def band(r):
        rp = jnp.pad(r, ((0, 0), (WINDOW, 0), (0, 0), (0, 0))).reshape(B, nb + nw, QBLK, HKV, HD)
        return jnp.concatenate([rp[:, j:j + nb] for j in range(nw + 1)], axis=2).astype(jnp.float32)

    kband, vband = band(k), band(v)
    qb = q.reshape(B, nb, QBLK, HKV, REP, HD).astype(jnp.float32)
    s = jnp.einsum('bnqgrd,bnkgd->bngrqk', qb, kband) * HD ** -0.5
    dist = (np.arange(QBLK)[:, None] + WINDOW - np.arange(kb_len)[None, :]).astype(np.int32)
    key_pos = np.arange(nb)[:, None] * QBLK - WINDOW + np.arange(kb_len)[None, :]
    valid = ((dist >= 0) & (dist < WINDOW))[None] & (key_pos >= 0)[:, None, :]
    bias = rel_bias.astype(jnp.float32)[_rel_bucket(jnp.asarray(dist))]
    bias = bias.reshape(QBLK, kb_len, HKV, REP).transpose(2, 3, 0, 1)
    s = jnp.where(valid[:, None, None], s + bias, NEG)
    p = jax.nn.softmax(s, axis=-1)
    return jnp.einsum('bngrqk,bnkgd->bnqgrd', p, vband).reshape(B, T, HQ, HD)


def _window_sample(q, k, v, rel_bias):
    B, Tq = q.shape[:2]
    Lw = k.shape[1]
    qpos = PAST_LEN + np.arange(Tq)
    kpos = PAST_LEN - (Lw - Tq) + np.arange(Lw)
    dist = (qpos[:, None] - kpos[None, :]).astype(np.int32)
    valid = (dist >= 0) & (dist < WINDOW)
    bias = rel_bias.astype(jnp.float32)[_rel_bucket(jnp.asarray(dist))].reshape(Tq, Lw, HKV, REP).transpose(2, 3, 0, 1)
    s = jnp.einsum('bqgrd,bkgd->bgrqk', q.astype(jnp.float32), k.astype(jnp.float32)) * HD ** -0.5
    s = jnp.where(valid, s + bias, NEG)
    p = jax.nn.softmax(s, axis=-1)
    return jnp.einsum('bgrqk,bkgd->bqgrd', p, v.astype(jnp.float32)).reshape(B, Tq, HQ, HD)


def _merge(x, h_m, mo, o_cmp, o_slc, o_win, ag, gm, norm_g, w_branch, w_out, ln_g, ln_b):
    B, T = x.shape[:2]
    m_out = _mlstm_branch(h_m, mo, norm_g)
    g = jax.nn.sigmoid(ag.astype(jnp.float32)).reshape(B, T, 3, HQ, 1)
    a_out = (g[:, :, 0] * o_cmp + g[:, :, 1] * o_slc + g[:, :, 2] * o_win).reshape(B, T, D_A)
    gates = jax.nn.sigmoid(gm.astype(jnp.float32)).reshape(B, T, 2, D_MODEL)
    u = (gates[:, :, 0] * (m_out.astype(x.dtype) @ w_branch[0])
         + gates[:, :, 1] * (a_out.astype(x.dtype) @ w_branch[1]))
    y = u.astype(x.dtype) @ w_out
    return _layer_norm(ALPHA * x + y, ln_g, ln_b)


def _moe(x, w_router, router_bias, w_gate, w_up, w_down):
    B, T, D = x.shape
    xf = x.reshape(B * T, D)
    aff = jax.nn.sigmoid((xf @ w_router).astype(jnp.float32))
    sel = aff + router_bias.astype(jnp.float32)
    grp_top, _ = lax.top_k(sel.reshape(-1, N_GROUPS, EXP_PER_GROUP), TOP_K)
    best = jnp.argmax(grp_top.sum(-1), axis=-1)
    in_grp = (np.arange(N_EXP) // EXP_PER_GROUP)[None, :] == best[:, None]
    _, top_i = lax.top_k(jnp.where(in_grp, sel, NEG), TOP_K)
    w = jnp.take_along_axis(aff, top_i, axis=-1)
    w = w / w.sum(-1, keepdims=True)
    combine = (jax.nn.one_hot(top_i, N_EXP, dtype=jnp.float32) * w[..., None]).sum(1)
    h = jax.nn.silu(jnp.einsum('nd,edf->nef', xf, w_gate)) * jnp.einsum('nd,edf->nef', xf, w_up)
    h = h * combine[..., None].astype(h.dtype)
    return jnp.einsum('nef,efd->nd', h, w_down).reshape(B, T, D)


def setup_inputs(seed: int = 0) -> dict:
    key = jax.random.key(seed)
    ks = jax.random.split(key, 32)
    f32 = jnp.float32
    n_pages = PAST_LEN // PAGE_SIZE
    n_used = DEC_BATCH * n_pages
    n_pool = (n_used * 5) // 4
    win_buf = min(WINDOW, PAST_LEN)

    def nrm(k, shape, s):
        return s * jax.random.normal(k, shape, f32)

    f_off = 4 * D_M + H_M
    b_in = nrm(ks[9], (DEPTH, N_IN), 0.02).at[:, f_off:f_off + H_M].add(jnp.linspace(3.0, 6.0, H_M, dtype=f32))
    return {
        "x_prompt": nrm(ks[0], (BATCH, SEQ, D_MODEL), 1.0),
        "x_sample": nrm(ks[1], (DEC_BATCH, DEC_SEQ, D_MODEL), 1.0),
        "cache_kv": nrm(ks[2], (DEPTH, n_pool, PAGE_SIZE, 4, HKV, HD), 1.0),
        "state_win_kv": nrm(ks[3], (DEPTH, DEC_BATCH, win_buf, 2, HKV, HD), 1.0),
        "state_mlstm_C": nrm(ks[4], (DEPTH, DEC_BATCH, H_M, DH_M, DH_M), 0.1),
        "state_mlstm_n": nrm(ks[5], (DEPTH, DEC_BATCH, H_M, DH_M), 0.3),
        "state_mlstm_m": nrm(ks[6], (DEPTH, DEC_BATCH, H_M), 1.0),
        "page_table": jax.random.permutation(ks[7], n_pool)[:n_used].reshape(DEC_BATCH, n_pages).astype(jnp.int32),
        "w_in": nrm(ks[8], (DEPTH, D_MODEL, N_IN), D_MODEL ** -0.5),
        "b_in": b_in,
        "w_cmp1": nrm(ks[10], (DEPTH, 2, CMP_LEN * HD, CMP_HID), (CMP_LEN * HD) ** -0.5),
        "w_cmp2": nrm(ks[11], (DEPTH, 2, CMP_HID, HD), CMP_HID ** -0.5),
        "mh_norm_g": 1.0 + nrm(ks[12], (DEPTH, D_M), 0.02),
        "w_branch": nrm(ks[13], (DEPTH, 2, D_M, D_MODEL), BETA * D_M ** -0.5),
        "w_out": nrm(ks[14], (DEPTH, D_MODEL, D_MODEL), BETA * D_MODEL ** -0.5),
        "ln1_g": 1.0 + nrm(ks[15], (DEPTH, D_MODEL), 0.02),
        "ln1_b": nrm(ks[16], (DEPTH, D_MODEL), 0.02),
        "ln2_g": 1.0 + nrm(ks[17], (DEPTH, D_MODEL), 0.02),
        "ln2_b": nrm(ks[18], (DEPTH, D_MODEL), 0.02),
        "w_router": nrm(ks[19], (D_MODEL, N_EXP), D_MODEL ** -0.5),
        "router_bias": nrm(ks[20], (N_EXP,), 0.01),
        "w_gate_e": nrm(ks[21], (DEPTH, N_EXP, D_MODEL, D_EXP), BETA * D_MODEL ** -0.5),
        "w_up_e": nrm(ks[22], (DEPTH, N_EXP, D_MODEL, D_EXP), BETA * D_MODEL ** -0.5),
        "w_down_e": nrm(ks[23], (DEPTH, N_EXP, D_EXP, D_MODEL), BETA * D_EXP ** -0.5),
        "rel_bias": nrm(ks[24], (N_BUCKETS, HQ), 0.5),
    }


def reference(x_prompt, x_sample, cache_kv, state_win_kv, state_mlstm_C, state_mlstm_n, state_mlstm_m, page_table,
              w_in, b_in, w_cmp1, w_cmp2, mh_norm_g, w_branch, w_out, ln1_g, ln1_b, ln2_g, ln2_b,
              w_router, router_bias, w_gate_e, w_up_e, w_down_e, rel_bias):
    n_pages = PAST_LEN // PAGE_SIZE
    win_buf = min(WINDOW, PAST_LEN)
    pad_p = max(win_buf - SEQ, 0)
    pos_p = jnp.arange(SEQ, dtype=jnp.int32)
    pos_s = PAST_LEN + jnp.arange(DEC_SEQ, dtype=jnp.int32)
    xp, xs = x_prompt, x_sample
    kv_p, win_p, C_p, n_p, m_p = [], [], [], [], []
    kv_s, win_s, C_s, n_s, m_s = [], [], [], [], []
    for l in range(DEPTH):
        (mq, mk, mv, mi, mf), mo, qa, kv, ag, gm = _mixer_inputs(xp, w_in[l], b_in[l])
        h_m, C, n, m = _mlstm_prompt(mq, mk, mv, mi, mf)
        o_cmp, o_slc = _cmp_slc_attend(qa, pos_p, kv[:, :, 0], kv[:, :, 1], kv[:, :, 2], kv[:, :, 3],
                                       w_cmp1[l], w_cmp2[l], rel_bias)
        o_win = _window_prompt(qa, kv[:, :, 4], kv[:, :, 5], rel_bias)
        xp = _merge(xp, h_m, mo, o_cmp, o_slc, o_win, ag, gm, mh_norm_g[l], w_branch[l], w_out[l], ln1_g[l], ln1_b[l])
        xp = _layer_norm(ALPHA * xp + _moe(xp, w_router, router_bias, w_gate_e[l], w_up_e[l], w_down_e[l]),
                         ln2_g[l], ln2_b[l])
        kv_p.append(kv[:, :, :4])
        win_p.append(jnp.pad(kv[:, :, 4:], ((0, 0), (pad_p, 0), (0, 0), (0, 0), (0, 0)))[:, -win_buf:])
        C_p.append(C)
        n_p.append(n)
        m_p.append(m)
        (mq, mk, mv, mi, mf), mo, qa, kv, ag, gm = _mixer_inputs(xs, w_in[l], b_in[l])
        h_m, C, n, m = _mlstm_chunk(mq, mk, mv, mi, mf, state_mlstm_C[l].astype(jnp.float32),
                                    state_mlstm_n[l].astype(jnp.float32), state_mlstm_m[l].astype(jnp.float32))
        past = cache_kv[l][page_table].reshape(DEC_BATCH, n_pages * PAGE_SIZE, 4, HKV, HD)
        full = jnp.concatenate([past, kv[:, :, :4].astype(past.dtype)], axis=1)
        o_cmp, o_slc = _cmp_slc_attend(qa, pos_s, full[:, :, 0], full[:, :, 1], full[:, :, 2], full[:, :, 3],
                                       w_cmp1[l], w_cmp2[l], rel_bias)
        win = jnp.concatenate([state_win_kv[l], kv[:, :, 4:].astype(state_win_kv.dtype)], axis=1)
        o_win = _window_sample(qa, win[:, :, 0], win[:, :, 1], rel_bias)
        xs = _merge(xs, h_m, mo, o_cmp, o_slc, o_win, ag, gm, mh_norm_g[l], w_branch[l], w_out[l], ln1_g[l], ln1_b[l])
        xs = _layer_norm(ALPHA * xs + _moe(xs, w_router, router_bias, w_gate_e[l], w_up_e[l], w_down_e[l]),
                         ln2_g[l], ln2_b[l])
        kv_s.append(kv[:, :, :4])
        win_s.append(win[:, -win_buf:])
        C_s.append(C)
        n_s.append(n)
        m_s.append(m)
    return (xp, xs, jnp.stack(kv_p), jnp.stack(win_p), jnp.stack(C_p), jnp.stack(n_p), jnp.stack(m_p),
            jnp.stack(kv_s), jnp.stack(win_s), jnp.stack(C_s), jnp.stack(n_s), jnp.stack(m_s))
```

```python
import functools
import math

import numpy as np
import jax
import jax.numpy as jnp
from jax import lax
from jax.experimental import pallas as pl
from jax.experimental.pallas import tpu as pltpu

F32 = jnp.float32
BF16 = jnp.bfloat16
HIGHEST = lax.Precision.HIGHEST

H_M, DH_M = 4, 128
D_M = H_M * DH_M
HQ, HKV, HD = 8, 2, 64
REP = HQ // HKV
D_A = HQ * HD
CMP_LEN, CMP_STRIDE, CMP_HID = 32, 16, 256
SLC_LEN, N_SEL, WINDOW = 64, 16, 512
N_BUCKETS, REL_MAX_DIST = 32, 128
N_EXP, N_GROUPS, TOP_K, D_EXP = 16, 4, 2, 256
EXP_PER_GROUP = N_EXP // N_GROUPS
LN_EPS = 1e-5
NEG = -1e30

LANES = 128
QT = 128
KT = 256
NSP = 64
NQS = 8
VMEM_LIMIT = 56 * 1024 * 1024


def _cparams(*sem):
    return pltpu.CompilerParams(dimension_semantics=sem, vmem_limit_bytes=VMEM_LIMIT)


def _dot(a, b):
    return jnp.dot(a.astype(BF16), b.astype(BF16), preferred_element_type=F32)


def _dot_nt(a, b):
    return lax.dot_general(a.astype(BF16), b.astype(BF16), (((1,), (1,)), ((), ())), preferred_element_type=F32)


def _dot_tn(a, b):
    return lax.dot_general(a.astype(BF16), b.astype(BF16), (((0,), (0,)), ((), ())), preferred_element_type=F32)


def _dot_f32(a, b):
    return jnp.dot(a, b, precision=HIGHEST, preferred_element_type=F32)


def _dot_nt_f32(a, b):
    return lax.dot_general(a, b, (((1,), (1,)), ((), ())), precision=HIGHEST, preferred_element_type=F32)


def _pick_tile(n, cands):
    for c in cands:
        if n % c == 0:
            return c
    raise ValueError(f"no tile for {n}")


def _layer_norm(y, g, b):
    mu = jnp.mean(y, axis=-1, keepdims=True)
    yc = y - mu
    var = jnp.mean(yc * yc, axis=-1, keepdims=True)
    return yc * lax.rsqrt(var + LN_EPS) * g + b


def _inproj_kernel(x_ref, *refs):
    n = len(refs) // 3
    x = x_ref[...].astype(BF16)
    for w_ref, b_ref, o_ref in zip(refs[:n], refs[n:2 * n], refs[2 * n:]):
        o_ref[...] = jnp.dot(x, w_ref[...], preferred_element_type=F32) + b_ref[...]


def _inproj(x, ws, bs):
    n_tok, d = x.shape
    tm = _pick_tile(n_tok, (256, 128, 64, 32, 16, 8))
    full = lambda i: (0, 0)
    row = lambda i: (i, 0)
    return pl.pallas_call(
        _inproj_kernel,
        grid=(n_tok // tm,),
        in_specs=[pl.BlockSpec((tm, d), row)]
        + [pl.BlockSpec(w.shape, full) for w in ws]
        + [pl.BlockSpec(b.shape, full) for b in bs],
        out_specs=[pl.BlockSpec((tm, w.shape[1]), row) for w in ws],
        out_shape=[jax.ShapeDtypeStruct((n_tok, w.shape[1]), F32) for w in ws],
        compiler_params=_cparams("parallel"),
    )(x, *ws, *bs)


def _log_sigmoid(x):
    return jnp.minimum(x, 0.0) - jnp.log(1.0 + jnp.exp(-jnp.abs(x)))


def _mlstm_kernel(q_ref, k_ref, v_ref, o_ref, g_ref, ng_ref, c0_ref, n0_ref, m0_ref,
                  mout_ref, c_ref, n_ref, m_ref, cs, ns, ms, *, L, l_valid):
    c = pl.program_id(1)

    @pl.when(c == 0)
    def _():
        cs[...] = c0_ref[0]
        ns[...] = n0_ref[0]
        ms[...] = m0_ref[0]

    g = g_ref[...]
    row = lax.broadcasted_iota(jnp.int32, (L, L), 0)
    col = lax.broadcasted_iota(jnp.int32, (L, L), 1)
    causal = row >= col
    fcum = _dot_f32(causal.astype(F32), _log_sigmoid(g))
    lane = lax.broadcasted_iota(jnp.int32, (L, LANES), 1)
    y = jnp.where(lane < H_M, g - pltpu.roll(fcum, LANES - H_M, 1), fcum)
    yt = y.T
    rowv = lax.broadcasted_iota(jnp.int32, (L, 1), 0)
    r = l_valid - 1
    for h in range(H_M):
        hs = slice(h * DH_M, (h + 1) * DH_M)
        f_col = fcum[:, H_M + h:H_M + h + 1]
        a_row = yt[h:h + 1, :]
        i_col = g[:, h:h + 1]
        m0 = ms[h][:, 0:1]
        n0 = ns[h]
        c0 = cs[h]
        dm = jnp.where(causal, f_col + a_row, NEG)
        b = f_col + m0
        mrow = jnp.maximum(b, jnp.max(dm, axis=1, keepdims=True))
        w = jnp.exp(dm - mrow)
        dec = jnp.exp(b - mrow)
        q = q_ref[:, hs]
        k = k_ref[:, hs] * (DH_M ** -0.5)
        v = v_ref[:, hs]
        s = _dot_nt(q, k) * w
        num = dec * _dot_nt(q, c0) + _dot(s, v)
        den = dec * jnp.sum(q * n0, axis=1, keepdims=True) + jnp.sum(s, axis=1, keepdims=True)
        hh = num / jnp.maximum(jnp.abs(den), jnp.exp(-mrow))
        mu = jnp.mean(hh, axis=1, keepdims=True)
        hc = hh - mu
        var = jnp.mean(hc * hc, axis=1, keepdims=True)
        hn = hc * lax.rsqrt(var + LN_EPS) * ng_ref[:, hs]
        mout_ref[:, hs] = (hn * jax.nn.sigmoid(o_ref[:, hs])).astype(mout_ref.dtype)
        f_r = fcum[r:r + 1, H_M + h:H_M + h + 1]
        m_r = mrow[r:r + 1, :]
        w_last = jnp.where(rowv <= r, jnp.exp(f_r - f_col + i_col - m_r), 0.0)
        d_last = dec[r:r + 1, :]
        cs[h] = d_last * c0 + _dot_tn(v * w_last, k)
        ns[h] = d_last * n0 + jnp.sum(k * w_last, axis=0, keepdims=True)
        ms[h] = jnp.broadcast_to(m_r, (1, LANES))

    @pl.when(c == pl.num_programs(1) - 1)
    def _():
        c_ref[0] = cs[...]
        n_ref[0] = ns[...]
        m_ref[0] = ms[...]


def _mlstm(zm, zg, norm_g, c0, n0, m0, *, n_seq, L, l_valid):
    n_tok = zm.shape[0]
    t = n_tok // n_seq
    nc = t // L
    n0 = n0.reshape(n_seq, H_M, 1, DH_M)
    m0 = jnp.broadcast_to(m0.reshape(n_seq, H_M, 1, 1), (n_seq, H_M, 1, LANES))
    colblk = lambda j: (lambda b, c: (b * nc + c, j))
    st4 = lambda b, c: (b, 0, 0, 0)
    mout, c_new, n_new, m_new = pl.pallas_call(
        functools.partial(_mlstm_kernel, L=L, l_valid=l_valid),
        grid=(n_seq, nc),
        in_specs=[pl.BlockSpec((L, D_M), colblk(0)), pl.BlockSpec((L, D_M), colblk(1)),
                  pl.BlockSpec((L, D_M), colblk(2)), pl.BlockSpec((L, D_M), colblk(3)),
                  pl.BlockSpec((L, LANES), colblk(0)),
                  pl.BlockSpec((1, D_M), lambda b, c: (0, 0)),
                  pl.BlockSpec((1, H_M, DH_M, DH_M), st4),
                  pl.BlockSpec((1, H_M, 1, DH_M), st4),
                  pl.BlockSpec((1, H_M, 1, LANES), st4)],
        out_specs=[pl.BlockSpec((L, D_M), colblk(0)),
                   pl.BlockSpec((1, H_M, DH_M, DH_M), st4),
                   pl.BlockSpec((1, H_M, 1, DH_M), st4),
                   pl.BlockSpec((1, H_M, 1, LANES), st4)],
        out_shape=[jax.ShapeDtypeStruct((n_tok, D_M), BF16),
                   jax.ShapeDtypeStruct((n_seq, H_M, DH_M, DH_M), F32),
                   jax.ShapeDtypeStruct((n_seq, H_M, 1, DH_M), F32),
                   jax.ShapeDtypeStruct((n_seq, H_M, 1, LANES), F32)],
        scratch_shapes=[pltpu.VMEM((H_M, DH_M, DH_M), F32), pltpu.VMEM((H_M, 1, DH_M), F32),
                        pltpu.VMEM((H_M, 1, LANES), F32)],
        compiler_params=_cparams("parallel", "arbitrary"),
    )(zm, zm, zm, zm, zg, norm_g.reshape(1, D_M), c0, n0, m0)
    return mout, c_new, n_new.reshape(n_seq, H_M, DH_M), m_new[:, :, 0, 0]


def _cmp_partial_kernel(x_ref, w_ref, o_ref):
    @pl.when(pl.program_id(2) == 0)
    def _():
        o_ref[...] = jnp.zeros_like(o_ref)

    o_ref[0] += _dot(x_ref[...], w_ref[0, 0])


def _cmp_partial(rows16, w1p):
    n_half = rows16.shape[0]
    tm = _pick_tile(n_half, (2048, 1024, 512, 256, 128, 64, 32, 16, 8))
    hid2 = 2 * HKV * CMP_HID
    return pl.pallas_call(
        _cmp_partial_kernel,
        grid=(2, n_half // tm, CMP_STRIDE),
        in_specs=[pl.BlockSpec((tm, LANES), lambda s, i, r: (i, 4 * r + s)),
                  pl.BlockSpec((1, 1, LANES, hid2), lambda s, i, r: (s, r, 0, 0))],
        out_specs=pl.BlockSpec((1, tm, hid2), lambda s, i, r: (s, i, 0)),
        out_shape=jax.ShapeDtypeStruct((2, n_half, hid2), F32),
        compiler_params=_cparams("parallel", "parallel", "arbitrary"),
    )(rows16, w1p)


def _gelu_tanh(x):
    return 0.5 * x * (1.0 + jnp.tanh(math.sqrt(2.0 / math.pi) * (x + 0.044715 * (x * x * x))))


def _cmp_finish_kernel(*refs, n_in):
    p_refs = refs[-(n_in + 2):-2]
    w2_ref, o_ref = refs[-2], refs[-1]
    half = HKV * CMP_HID
    for s in range(2):
        p = jnp.concatenate([pr[s] for pr in p_refs], axis=0) if n_in > 1 else p_refs[0][s]
        n = p.shape[0]
        first, second = p[:, :half], p[:, half:]
        hid = _gelu_tanh(first + pltpu.roll(second, n - 1, 0))
        o_ref[0, s] = _dot(hid, w2_ref[s])


def _cmp_finish_prompt(part, w2p, n_seq):
    n_half = part.shape[1] // n_seq
    hid2 = part.shape[2]
    return pl.pallas_call(
        functools.partial(_cmp_finish_kernel, n_in=1),
        grid=(n_seq,),
        in_specs=[pl.BlockSpec((2, n_half, hid2), lambda b: (0, b, 0)),
                  pl.BlockSpec(w2p.shape, lambda b: (0, 0, 0))],
        out_specs=pl.BlockSpec((1, 2, n_half, LANES), lambda b: (b, 0, 0, 0)),
        out_shape=jax.ShapeDtypeStruct((n_seq, 2, n_half, LANES), F32),
        compiler_params=_cparams("parallel"),
    )(part, w2p)


def _cmp_finish_paged(part, w2p, page_table, page_base, half_per_page):
    n_seq, n_pages = page_table.shape
    hid2 = part.shape[2]
    n_half = n_pages * half_per_page
    page_spec = lambda p: pl.BlockSpec((2, half_per_page, hid2), lambda b, pt: (0, page_base + pt[b, p], 0))
    return pl.pallas_call(
        functools.partial(_cmp_finish_kernel, n_in=n_pages),
        grid_spec=pltpu.PrefetchScalarGridSpec(
            num_scalar_prefetch=1, grid=(n_seq,),
            in_specs=[page_spec(p) for p in range(n_pages)] + [pl.BlockSpec(w2p.shape, lambda b, pt: (0, 0, 0))],
            out_specs=pl.BlockSpec((1, 2, n_half, LANES), lambda b, pt: (b, 0, 0, 0))),
        out_shape=jax.ShapeDtypeStruct((n_seq, 2, n_half, LANES), F32),
        compiler_params=_cparams("parallel"),
    )(page_table, *([part] * n_pages), w2p)


def _cmp_branch(q, ck, cv, bias_ref, g, ocmp_ref):
    gs = slice(g * HD, (g + 1) * HD)
    ck_g, cv_g = ck[:, gs], cv[:, gs]
    psum = None
    for r in range(REP):
        h = g * REP + r
        bias = bias_ref[h]
        s = _dot_nt_f32(q[:, h * HD:(h + 1) * HD], ck_g) + bias
        e = jnp.exp(s - jnp.max(s, axis=1, keepdims=True))
        p = jnp.where(bias > 0.5 * NEG, e / jnp.sum(e, axis=1, keepdims=True), 0.0)
        ocmp_ref[:, h * HD:(h + 1) * HD] = _dot(p, cv_g)
        psum = p if psum is None else psum + p
    return psum


def _select_blocks(imp, qpos, blk_axis):
    blk = lax.broadcasted_iota(jnp.int32, imp.shape, blk_axis)
    cur = (qpos // SLC_LEN) == blk
    avail = blk * SLC_LEN <= qpos
    imp = jnp.where(cur, -NEG, jnp.where(avail, imp, NEG))
    cnt = jnp.zeros(imp.shape, F32)
    for j in range(imp.shape[blk_axis]):
        other = lax.slice_in_dim(imp, j, j + 1, axis=blk_axis)
        cnt = cnt + jnp.where(blk > j, jnp.where(other >= imp, 1.0, 0.0), jnp.where(other > imp, 1.0, 0.0))
    return jnp.where((cnt < N_SEL) & (imp > 0.5 * NEG), 1.0, 0.0)


def _padded_queries(q, g):
    nq = q.shape[0]
    zero = jnp.zeros((nq, HD), F32)
    parts = []
    for r in range(REP):
        h = g * REP + r
        piece = q[:, h * HD:(h + 1) * HD]
        parts.append(jnp.concatenate([piece, zero] if g == 0 else [zero, piece], axis=1))
    return jnp.concatenate(parts, axis=0).astype(BF16)


def _store_heads(o_ref, o, g, nq):
    for r in range(REP):
        h = g * REP + r
        o_ref[:, h * HD:(h + 1) * HD] = o[r * nq:(r + 1) * nq, g * HD:(g + 1) * HD]


def _nsa_cmp_prompt_kernel(q_ref, ckv_ref, bias_ref, ovt_ref, ocmp_ref, sel_ref):
    qb = pl.program_id(1)
    q = q_ref[...] * (HD ** -0.5)
    ck, cv = ckv_ref[0, 0], ckv_ref[0, 1]
    qpos = qb * QT + lax.broadcasted_iota(jnp.int32, (1, QT), 1)
    for g in range(HKV):
        psum = _cmp_branch(q, ck, cv, bias_ref, g, ocmp_ref)
        imp_t = _dot_nt_f32(ovt_ref[...], psum)
        sel_ref[:, g * NSP:(g + 1) * NSP] = _select_blocks(imp_t, qpos, 0).T


def _nsa_cmp_prompt(zq, ckv, bias_cmp, ov_t, n_seq):
    n_tok = zq.shape[0]
    nqb = n_tok // n_seq // QT
    ncp = ckv.shape[2]
    tok = lambda b, i: (b * nqb + i, 0)
    return pl.pallas_call(
        _nsa_cmp_prompt_kernel,
        grid=(n_seq, nqb),
        in_specs=[pl.BlockSpec((QT, D_A), tok),
                  pl.BlockSpec((1, 2, ncp, LANES), lambda b, i: (b, 0, 0, 0)),
                  pl.BlockSpec((HQ, QT, ncp), lambda b, i: (0, i, 0)),
                  pl.BlockSpec(ov_t.shape, lambda b, i: (0, 0))],
        out_specs=[pl.BlockSpec((QT, D_A), tok), pl.BlockSpec((QT, HKV * NSP), tok)],
        out_shape=[jax.ShapeDtypeStruct((n_tok, D_A), F32), jax.ShapeDtypeStruct((n_tok, HKV * NSP), F32)],
        compiler_params=_cparams("parallel", "parallel"),
    )(zq, ckv, bias_cmp, ov_t)


def _flash_prompt_kernel(*refs, window, n_delta):
    if window:
        q_ref, k_ref, v_ref, bias_ref, o_ref = refs
        sel_ref = None
    else:
        q_ref, k_ref, v_ref, bias_ref, sel_ref, o_ref = refs
    qb = pl.program_id(1)
    q = q_ref[...] * (HD ** -0.5)
    hi = (qb * QT) // KT + 1
    lo = jnp.maximum(qb * QT - (WINDOW - 1), 0) // KT if window else 0
    for g in range(HKV):
        qpad = _padded_queries(q, g)
        sel_g = None if window else sel_ref[...].astype(BF16)

        def body(kt, carry, g=g, qpad=qpad, sel_g=sel_g):
            m, l, acc = carry
            k0 = pl.multiple_of(kt * KT, KT)
            k = k_ref[pl.ds(k0, KT), :]
            v = v_ref[pl.ds(k0, KT), :]
            delta = jnp.minimum(qb - (KT // QT) * kt, n_delta - 1)
            bias = bias_ref[delta, pl.ds(g * REP, REP)].reshape(REP * QT, KT)
            s = _dot_nt(qpad, k) + bias
            if not window:
                blk = lax.broadcasted_iota(jnp.int32, (HKV * NSP, KT), 0)
                key = lax.broadcasted_iota(jnp.int32, (HKV * NSP, KT), 1)
                expand = jnp.where(blk == g * NSP + (k0 + key) // SLC_LEN, 1.0, 0.0).astype(BF16)
                picked = jnp.dot(sel_g, expand, preferred_element_type=F32)
                s = jnp.where(jnp.concatenate([picked] * REP, axis=0) > 0.5, s, NEG)
            m_new = jnp.maximum(m, jnp.max(s, axis=1, keepdims=True))
            alpha = jnp.exp(m - m_new)
            p = jnp.exp(s - m_new)
            l = alpha * l + jnp.sum(p, axis=1, keepdims=True)
            acc = alpha * acc + _dot(p, v)
            return m_new, l, acc

        init = (jnp.full((REP * QT, 1), NEG, F32), jnp.zeros((REP * QT, 1), F32), jnp.zeros((REP * QT, LANES), F32))
        m, l, acc = lax.fori_loop(lo, hi, body, init)
        _store_heads(o_ref, acc / l, g, QT)


def _flash_prompt(zq, kv, kcol, vcol, bias, sel, n_seq, *, window):
    n_tok = zq.shape[0]
    t = n_tok // n_seq
    nqb = t // QT
    tok = lambda b, i: (b * nqb + i, 0)
    in_specs = [pl.BlockSpec((QT, D_A), tok),
                pl.BlockSpec((t, LANES), lambda b, i: (b, kcol)),
                pl.BlockSpec((t, LANES), lambda b, i: (b, vcol)),
                pl.BlockSpec(bias.shape, lambda b, i: (0, 0, 0, 0))]
    args = [zq, kv, kv, bias]
    if not window:
        in_specs.append(pl.BlockSpec((QT, HKV * NSP), tok))
        args.append(sel)
    return pl.pallas_call(
        functools.partial(_flash_prompt_kernel, window=window, n_delta=bias.shape[0]),
        grid=(n_seq, nqb),
        in_specs=in_specs,
        out_specs=pl.BlockSpec((QT, D_A), tok),
        out_shape=jax.ShapeDtypeStruct((n_tok, D_A), F32),
        compiler_params=_cparams("parallel", "parallel"),
    )(*args)


def _nsa_sample_kernel(*refs, n_pages, page, nq, past):
    pt_ref = refs[0]
    page_refs = refs[1:1 + n_pages]
    (q_ref, kvn_ref, wn_ref, wbuf_ref, ckv_ref, bcmp_ref, bslc_ref, bwin_ref, ov_ref, exp_ref,
     ocmp_ref, oslc_ref, owin_ref, ks, vs, kw, vw) = refs[1 + n_pages:]
    del pt_ref
    win_buf = wbuf_ref.shape[1]
    for p, pr in enumerate(page_refs):
        ks[p * page:(p + 1) * page, :] = pr[0, :, 2 * LANES:3 * LANES]
        vs[p * page:(p + 1) * page, :] = pr[0, :, 3 * LANES:4 * LANES]
    ks[past + nq:, :] = jnp.zeros((LANES - nq, LANES), F32)
    vs[past + nq:, :] = jnp.zeros((LANES - nq, LANES), F32)
    ks[past:past + nq, :] = kvn_ref[0, :, 2 * LANES:3 * LANES]
    vs[past:past + nq, :] = kvn_ref[0, :, 3 * LANES:4 * LANES]
    kw[:win_buf, :] = wbuf_ref[0, :, :LANES]
    vw[:win_buf, :] = wbuf_ref[0, :, LANES:]
    kw[win_buf + nq:, :] = jnp.zeros((LANES - nq, LANES), F32)
    vw[win_buf + nq:, :] = jnp.zeros((LANES - nq, LANES), F32)
    kw[win_buf:win_buf + nq, :] = wn_ref[0, :, :LANES]
    vw[win_buf:win_buf + nq, :] = wn_ref[0, :, LANES:]

    q = q_ref[0] * (HD ** -0.5)
    ck, cv = ckv_ref[0, 0], ckv_ref[0, 1]
    qpos = past + lax.broadcasted_iota(jnp.int32, (nq, 1), 0)

    def attend(qpad, k, v, bias):
        s = _dot_nt(qpad, k) + bias
        e = jnp.exp(s - jnp.max(s, axis=1, keepdims=True))
        return _dot(e, v) / jnp.sum(e, axis=1, keepdims=True)

    for g in range(HKV):
        psum = _cmp_branch(q, ck, cv, bcmp_ref, g, ocmp_ref.at[0])
        sel = _select_blocks(_dot_f32(psum, ov_ref[...]), qpos, 1)
        picked = jnp.dot(sel.astype(BF16), exp_ref[...], preferred_element_type=F32)
        qpad = _padded_queries(q, g)
        bslc = bslc_ref[g * REP:(g + 1) * REP].reshape(REP * nq, -1)
        bslc = jnp.where(jnp.concatenate([picked] * REP, axis=0) > 0.5, bslc, NEG)
        _store_heads(oslc_ref.at[0], attend(qpad, ks[...], vs[...], bslc), g, nq)
        bwin = bwin_ref[g * REP:(g + 1) * REP].reshape(REP * nq, -1)
        _store_heads(owin_ref.at[0], attend(qpad, kw[...], vw[...], bwin), g, nq)


def _nsa_sample(zq, kvn, wn, cache_pages, page_base, page_table, win_state, win_base, ckv,
                bias_cmp, bias_slc, bias_win, ov, expand):
    n_seq, nq, _ = zq.shape
    n_pages = page_table.shape[1]
    page = cache_pages.shape[1]
    past = n_pages * page
    win_buf = win_state.shape[1]
    ncp = ckv.shape[2]
    seq3 = lambda b, pt: (b, 0, 0)
    const = lambda nd: (lambda b, pt: (0,) * nd)
    page_spec = lambda p: pl.BlockSpec((1, page, 4 * LANES), lambda b, pt: (page_base + pt[b, p], 0, 0))
    in_specs = [page_spec(p) for p in range(n_pages)] + [
        pl.BlockSpec((1, nq, D_A), seq3),
        pl.BlockSpec((1, nq, 4 * LANES), seq3),
        pl.BlockSpec((1, nq, 2 * LANES), seq3),
        pl.BlockSpec((1, win_buf, 2 * LANES), lambda b, pt: (win_base + b, 0, 0)),
        pl.BlockSpec((1, 2, ncp, LANES), lambda b, pt: (b, 0, 0, 0)),
        pl.BlockSpec(bias_cmp.shape, const(3)),
        pl.BlockSpec(bias_slc.shape, const(3)),
        pl.BlockSpec(bias_win.shape, const(3)),
        pl.BlockSpec(ov.shape, const(2)),
        pl.BlockSpec(expand.shape, const(2))]
    out = pl.BlockSpec((1, nq, D_A), seq3)
    return pl.pallas_call(
        functools.partial(_nsa_sample_kernel, n_pages=n_pages, page=page, nq=nq, past=past),
        grid_spec=pltpu.PrefetchScalarGridSpec(
            num_scalar_prefetch=1, grid=(n_seq,), in_specs=in_specs, out_specs=[out, out, out],
            scratch_shapes=[pltpu.VMEM((past + LANES, LANES), F32), pltpu.VMEM((past + LANES, LANES), F32),
                            pltpu.VMEM((win_buf + LANES, LANES), F32), pltpu.VMEM((win_buf + LANES, LANES), F32)]),
        out_shape=[jax.ShapeDtypeStruct((n_seq, nq, D_A), F32)] * 3,
        compiler_params=_cparams("parallel"),
    )(page_table, *([cache_pages] * n_pages), zq, kvn, wn, win_state, ckv, bias_cmp, bias_slc, bias_win, ov, expand)


def _merge_kernel(x_ref, mout_ref, ocmp_ref, oslc_ref, owin_ref, zg_ref, zgm_ref, gexp_ref, wb_ref, wo_ref,
                  lg_ref, lb_ref, o_ref, *, alpha):
    d = x_ref.shape[1]
    gate = jax.nn.sigmoid(zg_ref[...])
    a = None
    for br, src in enumerate((ocmp_ref, oslc_ref, owin_ref)):
        term = _dot_f32(gate, gexp_ref[br]) * src[...]
        a = term if a is None else a + term
    u = (jax.nn.sigmoid(zgm_ref[:, :d]) * _dot(mout_ref[...], wb_ref[0])
         + jax.nn.sigmoid(zgm_ref[:, d:]) * _dot(a, wb_ref[1]))
    y = _dot(u, wo_ref[...])
    o_ref[...] = _layer_norm(alpha * x_ref[...] + y, lg_ref[...], lb_ref[...])


def _merge(x, mout, ocmp, oslc, owin, zg, zgm, gexp, wb, wo, lg, lb, alpha):
    n_tok, d = x.shape
    tm = _pick_tile(n_tok, (256, 128, 64, 32, 16, 8))
    row = lambda i: (i, 0)
    c2 = lambda i: (0, 0)
    c3 = lambda i: (0, 0, 0)
    return pl.pallas_call(
        functools.partial(_merge_kernel, alpha=alpha),
        grid=(n_tok // tm,),
        in_specs=[pl.BlockSpec((tm, d), row), pl.BlockSpec((tm, D_M), row), pl.BlockSpec((tm, D_A), row),
                  pl.BlockSpec((tm, D_A), row), pl.BlockSpec((tm, D_A), row), pl.BlockSpec((tm, LANES), row),
                  pl.BlockSpec((tm, 2 * d), row), pl.BlockSpec(gexp.shape, c3), pl.BlockSpec(wb.shape, c3),
                  pl.BlockSpec(wo.shape, c2), pl.BlockSpec((1, d), c2), pl.BlockSpec((1, d), c2)],
        out_specs=pl.BlockSpec((tm, d), row),
        out_shape=jax.ShapeDtypeStruct((n_tok, d), F32),
        compiler_params=_cparams("parallel"),
    )(x, mout, ocmp, oslc, owin, zg, zgm, gexp, wb, wo, lg.reshape(1, d), lb.reshape(1, d))


def _route(aff, sel):
    def top2_sum(a, b, c, d):
        x, x2, y, y2 = jnp.maximum(a, b), jnp.minimum(a, b), jnp.maximum(c, d), jnp.minimum(c, d)
        return jnp.maximum(x, y) + jnp.maximum(jnp.minimum(x, y), jnp.maximum(x2, y2))

    gsum = [top2_sum(*sel[EXP_PER_GROUP * gi:EXP_PER_GROUP * (gi + 1)]) for gi in range(N_GROUPS)]
    gmax = functools.reduce(jnp.maximum, gsum)
    chosen, taken = [], None
    for gi in range(N_GROUPS):
        is_best = gsum[gi] == gmax if taken is None else (gsum[gi] == gmax) & jnp.logical_not(taken)
        taken = is_best if taken is None else taken | is_best
        members = range(EXP_PER_GROUP * gi, EXP_PER_GROUP * (gi + 1))
        for e in members:
            ahead = None
            for e2 in members:
                if e2 == e:
                    continue
                before = (sel[e2] >= sel[e]) if e2 < e else (sel[e2] > sel[e])
                cnt = jnp.where(before, 1.0, 0.0)
                ahead = cnt if ahead is None else ahead + cnt
            chosen.append(is_best & (ahead < TOP_K))
    picked = [jnp.where(c, a, 0.0) for c, a in zip(chosen, aff)]
    total = functools.reduce(lambda a, b: a + b, picked)
    return [p / total for p in picked]


def _moe_kernel(x_ref, wr_ref, rb_ref, wg_ref, wu_ref, wd_ref, lg_ref, lb_ref, o_ref, xb, comb, acc, *, alpha):
    e = pl.program_id(1)
    tm = x_ref.shape[0]
    lane = lax.broadcasted_iota(jnp.int32, (tm, LANES), 1)

    @pl.when(e == 0)
    def _():
        x = x_ref[...]
        xb[...] = x.astype(BF16)
        aff_all = jax.nn.sigmoid(_dot_f32(x, wr_ref[...]))
        sel_all = aff_all + rb_ref[...]
        aff = [aff_all[:, i:i + 1] for i in range(N_EXP)]
        sel = [sel_all[:, i:i + 1] for i in range(N_EXP)]
        weights = _route(aff, sel)
        c = jnp.zeros((tm, LANES), F32)
        for i, w in enumerate(weights):
            c = jnp.where(lane == i, w, c)
        comb[...] = c
        acc[...] = jnp.zeros_like(acc)

    x16 = xb[...]
    w_e = jnp.sum(jnp.where(lane == e, comb[...], 0.0), axis=1, keepdims=True)
    hg = jnp.dot(x16, wg_ref[0, 0], preferred_element_type=F32)
    hu = jnp.dot(x16, wu_ref[0, 0], preferred_element_type=F32)
    h = hg * jax.nn.sigmoid(hg) * hu * w_e
    acc[...] += _dot(h, wd_ref[0, 0])

    @pl.when(e == pl.num_programs(1) - 1)
    def _():
        o_ref[...] = _layer_norm(alpha * x_ref[...] + acc[...], lg_ref[...], lb_ref[...])


def _moe(x, wr, rb, wg, wu, wd, layer, lg, lb, alpha):
    n_tok, d = x.shape
    tm = _pick_tile(n_tok, (512, 256, 128, 64, 32, 16, 8))
    row = lambda i, e: (i, 0)
    c2 = lambda i, e: (0, 0)
    wsel = lambda i, e: (layer, e, 0, 0)
    return pl.pallas_call(
        functools.partial(_moe_kernel, alpha=alpha),
        grid=(n_tok // tm, N_EXP),
        in_specs=[pl.BlockSpec((tm, d), row), pl.BlockSpec(wr.shape, c2), pl.BlockSpec(rb.shape, c2),
                  pl.BlockSpec((1, 1, d, D_EXP), wsel), pl.BlockSpec((1, 1, d, D_EXP), wsel),
                  pl.BlockSpec((1, 1, D_EXP, d), wsel), pl.BlockSpec((1, d), c2), pl.BlockSpec((1, d), c2)],
        out_specs=pl.BlockSpec((tm, d), row),
        out_shape=jax.ShapeDtypeStruct((n_tok, d), F32),
        scratch_shapes=[pltpu.VMEM((tm, d), BF16), pltpu.VMEM((tm, LANES), F32), pltpu.VMEM((tm, d), F32)],
        compiler_params=_cparams("parallel", "arbitrary"),
    )(x, wr, rb, wg, wu, wd, lg.reshape(1, d), lb.reshape(1, d))


def _bucket_np(dist):
    n = np.maximum(dist, 0)
    exact = N_BUCKETS // 2
    nf = np.maximum(n, 1).astype(np.float32)
    large = exact + (np.log(nf / np.float32(exact)) / np.float32(math.log(REL_MAX_DIST / exact))
                     * np.float32(N_BUCKETS - exact)).astype(np.int32)
    return np.where(n < exact, n, np.minimum(large, N_BUCKETS - 1)).astype(np.int32)


def _bias_table(rel_bias, dist, valid):
    tab = rel_bias.astype(F32)[jnp.asarray(_bucket_np(dist))]
    tab = jnp.where(jnp.asarray(valid)[..., None], tab, NEG)
    return jnp.moveaxis(tab, -1, 0)


def _overlap_np(ncp, n_cmp, n_slc):
    c0 = np.arange(ncp)[:, None] * CMP_STRIDE
    j0 = np.arange(NSP)[None, :] * SLC_LEN
    ov = (c0 < j0 + SLC_LEN) & (c0 + CMP_LEN > j0)
    ov &= (np.arange(ncp)[:, None] < n_cmp) & (np.arange(NSP)[None, :] < n_slc)
    return ov.astype(np.float32)


def kernel(x_prompt, x_sample, cache_kv, state_win_kv, state_mlstm_C, state_mlstm_n, state_mlstm_m, page_table,
           w_in, b_in, w_cmp1, w_cmp2, mh_norm_g, w_branch, w_out, ln1_g, ln1_b, ln2_g, ln2_b,
           w_router, router_bias, w_gate_e, w_up_e, w_down_e, rel_bias):
    B, T, D = x_prompt.shape
    DB, TS, _ = x_sample.shape
    depth, n_pool, page = cache_kv.shape[:3]
    n_pages = page_table.shape[1]
    past = n_pages * page
    win_buf = state_win_kv.shape[2]
    alpha = (2 * depth) ** 0.25
    assert T % KT == 0 and T // SLC_LEN <= NSP and page % CMP_STRIDE == 0 and past % SLC_LEN == 0
    assert win_buf == min(WINDOW, past) and T >= win_buf

    off = np.cumsum((0, D_M, D_M, D_M, D_M, H_M, H_M, D_A, 6 * HKV * HD, 3 * HQ, 2 * D)).tolist()
    seg = lambda a, i, j: a[..., off[i]:off[j]]
    small = lambda a: jnp.concatenate(
        [seg(a, 4, 6), seg(a, 8, 9), jnp.zeros(a.shape[:-1] + (LANES - 2 * H_M - 3 * HQ,), a.dtype)], axis=-1)
    kv_mid = off[7] + 4 * HKV * HD
    groups = lambda a: (seg(a, 0, 4), small(a), seg(a, 6, 7), a[..., off[7]:kv_mid], a[..., kv_mid:off[8]], seg(a, 9, 10))
    w_groups = [w.astype(BF16) for w in groups(w_in)]
    b_groups = [b[:, None, :] for b in groups(b_in)]
    wb16, wo16 = w_branch.astype(BF16), w_out.astype(BF16)
    wg16, wu16, wd16 = w_gate_e.astype(BF16), w_up_e.astype(BF16), w_down_e.astype(BF16)
    wr_pad = jnp.pad(w_router, ((0, 0), (0, LANES - N_EXP)))
    rb_pad = jnp.pad(router_bias, (0, LANES - N_EXP)).reshape(1, LANES)
    eye = jnp.eye(HKV, dtype=F32)
    w1 = w_cmp1.reshape(depth, 2, 2, CMP_STRIDE, HD, CMP_HID)
    w1p = jnp.einsum('lshrdf,gG->lsrgdhGf', w1, eye).reshape(depth, 2, CMP_STRIDE, HKV * HD, 2 * HKV * CMP_HID)
    w1p = w1p.astype(BF16)
    w2p = jnp.einsum('lsfd,gG->lsgfGd', w_cmp2, eye).reshape(depth, 2, HKV * CMP_HID, HKV * HD).astype(BF16)
    gexp = np.zeros((3, LANES, D_A), np.float32)
    for br in range(3):
        for h in range(HQ):
            gexp[br, 2 * H_M + br * HQ + h, h * HD:(h + 1) * HD] = 1.0
    gexp = jnp.asarray(gexp)

    ncp_p, n_cmp_p = T // CMP_STRIDE, (T - CMP_LEN) // CMP_STRIDE + 1
    tq = np.arange(T)[:, None]
    d_cmp = tq - (np.arange(ncp_p)[None, :] * CMP_STRIDE + CMP_LEN - 1)
    bias_cmp_p = _bias_table(rel_bias, d_cmp, (d_cmp >= 0) & (np.arange(ncp_p)[None, :] < n_cmp_p))
    ov_t_p = jnp.asarray(_overlap_np(ncp_p, n_cmp_p, -(-T // SLC_LEN)).T)
    qi, kj = np.arange(QT)[None, :, None], np.arange(KT)[None, None, :]
    d_slc = np.arange(4)[:, None, None] * QT + qi - kj
    bias_slc_p = jnp.moveaxis(_bias_table(rel_bias, d_slc, d_slc >= 0), 0, 1)
    d_win = np.arange(WINDOW // QT + 2)[:, None, None] * QT + qi - kj
    bias_win_p = jnp.moveaxis(_bias_table(rel_bias, d_win, (d_win >= 0) & (d_win < WINDOW)), 0, 1)

    ncp_s = past // CMP_STRIDE
    n_cmp_s = (past + TS - CMP_LEN) // CMP_STRIDE + 1
    assert n_cmp_s <= ncp_s and -(-(past + TS) // SLC_LEN) <= NSP and TS <= NQS
    qs = past + np.arange(NQS)[:, None]
    d_cmp_s = qs - (np.arange(ncp_s)[None, :] * CMP_STRIDE + CMP_LEN - 1)
    bias_cmp_s = _bias_table(rel_bias, d_cmp_s, (d_cmp_s >= 0) & (np.arange(ncp_s)[None, :] < n_cmp_s))
    ov_s = jnp.asarray(_overlap_np(ncp_s, n_cmp_s, -(-(past + TS) // SLC_LEN)))
    key_s = np.arange(past + LANES)[None, :]
    d_slc_s = qs - key_s
    bias_slc_s = _bias_table(rel_bias, d_slc_s, (d_slc_s >= 0) & (key_s < past + TS))
    expand_s = jnp.asarray((np.arange(NSP)[:, None] == key_s // SLC_LEN).astype(np.float32)).astype(BF16)
    idx_w = np.arange(win_buf + LANES)[None, :]
    d_win_s = qs - (past - win_buf + idx_w)
    bias_win_s = _bias_table(rel_bias, d_win_s, (d_win_s >= 0) & (d_win_s < WINDOW) & (idx_w < win_buf + TS))

    cache_pages = cache_kv.reshape(depth * n_pool, page, 4 * LANES)
    cache_rows16 = cache_kv.reshape(depth, n_pool * page // CMP_STRIDE, CMP_STRIDE * 4 * LANES)
    win_state = state_win_kv.reshape(depth * DB, win_buf, 2 * LANES)
    half_per_page = page // CMP_STRIDE

    LP = _pick_tile(T, (256, 128, 64))
    LS = 128
    zeros_state = (jnp.zeros((B, H_M, DH_M, DH_M), F32), jnp.zeros((B, H_M, DH_M), F32), jnp.zeros((B, H_M), F32))

    xp = x_prompt.reshape(B * T, D)
    xs = x_sample.reshape(DB * TS, D)
    outs = [[] for _ in range(10)]
    for l in range(depth):
        ws = [w[l] for w in w_groups]
        bs = [b[l] for b in b_groups]
        zm, zg, zq, zkv, zw, zgm = _inproj(xp, ws, bs)
        mout, c_p, n_p, m_p = _mlstm(zm, zg, mh_norm_g[l], *zeros_state, n_seq=B, L=LP, l_valid=LP)
        part = _cmp_partial(zkv.reshape(B * T // CMP_STRIDE, CMP_STRIDE * 4 * LANES), w1p[l])
        ckv = _cmp_finish_prompt(part, w2p[l], B)
        ocmp, sel = _nsa_cmp_prompt(zq, ckv, bias_cmp_p, ov_t_p, B)
        oslc = _flash_prompt(zq, zkv, 2, 3, bias_slc_p, sel, B, window=False)
        owin = _flash_prompt(zq, zw, 0, 1, bias_win_p, None, B, window=True)
        x1 = _merge(xp, mout, ocmp, oslc, owin, zg, zgm, gexp, wb16[l], wo16[l], ln1_g[l], ln1_b[l], alpha)
        xp = _moe(x1, wr_pad, rb_pad, wg16, wu16, wd16, l, ln2_g[l], ln2_b[l], alpha)
        outs[0].append(zkv.reshape(B, T, 4, HKV, HD))
        outs[1].append(zw.reshape(B, T, 2, HKV, HD)[:, T - win_buf:])
        outs[2].append(c_p)
        outs[3].append(n_p)
        outs[4].append(m_p)
        zm, zg, zq, zkv, zw, zgm = _inproj(xs, ws, bs)
        padt = lambda a: jnp.pad(a.reshape(DB, TS, -1), ((0, 0), (0, LS - TS), (0, 0))).reshape(DB * LS, -1)
        mout, c_s, n_s, m_s = _mlstm(padt(zm), padt(zg), mh_norm_g[l], state_mlstm_C[l], state_mlstm_n[l],
                                     state_mlstm_m[l], n_seq=DB, L=LS, l_valid=TS)
        mout = mout.reshape(DB, LS, D_M)[:, :TS].reshape(DB * TS, D_M)
        part = _cmp_partial(cache_rows16[l], w1p[l])
        ckv = _cmp_finish_paged(part, w2p[l], page_table, 0, half_per_page)
        padq = lambda a: jnp.pad(a.reshape(DB, TS, -1), ((0, 0), (0, NQS - TS), (0, 0)))
        o3 = _nsa_sample(padq(zq), padq(zkv), padq(zw), cache_pages, l * n_pool, page_table, win_state, l * DB, ckv,
                         bias_cmp_s, bias_slc_s, bias_win_s, ov_s, expand_s)
        ocmp, oslc, owin = [o[:, :TS].reshape(DB * TS, D_A) for o in o3]
        x1 = _merge(xs, mout, ocmp, oslc, owin, zg, zgm, gexp, wb16[l], wo16[l], ln1_g[l], ln1_b[l], alpha)
        xs = _moe(x1, wr_pad, rb_pad, wg16, wu16, wd16, l, ln2_g[l], ln2_b[l], alpha)
        zw5 = zw.reshape(DB, TS, 2, HKV, HD)
        outs[5].append(zkv.reshape(DB, TS, 4, HKV, HD))
        outs[6].append(jnp.concatenate([state_win_kv[l], zw5], axis=1)[:, TS:])
        outs[7].append(c_s)
        outs[8].append(n_s)
        outs[9].append(m_s)
    return (xp.reshape(B, T, D), xs.reshape(DB, TS, D)) + tuple(jnp.stack(o) for o in outs)
```

```python
import functools
import math

import numpy as np
import jax
import jax.numpy as jnp
from jax import lax
from jax.experimental import pallas as pl
from jax.experimental.pallas import tpu as pltpu

F32 = jnp.float32
BF16 = jnp.bfloat16
HIGHEST = lax.Precision.HIGHEST

H_M, DH_M = 4, 128
D_M = H_M * DH_M
HQ, HKV, HD = 8, 2, 64
REP = HQ // HKV
D_A = HQ * HD
CMP_LEN, CMP_STRIDE, CMP_HID = 32, 16, 256
SLC_LEN, N_SEL, WINDOW = 64, 16, 512
N_BUCKETS, REL_MAX_DIST = 32, 128
N_EXP, N_GROUPS, TOP_K, D_EXP = 16, 4, 2, 256
EXP_PER_GROUP = N_EXP // N_GROUPS
LN_EPS = 1e-5
NEG = -1e30

LANES = 128
QT = 128
KT = 512
NSP = 64
NQS = 8
VMEM_LIMIT = 56 * 1024 * 1024


def _cparams(*sem):
    return pltpu.CompilerParams(dimension_semantics=sem, vmem_limit_bytes=VMEM_LIMIT)


def _dot(a, b):
    return jnp.dot(a.astype(BF16), b.astype(BF16), preferred_element_type=F32)


def _dot_nt(a, b):
    return lax.dot_general(a.astype(BF16), b.astype(BF16), (((1,), (1,)), ((), ())), preferred_element_type=F32)


def _dot_tn(a, b):
    return lax.dot_general(a.astype(BF16), b.astype(BF16), (((0,), (0,)), ((), ())), preferred_element_type=F32)


def _dot_f32(a, b):
    return jnp.dot(a, b, precision=HIGHEST, preferred_element_type=F32)


def _dot_nt_f32(a, b):
    return lax.dot_general(a, b, (((1,), (1,)), ((), ())), precision=HIGHEST, preferred_element_type=F32)


def _pick_tile(n, cands):
    for c in cands:
        if n % c == 0:
            return c
    raise ValueError(f"no tile for {n}")


def _layer_norm(y, g, b):
    mu = jnp.mean(y, axis=-1, keepdims=True)
    yc = y - mu
    var = jnp.mean(yc * yc, axis=-1, keepdims=True)
    return yc * lax.rsqrt(var + LN_EPS) * g + b


def _inproj_kernel(x_ref, *refs, n_plain):
    n = len(refs) // 3
    x = x_ref[...].astype(BF16)
    for j, (w_ref, b_ref, o_ref) in enumerate(zip(refs[:n], refs[n:2 * n], refs[2 * n:])):
        if j < n_plain:
            o_ref[...] = jnp.dot(x, w_ref[...], preferred_element_type=F32) + b_ref[...]
        else:
            o_ref[0] = lax.dot_general(w_ref[...], x, (((1,), (1,)), ((), ())), preferred_element_type=F32) + b_ref[...]


def _inproj(x, ws, bs, wts=(), bts=(), n_seq=1):
    n_tok, d = x.shape
    t = n_tok // n_seq
    tm = _pick_tile(t, (256, 128, 64, 32, 16, 8))
    nt = t // tm
    full = lambda i: (0, 0)
    row = lambda i: (i, 0)
    return pl.pallas_call(
        functools.partial(_inproj_kernel, n_plain=len(ws)),
        grid=(n_tok // tm,),
        in_specs=[pl.BlockSpec((tm, d), row)]
        + [pl.BlockSpec(w.shape, full) for w in (*ws, *wts)]
        + [pl.BlockSpec(b.shape, full) for b in (*bs, *bts)],
        out_specs=[pl.BlockSpec((tm, w.shape[1]), row) for w in ws]
        + [pl.BlockSpec((1, w.shape[0], tm), lambda i: (i // nt, 0, i % nt)) for w in wts],
        out_shape=[jax.ShapeDtypeStruct((n_tok, w.shape[1]), F32) for w in ws]
        + [jax.ShapeDtypeStruct((n_seq, w.shape[0], t), F32) for w in wts],
        compiler_params=_cparams("parallel"),
    )(x, *ws, *wts, *bs, *bts)


def _log_sigmoid(x):
    return jnp.minimum(x, 0.0) - jnp.log(1.0 + jnp.exp(-jnp.abs(x)))


def _mlstm_kernel(q_ref, k_ref, v_ref, o_ref, g_ref, ng_ref, c0_ref, n0_ref, m0_ref,
                  mout_ref, c_ref, n_ref, m_ref, cs, ns, ms, *, L, l_valid):
    c = pl.program_id(1)

    @pl.when(c == 0)
    def _():
        cs[...] = c0_ref[0]
        ns[...] = n0_ref[0]
        ms[...] = m0_ref[0]

    g = g_ref[...]
    row = lax.broadcasted_iota(jnp.int32, (L, L), 0)
    col = lax.broadcasted_iota(jnp.int32, (L, L), 1)
    causal = row >= col
    fcum = _dot_f32(causal.astype(F32), _log_sigmoid(g))
    lane = lax.broadcasted_iota(jnp.int32, (L, LANES), 1)
    y = jnp.where(lane < H_M, g - pltpu.roll(fcum, LANES - H_M, 1), fcum)
    yt = y.T
    rowv = lax.broadcasted_iota(jnp.int32, (L, 1), 0)
    r = l_valid - 1
    for h in range(H_M):
        hs = slice(h * DH_M, (h + 1) * DH_M)
        f_col = fcum[:, H_M + h:H_M + h + 1]
        a_row = yt[h:h + 1, :]
        i_col = g[:, h:h + 1]
        m0 = ms[h][:, 0:1]
        n0 = ns[h]
        c0 = cs[h]
        dm = jnp.where(causal, f_col + a_row, NEG)
        b = f_col + m0
        mrow = jnp.maximum(b, jnp.max(dm, axis=1, keepdims=True))
        w = jnp.exp(dm - mrow)
        dec = jnp.exp(b - mrow)
        q = q_ref[:, hs]
        k = k_ref[:, hs] * (DH_M ** -0.5)
        v = v_ref[:, hs]
        s = _dot_nt(q, k) * w
        num = dec * _dot_nt(q, c0) + _dot(s, v)
        den = dec * jnp.sum(q * n0, axis=1, keepdims=True) + jnp.sum(s, axis=1, keepdims=True)
        hh = num / jnp.maximum(jnp.abs(den), jnp.exp(-mrow))
        mu = jnp.mean(hh, axis=1, keepdims=True)
        hc = hh - mu
        var = jnp.mean(hc * hc, axis=1, keepdims=True)
        hn = hc * lax.rsqrt(var + LN_EPS) * ng_ref[:, hs]
        mout_ref[:, hs] = (hn * jax.nn.sigmoid(o_ref[:, hs])).astype(mout_ref.dtype)
        f_r = fcum[r:r + 1, H_M + h:H_M + h + 1]
        m_r = mrow[r:r + 1, :]
        w_last = jnp.where(rowv <= r, jnp.exp(f_r - f_col + i_col - m_r), 0.0)
        d_last = dec[r:r + 1, :]
        cs[h] = d_last * c0 + _dot_tn(v * w_last, k)
        ns[h] = d_last * n0 + jnp.sum(k * w_last, axis=0, keepdims=True)
        ms[h] = jnp.broadcast_to(m_r, (1, LANES))

    @pl.when(c == pl.num_programs(1) - 1)
    def _():
        c_ref[0] = cs[...]
        n_ref[0] = ns[...]
        m_ref[0] = ms[...]


def _mlstm(zm, zg, norm_g, c0, n0, m0, *, n_seq, L, l_valid):
    n_tok = zm.shape[0]
    t = n_tok // n_seq
    nc = t // L
    n0 = n0.reshape(n_seq, H_M, 1, DH_M)
    m0 = jnp.broadcast_to(m0.reshape(n_seq, H_M, 1, 1), (n_seq, H_M, 1, LANES))
    colblk = lambda j: (lambda b, c: (b * nc + c, j))
    st4 = lambda b, c: (b, 0, 0, 0)
    mout, c_new, n_new, m_new = pl.pallas_call(
        functools.partial(_mlstm_kernel, L=L, l_valid=l_valid),
        grid=(n_seq, nc),
        in_specs=[pl.BlockSpec((L, D_M), colblk(0)), pl.BlockSpec((L, D_M), colblk(1)),
                  pl.BlockSpec((L, D_M), colblk(2)), pl.BlockSpec((L, D_M), colblk(3)),
                  pl.BlockSpec((L, LANES), colblk(0)),
                  pl.BlockSpec((1, D_M), lambda b, c: (0, 0)),
                  pl.BlockSpec((1, H_M, DH_M, DH_M), st4),
                  pl.BlockSpec((1, H_M, 1, DH_M), st4),
                  pl.BlockSpec((1, H_M, 1, LANES), st4)],
        out_specs=[pl.BlockSpec((L, D_M), colblk(0)),
                   pl.BlockSpec((1, H_M, DH_M, DH_M), st4),
                   pl.BlockSpec((1, H_M, 1, DH_M), st4),
                   pl.BlockSpec((1, H_M, 1, LANES), st4)],
        out_shape=[jax.ShapeDtypeStruct((n_tok, D_M), BF16),
                   jax.ShapeDtypeStruct((n_seq, H_M, DH_M, DH_M), F32),
                   jax.ShapeDtypeStruct((n_seq, H_M, 1, DH_M), F32),
                   jax.ShapeDtypeStruct((n_seq, H_M, 1, LANES), F32)],
        scratch_shapes=[pltpu.VMEM((H_M, DH_M, DH_M), F32), pltpu.VMEM((H_M, 1, DH_M), F32),
                        pltpu.VMEM((H_M, 1, LANES), F32)],
        compiler_params=_cparams("parallel", "arbitrary"),
    )(zm, zm, zm, zm, zg, norm_g.reshape(1, D_M), c0, n0, m0)
    return mout, c_new, n_new.reshape(n_seq, H_M, DH_M), m_new[:, :, 0, 0]


def _cmp_partial_kernel(x_ref, w_ref, o_ref):
    @pl.when(pl.program_id(2) == 0)
    def _():
        o_ref[...] = jnp.zeros_like(o_ref)

    o_ref[0] += _dot(x_ref[...], w_ref[0, 0])


def _cmp_partial(rows16, w1p):
    n_half = rows16.shape[0]
    tm = _pick_tile(n_half, (2048, 1024, 512, 256, 128, 64, 32, 16, 8))
    hid2 = 2 * HKV * CMP_HID
    return pl.pallas_call(
        _cmp_partial_kernel,
        grid=(2, n_half // tm, CMP_STRIDE),
        in_specs=[pl.BlockSpec((tm, LANES), lambda s, i, r: (i, 2 * r + s)),
                  pl.BlockSpec((1, 1, LANES, hid2), lambda s, i, r: (s, r, 0, 0))],
        out_specs=pl.BlockSpec((1, tm, hid2), lambda s, i, r: (s, i, 0)),
        out_shape=jax.ShapeDtypeStruct((2, n_half, hid2), F32),
        compiler_params=_cparams("parallel", "parallel", "arbitrary"),
    )(rows16, w1p)


def _cmp_partial_paged_kernel(x_ref, w_ref, o_ref, tok, *, pages, page):
    half_per_page = page // CMP_STRIDE
    for s in range(2):
        def to_token_major(p, carry, s=s):
            r0 = pl.multiple_of(p * page, page)
            tok[pl.ds(r0, page), :] = x_ref[p, s * LANES:(s + 1) * LANES, :].T
            return carry

        lax.fori_loop(0, pages, to_token_major, 0)
        acc = None
        for r in range(CMP_STRIDE):
            part = _dot(tok[pl.ds(r, pages * half_per_page, stride=CMP_STRIDE), :], w_ref[s, r])
            acc = part if acc is None else acc + part
        o_ref[s] = acc


def _cmp_partial_paged(cache_t, w1p, layer, n_pool):
    page = cache_t.shape[2]
    pages = _pick_tile(n_pool, (64, 32, 16, 8, 4, 2, 1))
    hid2 = 2 * HKV * CMP_HID
    n_half = n_pool * page // CMP_STRIDE
    rows = pages * page // CMP_STRIDE
    base = layer * (n_pool // pages)
    return pl.pallas_call(
        functools.partial(_cmp_partial_paged_kernel, pages=pages, page=page),
        grid=(n_pool // pages,),
        in_specs=[pl.BlockSpec((pages, 2 * LANES, page), lambda i: (base + i, 0, 0)),
                  pl.BlockSpec(w1p.shape, lambda i: (0, 0, 0, 0))],
        out_specs=pl.BlockSpec((2, rows, hid2), lambda i: (0, i, 0)),
        out_shape=jax.ShapeDtypeStruct((2, n_half, hid2), F32),
        scratch_shapes=[pltpu.VMEM((pages * page, LANES), F32)],
        compiler_params=_cparams("parallel"),
    )(cache_t, w1p)


def _gelu_tanh(x):
    return 0.5 * x * (1.0 + jnp.tanh(math.sqrt(2.0 / math.pi) * (x + 0.044715 * (x * x * x))))


def _cmp_finish_kernel(*refs, n_in):
    p_refs = refs[-(n_in + 2):-2]
    w2_ref, o_ref = refs[-2], refs[-1]
    half = HKV * CMP_HID
    for s in range(2):
        p = jnp.concatenate([pr[s] for pr in p_refs], axis=0) if n_in > 1 else p_refs[0][s]
        n = p.shape[0]
        first, second = p[:, :half], p[:, half:]
        hid = _gelu_tanh(first + pltpu.roll(second, n - 1, 0))
        o_ref[0, s] = _dot(hid, w2_ref[s])


def _cmp_finish_prompt(part, w2p, n_seq):
    n_half = part.shape[1] // n_seq
    hid2 = part.shape[2]
    return pl.pallas_call(
        functools.partial(_cmp_finish_kernel, n_in=1),
        grid=(n_seq,),
        in_specs=[pl.BlockSpec((2, n_half, hid2), lambda b: (0, b, 0)),
                  pl.BlockSpec(w2p.shape, lambda b: (0, 0, 0))],
        out_specs=pl.BlockSpec((1, 2, n_half, LANES), lambda b: (b, 0, 0, 0)),
        out_shape=jax.ShapeDtypeStruct((n_seq, 2, n_half, LANES), F32),
        compiler_params=_cparams("parallel"),
    )(part, w2p)


def _cmp_finish_paged(part, w2p, page_table, page_base, half_per_page):
    n_seq, n_pages = page_table.shape
    hid2 = part.shape[2]
    n_half = n_pages * half_per_page
    page_spec = lambda p: pl.BlockSpec((2, half_per_page, hid2), lambda b, pt: (0, page_base + pt[b, p], 0))
    return pl.pallas_call(
        functools.partial(_cmp_finish_kernel, n_in=n_pages),
        grid_spec=pltpu.PrefetchScalarGridSpec(
            num_scalar_prefetch=1, grid=(n_seq,),
            in_specs=[page_spec(p) for p in range(n_pages)] + [pl.BlockSpec(w2p.shape, lambda b, pt: (0, 0, 0))],
            out_specs=pl.BlockSpec((1, 2, n_half, LANES), lambda b, pt: (b, 0, 0, 0))),
        out_shape=jax.ShapeDtypeStruct((n_seq, 2, n_half, LANES), F32),
        compiler_params=_cparams("parallel"),
    )(page_table, *([part] * n_pages), w2p)


def _cmp_branch(q, ck, cv, bias_ref, g, ocmp_ref):
    gs = slice(g * HD, (g + 1) * HD)
    ck_g, cv_g = ck[:, gs], cv[:, gs]
    psum = None
    for r in range(REP):
        h = g * REP + r
        bias = bias_ref[h]
        s = _dot_nt_f32(q[:, h * HD:(h + 1) * HD], ck_g) + bias
        e = jnp.exp(s - jnp.max(s, axis=1, keepdims=True))
        p = jnp.where(bias > 0.5 * NEG, e / jnp.sum(e, axis=1, keepdims=True), 0.0)
        ocmp_ref[:, h * HD:(h + 1) * HD] = _dot(p, cv_g)
        psum = p if psum is None else psum + p
    return psum


def _select_blocks(imp, qpos, blk_axis):
    blk = lax.broadcasted_iota(jnp.int32, imp.shape, blk_axis)
    cur = (qpos // SLC_LEN) == blk
    avail = blk * SLC_LEN <= qpos
    imp = jnp.where(cur, -NEG, jnp.where(avail, imp, NEG))
    cnt = jnp.zeros(imp.shape, F32)
    for j in range(imp.shape[blk_axis]):
        other = lax.slice_in_dim(imp, j, j + 1, axis=blk_axis)
        cnt = cnt + jnp.where(blk > j, jnp.where(other >= imp, 1.0, 0.0), jnp.where(other > imp, 1.0, 0.0))
    return jnp.where((cnt < N_SEL) & (imp > 0.5 * NEG), 1.0, 0.0)


def _padded_queries(q, g):
    nq = q.shape[0]
    zero = jnp.zeros((nq, HD), F32)
    parts = []
    for r in range(REP):
        h = g * REP + r
        piece = q[:, h * HD:(h + 1) * HD]
        parts.append(jnp.concatenate([piece, zero] if g == 0 else [zero, piece], axis=1))
    return jnp.concatenate(parts, axis=0).astype(BF16)


def _store_heads(o_ref, o, g, nq):
    for r in range(REP):
        h = g * REP + r
        o_ref[:, h * HD:(h + 1) * HD] = o[r * nq:(r + 1) * nq, g * HD:(g + 1) * HD]


def _nsa_cmp_prompt_kernel(q_ref, ckv_ref, bias_ref, ovt_ref, ocmp_ref, sel_ref):
    qb = pl.program_id(1)
    q = q_ref[...] * (HD ** -0.5)
    ck, cv = ckv_ref[0, 0], ckv_ref[0, 1]
    qpos = qb * QT + lax.broadcasted_iota(jnp.int32, (1, QT), 1)
    for g in range(HKV):
        psum = _cmp_branch(q, ck, cv, bias_ref, g, ocmp_ref)
        imp_t = _dot_nt_f32(ovt_ref[...], psum)
        sel_ref[:, g * NSP:(g + 1) * NSP] = _select_blocks(imp_t, qpos, 0).T


def _nsa_cmp_prompt(zq, ckv, bias_cmp, ov_t, n_seq):
    n_tok = zq.shape[0]
    nqb = n_tok // n_seq // QT
    ncp = ckv.shape[2]
    tok = lambda b, i: (b * nqb + i, 0)
    return pl.pallas_call(
        _nsa_cmp_prompt_kernel,
        grid=(n_seq, nqb),
        in_specs=[pl.BlockSpec((QT, D_A), tok),
                  pl.BlockSpec((1, 2, ncp, LANES), lambda b, i: (b, 0, 0, 0)),
                  pl.BlockSpec((HQ, QT, ncp), lambda b, i: (0, i, 0)),
                  pl.BlockSpec(ov_t.shape, lambda b, i: (0, 0))],
        out_specs=[pl.BlockSpec((QT, D_A), tok), pl.BlockSpec((QT, HKV * NSP), tok)],
        out_shape=[jax.ShapeDtypeStruct((n_tok, D_A), F32), jax.ShapeDtypeStruct((n_tok, HKV * NSP), F32)],
        compiler_params=_cparams("parallel", "parallel"),
    )(zq, ckv, bias_cmp, ov_t)


def _flash_prompt_kernel(*refs, window, n_delta):
    if window:
        q_ref, kt_ref, vt_ref, bias_ref, o_ref = refs
        sel_ref = None
    else:
        q_ref, kt_ref, vt_ref, bias_ref, sel_ref, o_ref = refs
    g = pl.program_id(0)
    qb = pl.program_id(2)
    q = q_ref[...] * (HD ** -0.5)
    in_half = lax.broadcasted_iota(jnp.int32, (QT, LANES), 1) // HD == g
    parts = []
    for r in range(REP):
        piece = q[:, r * HD:(r + 1) * HD]
        parts.append(jnp.where(in_half, jnp.concatenate([piece, piece], axis=1), 0.0))
    qpad = jnp.concatenate(parts, axis=0).astype(BF16)
    sel_g = None if window else sel_ref[...].astype(BF16)
    hi = (qb * QT) // KT + 1
    lo = jnp.maximum(qb * QT - (WINDOW - 1), 0) // KT if window else 0

    def body(kt, carry):
        m, l, acc = carry
        k0 = pl.multiple_of(kt * KT, KT)
        k_t = kt_ref[0, :, pl.ds(k0, KT)].astype(BF16)
        v_t = vt_ref[0, :, pl.ds(k0, KT)].astype(BF16)
        delta = jnp.minimum(qb - (KT // QT) * kt, n_delta - 1)
        s = jnp.dot(qpad, k_t, preferred_element_type=F32) + bias_ref[delta].reshape(REP * QT, KT)
        if not window:
            blk = lax.broadcasted_iota(jnp.int32, (HKV * NSP, KT), 0)
            key = lax.broadcasted_iota(jnp.int32, (HKV * NSP, KT), 1)
            expand = jnp.where(blk == g * NSP + (k0 + key) // SLC_LEN, 1.0, 0.0).astype(BF16)
            picked = jnp.dot(sel_g, expand, preferred_element_type=F32)
            s = jnp.where(jnp.concatenate([picked] * REP, axis=0) > 0.5, s, NEG)
        m_new = jnp.maximum(m, jnp.max(s, axis=1, keepdims=True))
        alpha = jnp.exp(m - m_new)
        p = jnp.exp(s - m_new)
        l = alpha * l + jnp.sum(p, axis=1, keepdims=True)
        acc = alpha * acc + _dot_nt(p, v_t)
        return m_new, l, acc

    init = (jnp.full((REP * QT, 1), NEG, F32), jnp.zeros((REP * QT, 1), F32), jnp.zeros((REP * QT, LANES), F32))
    m, l, acc = lax.fori_loop(lo, hi, body, init)
    o = acc / l
    for r in range(REP):
        o_r = o[r * QT:(r + 1) * QT]
        o_ref[:, r * HD:(r + 1) * HD] = jnp.where(g == 0, o_r[:, :HD], o_r[:, HD:])


def _flash_prompt(zq, kv_t, krow, vrow, bias, sel, n_seq, *, window):
    n_tok = zq.shape[0]
    t = kv_t.shape[2]
    nqb = t // QT
    in_specs = [pl.BlockSpec((QT, REP * HD), lambda g, b, i: (b * nqb + i, g)),
                pl.BlockSpec((1, LANES, t), lambda g, b, i: (b, krow, 0)),
                pl.BlockSpec((1, LANES, t), lambda g, b, i: (b, vrow, 0)),
                pl.BlockSpec((bias.shape[0], REP, QT, KT), lambda g, b, i: (0, g, 0, 0))]
    args = [zq, kv_t, kv_t, bias]
    if not window:
        in_specs.append(pl.BlockSpec((QT, HKV * NSP), lambda g, b, i: (b * nqb + i, 0)))
        args.append(sel)
    return pl.pallas_call(
        functools.partial(_flash_prompt_kernel, window=window, n_delta=bias.shape[0]),
        grid=(HKV, n_seq, nqb),
        in_specs=in_specs,
        out_specs=pl.BlockSpec((QT, REP * HD), lambda g, b, i: (b * nqb + i, g)),
        out_shape=jax.ShapeDtypeStruct((n_tok, D_A), F32),
        compiler_params=_cparams("parallel", "parallel", "parallel"),
    )(*args)


def _nsa_sample_kernel(*refs, n_pages, page, nq, past):
    pt_ref = refs[0]
    page_refs = refs[1:1 + n_pages]
    (q_ref, kvn_ref, wn_ref, wbuf_ref, ckv_ref, bcmp_ref, bslc_ref, bwin_ref, ov_ref, exp_ref,
     ocmp_ref, oslc_ref, owin_ref, new_kv, new_w) = refs[1 + n_pages:]
    del pt_ref
    win_buf = wbuf_ref.shape[2]
    new_kv[nq:, :] = jnp.zeros((LANES - nq, 4 * LANES), F32)
    new_kv[:nq, :] = kvn_ref[0]
    new_w[nq:, :] = jnp.zeros((LANES - nq, 2 * LANES), F32)
    new_w[:nq, :] = wn_ref[0]

    q = q_ref[0] * (HD ** -0.5)
    ck, cv = ckv_ref[0, 0], ckv_ref[0, 1]
    qpos = past + lax.broadcasted_iota(jnp.int32, (nq, 1), 0)

    def attend(qpad, old_k, old_v, new_k, new_v, bias):
        s = jnp.concatenate([jnp.dot(qpad, k_t.astype(BF16), preferred_element_type=F32) for k_t in old_k]
                            + [_dot_nt(qpad, new_k)], axis=1) + bias
        e = jnp.exp(s - jnp.max(s, axis=1, keepdims=True))
        o, c0 = _dot(e[:, s.shape[1] - LANES:], new_v), 0
        for v_t in old_v:
            o = o + _dot_nt(e[:, c0:c0 + v_t.shape[1]], v_t)
            c0 += v_t.shape[1]
        return o / jnp.sum(e, axis=1, keepdims=True)

    for g in range(HKV):
        psum = _cmp_branch(q, ck, cv, bcmp_ref, g, ocmp_ref.at[0])
        sel = _select_blocks(_dot_f32(psum, ov_ref[...]), qpos, 1)
        picked = jnp.dot(sel.astype(BF16), exp_ref[...], preferred_element_type=F32)
        qpad = _padded_queries(q, g)
        bslc = bslc_ref[g * REP:(g + 1) * REP].reshape(REP * nq, -1)
        bslc = jnp.where(jnp.concatenate([picked] * REP, axis=0) > 0.5, bslc, NEG)
        o = attend(qpad, [pr[0, 2 * LANES:3 * LANES, :] for pr in page_refs],
                   [pr[0, 3 * LANES:4 * LANES, :] for pr in page_refs],
                   new_kv[:, 2 * LANES:3 * LANES], new_kv[:, 3 * LANES:4 * LANES], bslc)
        _store_heads(oslc_ref.at[0], o, g, nq)
        bwin = bwin_ref[g * REP:(g + 1) * REP].reshape(REP * nq, -1)
        o = attend(qpad, [wbuf_ref[0, :LANES, :]], [wbuf_ref[0, LANES:, :]], new_w[:, :LANES], new_w[:, LANES:], bwin)
        _store_heads(owin_ref.at[0], o, g, nq)


def _nsa_sample(zq, kvn, wn, cache_t, page_base, page_table, win_t, win_base, ckv,
                bias_cmp, bias_slc, bias_win, ov, expand):
    n_seq, nq, _ = zq.shape
    n_pages = page_table.shape[1]
    page = cache_t.shape[2]
    past = n_pages * page
    win_buf = win_t.shape[2]
    ncp = ckv.shape[2]
    seq3 = lambda b, pt: (b, 0, 0)
    const = lambda nd: (lambda b, pt: (0,) * nd)
    page_spec = lambda p: pl.BlockSpec((1, 4 * LANES, page), lambda b, pt: (page_base + pt[b, p], 0, 0))
    in_specs = [page_spec(p) for p in range(n_pages)] + [
        pl.BlockSpec((1, nq, D_A), seq3),
        pl.BlockSpec((1, nq, 4 * LANES), seq3),
        pl.BlockSpec((1, nq, 2 * LANES), seq3),
        pl.BlockSpec((1, 2 * LANES, win_buf), lambda b, pt: (win_base + b, 0, 0)),
        pl.BlockSpec((1, 2, ncp, LANES), lambda b, pt: (b, 0, 0, 0)),
        pl.BlockSpec(bias_cmp.shape, const(3)),
        pl.BlockSpec(bias_slc.shape, const(3)),
        pl.BlockSpec(bias_win.shape, const(3)),
        pl.BlockSpec(ov.shape, const(2)),
        pl.BlockSpec(expand.shape, const(2))]
    out = pl.BlockSpec((1, nq, D_A), seq3)
    return pl.pallas_call(
        functools.partial(_nsa_sample_kernel, n_pages=n_pages, page=page, nq=nq, past=past),
        grid_spec=pltpu.PrefetchScalarGridSpec(
            num_scalar_prefetch=1, grid=(n_seq,), in_specs=in_specs, out_specs=[out, out, out],
            scratch_shapes=[pltpu.VMEM((LANES, 4 * LANES), F32), pltpu.VMEM((LANES, 2 * LANES), F32)]),
        out_shape=[jax.ShapeDtypeStruct((n_seq, nq, D_A), F32)] * 3,
        compiler_params=_cparams("parallel"),
    )(page_table, *([cache_t] * n_pages), zq, kvn, wn, win_t, ckv, bias_cmp, bias_slc, bias_win, ov, expand)


def _merge_kernel(x_ref, mout_ref, ocmp_ref, oslc_ref, owin_ref, zg_ref, zgm_ref, gexp_ref, wb_ref, wo_ref,
                  lg_ref, lb_ref, o_ref, *, alpha):
    d = x_ref.shape[1]
    gate = jax.nn.sigmoid(zg_ref[...])
    a = None
    for br, src in enumerate((ocmp_ref, oslc_ref, owin_ref)):
        term = _dot_f32(gate, gexp_ref[br]) * src[...]
        a = term if a is None else a + term
    u = (jax.nn.sigmoid(zgm_ref[:, :d]) * _dot(mout_ref[...], wb_ref[0])
         + jax.nn.sigmoid(zgm_ref[:, d:]) * _dot(a, wb_ref[1]))
    y = _dot(u, wo_ref[...])
    o_ref[...] = _layer_norm(alpha * x_ref[...] + y, lg_ref[...], lb_ref[...])


def _merge(x, mout, ocmp, oslc, owin, zg, zgm, gexp, wb, wo, lg, lb, alpha):
    n_tok, d = x.shape
    tm = _pick_tile(n_tok, (256, 128, 64, 32, 16, 8))
    row = lambda i: (i, 0)
    c2 = lambda i: (0, 0)
    c3 = lambda i: (0, 0, 0)
    return pl.pallas_call(
        functools.partial(_merge_kernel, alpha=alpha),
        grid=(n_tok // tm,),
        in_specs=[pl.BlockSpec((tm, d), row), pl.BlockSpec((tm, D_M), row), pl.BlockSpec((tm, D_A), row),
                  pl.BlockSpec((tm, D_A), row), pl.BlockSpec((tm, D_A), row), pl.BlockSpec((tm, LANES), row),
                  pl.BlockSpec((tm, 2 * d), row), pl.BlockSpec(gexp.shape, c3), pl.BlockSpec(wb.shape, c3),
                  pl.BlockSpec(wo.shape, c2), pl.BlockSpec((1, d), c2), pl.BlockSpec((1, d), c2)],
        out_specs=pl.BlockSpec((tm, d), row),
        out_shape=jax.ShapeDtypeStruct((n_tok, d), F32),
        compiler_params=_cparams("parallel"),
    )(x, mout, ocmp, oslc, owin, zg, zgm, gexp, wb, wo, lg.reshape(1, d), lb.reshape(1, d))


def _route(aff, sel):
    def top2_sum(a, b, c, d):
        x, x2, y, y2 = jnp.maximum(a, b), jnp.minimum(a, b), jnp.maximum(c, d), jnp.minimum(c, d)
        return jnp.maximum(x, y) + jnp.maximum(jnp.minimum(x, y), jnp.maximum(x2, y2))

    gsum = [top2_sum(*sel[EXP_PER_GROUP * gi:EXP_PER_GROUP * (gi + 1)]) for gi in range(N_GROUPS)]
    gmax = functools.reduce(jnp.maximum, gsum)
    chosen, taken = [], None
    for gi in range(N_GROUPS):
        is_best = gsum[gi] == gmax if taken is None else (gsum[gi] == gmax) & jnp.logical_not(taken)
        taken = is_best if taken is None else taken | is_best
        members = range(EXP_PER_GROUP * gi, EXP_PER_GROUP * (gi + 1))
        for e in members:
            ahead = None
            for e2 in members:
                if e2 == e:
                    continue
                before = (sel[e2] >= sel[e]) if e2 < e else (sel[e2] > sel[e])
                cnt = jnp.where(before, 1.0, 0.0)
                ahead = cnt if ahead is None else ahead + cnt
            chosen.append(is_best & (ahead < TOP_K))
    picked = [jnp.where(c, a, 0.0) for c, a in zip(chosen, aff)]
    total = functools.reduce(lambda a, b: a + b, picked)
    return [p / total for p in picked]


def _moe_kernel(x_ref, wr_ref, rb_ref, wg_ref, wu_ref, wd_ref, lg_ref, lb_ref, o_ref, xb, comb, acc, *, alpha):
    e = pl.program_id(1)
    tm = x_ref.shape[0]
    lane = lax.broadcasted_iota(jnp.int32, (tm, LANES), 1)

    @pl.when(e == 0)
    def _():
        x = x_ref[...]
        xb[...] = x.astype(BF16)
        aff_all = jax.nn.sigmoid(_dot_f32(x, wr_ref[...]))
        sel_all = aff_all + rb_ref[...]
        aff = [aff_all[:, i:i + 1] for i in range(N_EXP)]
        sel = [sel_all[:, i:i + 1] for i in range(N_EXP)]
        weights = _route(aff, sel)
        c = jnp.zeros((tm, LANES), F32)
        for i, w in enumerate(weights):
            c = jnp.where(lane == i, w, c)
        comb[...] = c
        acc[...] = jnp.zeros_like(acc)

    x16 = xb[...]
    w_e = jnp.sum(jnp.where(lane == e, comb[...], 0.0), axis=1, keepdims=True)
    hg = jnp.dot(x16, wg_ref[0, 0], preferred_element_type=F32)
    hu = jnp.dot(x16, wu_ref[0, 0], preferred_element_type=F32)
    h = hg * jax.nn.sigmoid(hg) * hu * w_e
    acc[...] += _dot(h, wd_ref[0, 0])

    @pl.when(e == pl.num_programs(1) - 1)
    def _():
        o_ref[...] = _layer_norm(alpha * x_ref[...] + acc[...], lg_ref[...], lb_ref[...])


def _moe(x, wr, rb, wg, wu, wd, layer, lg, lb, alpha):
    n_tok, d = x.shape
    tm = _pick_tile(n_tok, (512, 256, 128, 64, 32, 16, 8))
    row = lambda i, e: (i, 0)
    c2 = lambda i, e: (0, 0)
    wsel = lambda i, e: (layer, e, 0, 0)
    return pl.pallas_call(
        functools.partial(_moe_kernel, alpha=alpha),
        grid=(n_tok // tm, N_EXP),
        in_specs=[pl.BlockSpec((tm, d), row), pl.BlockSpec(wr.shape, c2), pl.BlockSpec(rb.shape, c2),
                  pl.BlockSpec((1, 1, d, D_EXP), wsel), pl.BlockSpec((1, 1, d, D_EXP), wsel),
                  pl.BlockSpec((1, 1, D_EXP, d), wsel), pl.BlockSpec((1, d), c2), pl.BlockSpec((1, d), c2)],
        out_specs=pl.BlockSpec((tm, d), row),
        out_shape=jax.ShapeDtypeStruct((n_tok, d), F32),
        scratch_shapes=[pltpu.VMEM((tm, d), BF16), pltpu.VMEM((tm, LANES), F32), pltpu.VMEM((tm, d), F32)],
        compiler_params=_cparams("parallel", "arbitrary"),
    )(x, wr, rb, wg, wu, wd, lg.reshape(1, d), lb.reshape(1, d))


def _bucket_np(dist):
    n = np.maximum(dist, 0)
    exact = N_BUCKETS // 2
    nf = np.maximum(n, 1).astype(np.float32)
    large = exact + (np.log(nf / np.float32(exact)) / np.float32(math.log(REL_MAX_DIST / exact))
                     * np.float32(N_BUCKETS - exact)).astype(np.int32)
    return np.where(n < exact, n, np.minimum(large, N_BUCKETS - 1)).astype(np.int32)


def _bias_table(rel_bias, dist, valid):
    tab = rel_bias.astype(F32)[jnp.asarray(_bucket_np(dist))]
    tab = jnp.where(jnp.asarray(valid)[..., None], tab, NEG)
    return jnp.moveaxis(tab, -1, 0)


def _bias_tiles_kernel(rb_ref, o_ref, *, tile_step, key_stride, key_off, hi_valid, upper, lead):
    t = pl.program_id(0)
    shape = o_ref.shape[-2:]
    dist = (tile_step * t - key_off + lax.broadcasted_iota(jnp.int32, shape, 0)
            - key_stride * lax.broadcasted_iota(jnp.int32, shape, 1))
    valid = dist >= 0 if hi_valid is None else (dist >= 0) & (dist < hi_valid)
    acc = [jnp.full(shape, rb_ref[N_BUCKETS - 1, h], F32) for h in range(HQ)]
    for b in range(N_BUCKETS - 2, -1, -1):
        below = dist < upper[b]
        acc = [jnp.where(below, rb_ref[b, h], a) for h, a in enumerate(acc)]
    for h in range(HQ):
        if lead:
            o_ref[0, h] = jnp.where(valid, acc[h], NEG)
        else:
            o_ref[h] = jnp.where(valid, acc[h], NEG)


def _bias_tiles(rel_bias, n_tiles, n_keys, *, tile_step, key_stride, key_off, hi_valid, lead):
    buckets = _bucket_np(np.arange(8 * REL_MAX_DIST))
    upper = tuple(int(np.searchsorted(buckets, b, side='right')) for b in range(N_BUCKETS - 1))
    if lead:
        out_spec = pl.BlockSpec((1, HQ, QT, n_keys), lambda t: (t, 0, 0, 0))
        out_shape = jax.ShapeDtypeStruct((n_tiles, HQ, QT, n_keys), F32)
    else:
        out_spec = pl.BlockSpec((HQ, QT, n_keys), lambda t: (0, t, 0))
        out_shape = jax.ShapeDtypeStruct((HQ, n_tiles * QT, n_keys), F32)
    return pl.pallas_call(
        functools.partial(_bias_tiles_kernel, tile_step=tile_step, key_stride=key_stride, key_off=key_off,
                          hi_valid=hi_valid, upper=upper, lead=lead),
        grid=(n_tiles,),
        in_specs=[pl.BlockSpec(memory_space=pltpu.SMEM)],
        out_specs=out_spec, out_shape=out_shape,
        compiler_params=_cparams("parallel"),
    )(rel_bias.astype(F32))


def _overlap_np(ncp, n_cmp, n_slc):
    c0 = np.arange(ncp)[:, None] * CMP_STRIDE
    j0 = np.arange(NSP)[None, :] * SLC_LEN
    ov = (c0 < j0 + SLC_LEN) & (c0 + CMP_LEN > j0)
    ov &= (np.arange(ncp)[:, None] < n_cmp) & (np.arange(NSP)[None, :] < n_slc)
    return ov.astype(np.float32)


def kernel(x_prompt, x_sample, cache_kv, state_win_kv, state_mlstm_C, state_mlstm_n, state_mlstm_m, page_table,
           w_in, b_in, w_cmp1, w_cmp2, mh_norm_g, w_branch, w_out, ln1_g, ln1_b, ln2_g, ln2_b,
           w_router, router_bias, w_gate_e, w_up_e, w_down_e, rel_bias):
    B, T, D = x_prompt.shape
    DB, TS, _ = x_sample.shape
    depth, n_pool, page = cache_kv.shape[:3]
    n_pages = page_table.shape[1]
    past = n_pages * page
    win_buf = state_win_kv.shape[2]
    alpha = (2 * depth) ** 0.25
    assert T % KT == 0 and T // SLC_LEN <= NSP and page % CMP_STRIDE == 0 and past % SLC_LEN == 0
    assert win_buf == min(WINDOW, past) and T >= win_buf

    off = np.cumsum((0, D_M, D_M, D_M, D_M, H_M, H_M, D_A, 6 * HKV * HD, 3 * HQ, 2 * D)).tolist()
    seg = lambda a, i, j: a[..., off[i]:off[j]]
    small = lambda a: jnp.concatenate(
        [seg(a, 4, 6), seg(a, 8, 9), jnp.zeros(a.shape[:-1] + (LANES - 2 * H_M - 3 * HQ,), a.dtype)], axis=-1)
    kv_cmp, kv_mid = off[7] + 2 * HKV * HD, off[7] + 4 * HKV * HD
    groups_s = lambda a: (seg(a, 0, 4), small(a), seg(a, 6, 7), a[..., off[7]:kv_mid], a[..., kv_mid:off[8]], seg(a, 9, 10))
    groups_p = lambda a: (seg(a, 0, 4), small(a), seg(a, 6, 7), a[..., off[7]:kv_cmp], seg(a, 9, 10))
    ws_s = [w.astype(BF16) for w in groups_s(w_in)]
    bs_s = [b[:, None, :] for b in groups_s(b_in)]
    ws_p = [w.astype(BF16) for w in groups_p(w_in)]
    bs_p = [b[:, None, :] for b in groups_p(b_in)]
    w_in_t = jnp.transpose(w_in, (0, 2, 1))
    wts_p = [w_in_t[:, off[7]:kv_mid].astype(BF16), w_in_t[:, kv_mid:off[8]].astype(BF16)]
    bts_p = [b_in[:, off[7]:kv_mid, None], b_in[:, kv_mid:off[8], None]]
    wb16, wo16 = w_branch.astype(BF16), w_out.astype(BF16)
    wg16, wu16, wd16 = w_gate_e.astype(BF16), w_up_e.astype(BF16), w_down_e.astype(BF16)
    wr_pad = jnp.pad(w_router, ((0, 0), (0, LANES - N_EXP)))
    rb_pad = jnp.pad(router_bias, (0, LANES - N_EXP)).reshape(1, LANES)
    eye = jnp.eye(HKV, dtype=F32)
    w1 = w_cmp1.reshape(depth, 2, 2, CMP_STRIDE, HD, CMP_HID)
    w1p = jnp.einsum('lshrdf,gG->lsrgdhGf', w1, eye).reshape(depth, 2, CMP_STRIDE, HKV * HD, 2 * HKV * CMP_HID)
    w1p = w1p.astype(BF16)
    w2p = jnp.einsum('lsfd,gG->lsgfGd', w_cmp2, eye).reshape(depth, 2, HKV * CMP_HID, HKV * HD).astype(BF16)
    gexp = np.zeros((3, LANES, D_A), np.float32)
    for br in range(3):
        for h in range(HQ):
            gexp[br, 2 * H_M + br * HQ + h, h * HD:(h + 1) * HD] = 1.0
    gexp = jnp.asarray(gexp)

    ncp_p, n_cmp_p = T // CMP_STRIDE, (T - CMP_LEN) // CMP_STRIDE + 1
    assert n_cmp_p * CMP_STRIDE + CMP_LEN - 1 > T - 1
    bias_cmp_p = _bias_tiles(rel_bias, T // QT, ncp_p, tile_step=QT, key_stride=CMP_STRIDE, key_off=CMP_LEN - 1,
                             hi_valid=None, lead=False)
    ov_t_p = jnp.asarray(_overlap_np(ncp_p, n_cmp_p, -(-T // SLC_LEN)).T)
    n_far = -(-(KT - 1 + REL_MAX_DIST) // QT)
    bias_slc_p = _bias_tiles(rel_bias, n_far + 1, KT, tile_step=QT, key_stride=1, key_off=0, hi_valid=None, lead=True)
    bias_win_p = _bias_tiles(rel_bias, (WINDOW + KT) // QT, KT, tile_step=QT, key_stride=1, key_off=0,
                             hi_valid=WINDOW, lead=True)

    ncp_s = past // CMP_STRIDE
    n_cmp_s = (past + TS - CMP_LEN) // CMP_STRIDE + 1
    assert n_cmp_s <= ncp_s and -(-(past + TS) // SLC_LEN) <= NSP and TS <= NQS
    qs = past + np.arange(NQS)[:, None]
    d_cmp_s = qs - (np.arange(ncp_s)[None, :] * CMP_STRIDE + CMP_LEN - 1)
    bias_cmp_s = _bias_table(rel_bias, d_cmp_s, (d_cmp_s >= 0) & (np.arange(ncp_s)[None, :] < n_cmp_s))
    ov_s = jnp.asarray(_overlap_np(ncp_s, n_cmp_s, -(-(past + TS) // SLC_LEN)))
    key_s = np.arange(past + LANES)[None, :]
    d_slc_s = qs - key_s
    bias_slc_s = _bias_table(rel_bias, d_slc_s, (d_slc_s >= 0) & (key_s < past + TS))
    expand_s = jnp.asarray((np.arange(NSP)[:, None] == key_s // SLC_LEN).astype(np.float32)).astype(BF16)
    idx_w = np.arange(win_buf + LANES)[None, :]
    d_win_s = qs - (past - win_buf + idx_w)
    bias_win_s = _bias_table(rel_bias, d_win_s, (d_win_s >= 0) & (d_win_s < WINDOW) & (idx_w < win_buf + TS))

    to_feature_major = lambda a: jnp.transpose(a, (0, 1, 3, 4, 5, 2))
    cache_t = to_feature_major(cache_kv).reshape(depth * n_pool, 4 * LANES, page)
    win_t = to_feature_major(state_win_kv).reshape(depth * DB, 2 * LANES, win_buf)
    from_feature_major = lambda a, slots: jnp.transpose(
        a.reshape(a.shape[:2] + (slots, HKV, HD, a.shape[-1])), (0, 1, 5, 2, 3, 4))
    half_per_page = page // CMP_STRIDE

    LP = _pick_tile(T, (256, 128, 64))
    LS = NQS
    zeros_state = (jnp.zeros((B, H_M, DH_M, DH_M), F32), jnp.zeros((B, H_M, DH_M), F32), jnp.zeros((B, H_M), F32))

    xp = x_prompt.reshape(B * T, D)
    xs = x_sample.reshape(DB * TS, D)
    outs = [[] for _ in range(10)]
    for l in range(depth):
        zm, zg, zq, zc, zgm, zkv_t, zw_t = _inproj(xp, [w[l] for w in ws_p], [b[l] for b in bs_p],
                                                   [w[l] for w in wts_p], [b[l] for b in bts_p], n_seq=B)
        mout, c_p, n_p, m_p = _mlstm(zm, zg, mh_norm_g[l], *zeros_state, n_seq=B, L=LP, l_valid=LP)
        part = _cmp_partial(zc.reshape(B * T // CMP_STRIDE, CMP_STRIDE * 2 * LANES), w1p[l])
        ckv = _cmp_finish_prompt(part, w2p[l], B)
        ocmp, sel = _nsa_cmp_prompt(zq, ckv, bias_cmp_p, ov_t_p, B)
        oslc = _flash_prompt(zq, zkv_t, 2, 3, bias_slc_p, sel, B, window=False)
        owin = _flash_prompt(zq, zw_t, 0, 1, bias_win_p, None, B, window=True)
        x1 = _merge(xp, mout, ocmp, oslc, owin, zg, zgm, gexp, wb16[l], wo16[l], ln1_g[l], ln1_b[l], alpha)
        xp = _moe(x1, wr_pad, rb_pad, wg16, wu16, wd16, l, ln2_g[l], ln2_b[l], alpha)
        outs[0].append(zkv_t)
        outs[1].append(zw_t[:, :, T - win_buf:])
        outs[2].append(c_p)
        outs[3].append(n_p)
        outs[4].append(m_p)
        zm, zg, zq, zkv, zw, zgm = _inproj(xs, [w[l] for w in ws_s], [b[l] for b in bs_s])
        padt = lambda a: jnp.pad(a.reshape(DB, TS, -1), ((0, 0), (0, LS - TS), (0, 0))).reshape(DB * LS, -1)
        mout, c_s, n_s, m_s = _mlstm(padt(zm), padt(zg), mh_norm_g[l], state_mlstm_C[l], state_mlstm_n[l],
                                     state_mlstm_m[l], n_seq=DB, L=LS, l_valid=TS)
        mout = mout.reshape(DB, LS, D_M)[:, :TS].reshape(DB * TS, D_M)
        part = _cmp_partial_paged(cache_t, w1p[l], l, n_pool)
        ckv = _cmp_finish_paged(part, w2p[l], page_table, 0, half_per_page)
        padq = lambda a: jnp.pad(a.reshape(DB, TS, -1), ((0, 0), (0, NQS - TS), (0, 0)))
        o3 = _nsa_sample(padq(zq), padq(zkv), padq(zw), cache_t, l * n_pool, page_table, win_t, l * DB, ckv,
                         bias_cmp_s, bias_slc_s, bias_win_s, ov_s, expand_s)
        ocmp, oslc, owin = [o[:, :TS].reshape(DB * TS, D_A) for o in o3]
        x1 = _merge(xs, mout, ocmp, oslc, owin, zg, zgm, gexp, wb16[l], wo16[l], ln1_g[l], ln1_b[l], alpha)
        xs = _moe(x1, wr_pad, rb_pad, wg16, wu16, wd16, l, ln2_g[l], ln2_b[l], alpha)
        zw5 = zw.reshape(DB, TS, 2, HKV, HD)
        outs[5].append(zkv.reshape(DB, TS, 4, HKV, HD))
        outs[6].append(jnp.concatenate([state_win_kv[l], zw5], axis=1)[:, TS:])
        outs[7].append(c_s)
        outs[8].append(n_s)
        outs[9].append(m_s)
    stacked = [jnp.stack(o) for o in outs]
    stacked[0] = from_feature_major(stacked[0], 4)
    stacked[1] = from_feature_major(stacked[1], 2)
    return (xp.reshape(B, T, D), xs.reshape(DB, TS, D)) + tuple(stacked)
```

```python
import functools
import math

import numpy as np
import jax
import jax.numpy as jnp
from jax import lax
from jax.experimental import pallas as pl
from jax.experimental.pallas import tpu as pltpu

F32 = jnp.float32
BF16 = jnp.bfloat16
HIGHEST = lax.Precision.HIGHEST

H_M, DH_M = 4, 128
D_M = H_M * DH_M
HQ, HKV, HD = 8, 2, 64
REP = HQ // HKV
D_A = HQ * HD
CMP_LEN, CMP_STRIDE, CMP_HID = 32, 16, 256
SLC_LEN, N_SEL, WINDOW = 64, 16, 512
N_BUCKETS, REL_MAX_DIST = 32, 128
N_EXP, N_GROUPS, TOP_K, D_EXP = 16, 4, 2, 256
EXP_PER_GROUP = N_EXP // N_GROUPS
LN_EPS = 1e-5
NEG = -1e30

LANES = 128
QT = 128
KT = 512
ROWS = 32
NSP = 64
NQS = 8
VMEM_LIMIT = 56 * 1024 * 1024


def _cparams(*sem):
    return pltpu.CompilerParams(dimension_semantics=sem, vmem_limit_bytes=VMEM_LIMIT)


def _dot(a, b):
    return jnp.dot(a.astype(BF16), b.astype(BF16), preferred_element_type=F32)


def _dot_nt(a, b):
    return lax.dot_general(a.astype(BF16), b.astype(BF16), (((1,), (1,)), ((), ())), preferred_element_type=F32)


def _dot_tn(a, b):
    return lax.dot_general(a.astype(BF16), b.astype(BF16), (((0,), (0,)), ((), ())), preferred_element_type=F32)


def _dot_f32(a, b):
    return jnp.dot(a, b, precision=HIGHEST, preferred_element_type=F32)


def _dot_nt_f32(a, b):
    return lax.dot_general(a, b, (((1,), (1,)), ((), ())), precision=HIGHEST, preferred_element_type=F32)


def _pick_tile(n, cands):
    for c in cands:
        if n % c == 0:
            return c
    raise ValueError(f"no tile for {n}")


def _layer_norm(y, g, b):
    mu = jnp.mean(y, axis=-1, keepdims=True)
    yc = y - mu
    var = jnp.mean(yc * yc, axis=-1, keepdims=True)
    return yc * lax.rsqrt(var + LN_EPS) * g + b


def _inproj_kernel(x_ref, *refs, n_plain, t_split):
    n_w = n_plain + (1 if t_split else 0)
    w_refs, b_refs, o_refs = refs[:n_plain], refs[n_w:n_w + n_plain], refs[2 * n_w:]
    x = x_ref[...].astype(BF16)
    for w_ref, b_ref, o_ref in zip(w_refs, b_refs, o_refs):
        o_ref[...] = jnp.dot(x, w_ref[...], preferred_element_type=F32) + b_ref[...]
    if t_split:
        wt_ref, bt_ref = refs[n_plain], refs[n_w + n_plain]
        zt = lax.dot_general(wt_ref[...], x, (((1,), (1,)), ((), ())), preferred_element_type=F32) + bt_ref[...]
        lo_ref, hi_ref, all16_ref = o_refs[n_plain:]
        lo_ref[0] = zt[:t_split]
        hi_ref[0] = zt[t_split:]
        all16_ref[0] = zt.astype(BF16)


def _inproj(x, ws, bs, wt=None, bt=None, t_split=0, n_seq=1):
    n_tok, d = x.shape
    t = n_tok // n_seq
    tm = _pick_tile(t, (256, 128, 64, 32, 16, 8))
    nt = t // tm
    full = lambda i: (0, 0)
    row = lambda i: (i, 0)
    tspec = lambda n: pl.BlockSpec((1, n, tm), lambda i: (i // nt, 0, i % nt))
    extra_w = [] if wt is None else [wt]
    extra_b = [] if wt is None else [bt]
    n_t = 0 if wt is None else wt.shape[0]
    t_specs = [] if wt is None else [tspec(t_split), tspec(n_t - t_split), tspec(n_t)]
    t_shapes = [] if wt is None else [jax.ShapeDtypeStruct((n_seq, t_split, t), F32),
                                      jax.ShapeDtypeStruct((n_seq, n_t - t_split, t), F32),
                                      jax.ShapeDtypeStruct((n_seq, n_t, t), BF16)]
    return pl.pallas_call(
        functools.partial(_inproj_kernel, n_plain=len(ws), t_split=t_split if wt is not None else 0),
        grid=(n_tok // tm,),
        in_specs=[pl.BlockSpec((tm, d), row)]
        + [pl.BlockSpec(w.shape, full) for w in (*ws, *extra_w)]
        + [pl.BlockSpec(b.shape, full) for b in (*bs, *extra_b)],
        out_specs=[pl.BlockSpec((tm, w.shape[1]), row) for w in ws] + t_specs,
        out_shape=[jax.ShapeDtypeStruct((n_tok, w.shape[1]), F32) for w in ws] + t_shapes,
        compiler_params=_cparams("parallel"),
    )(x, *ws, *extra_w, *bs, *extra_b)


def _log_sigmoid(x):
    return jnp.minimum(x, 0.0) - jnp.log(1.0 + jnp.exp(-jnp.abs(x)))


def _mlstm_kernel(q_ref, k_ref, v_ref, o_ref, g_ref, ng_ref, c0_ref, n0_ref, m0_ref,
                  mout_ref, c_ref, n_ref, m_ref, cs, ns, ms, *, L, l_valid):
    c = pl.program_id(1)

    @pl.when(c == 0)
    def _():
        cs[...] = c0_ref[0]
        ns[...] = n0_ref[0]
        ms[...] = m0_ref[0]

    g = g_ref[...]
    row = lax.broadcasted_iota(jnp.int32, (L, L), 0)
    col = lax.broadcasted_iota(jnp.int32, (L, L), 1)
    causal = row >= col
    fcum = _dot_f32(causal.astype(F32), _log_sigmoid(g))
    lane = lax.broadcasted_iota(jnp.int32, (L, LANES), 1)
    y = jnp.where(lane < H_M, g - pltpu.roll(fcum, LANES - H_M, 1), fcum)
    yt = y.T
    rowv = lax.broadcasted_iota(jnp.int32, (L, 1), 0)
    r = l_valid - 1
    for h in range(H_M):
        hs = slice(h * DH_M, (h + 1) * DH_M)
        f_col = fcum[:, H_M + h:H_M + h + 1]
        a_row = yt[h:h + 1, :]
        i_col = g[:, h:h + 1]
        m0 = ms[h][:, 0:1]
        n0 = ns[h]
        c0 = cs[h]
        dm = jnp.where(causal, f_col + a_row, NEG)
        b = f_col + m0
        mrow = jnp.maximum(b, jnp.max(dm, axis=1, keepdims=True))
        w = jnp.exp(dm - mrow)
        dec = jnp.exp(b - mrow)
        q = q_ref[:, hs]
        k = k_ref[:, hs] * (DH_M ** -0.5)
        v = v_ref[:, hs]
        s = _dot_nt(q, k) * w
        num = dec * _dot_nt(q, c0) + _dot(s, v)
        den = dec * jnp.sum(q * n0, axis=1, keepdims=True) + jnp.sum(s, axis=1, keepdims=True)
        hh = num / jnp.maximum(jnp.abs(den), jnp.exp(-mrow))
        mu = jnp.mean(hh, axis=1, keepdims=True)
        hc = hh - mu
        var = jnp.mean(hc * hc, axis=1, keepdims=True)
        hn = hc * lax.rsqrt(var + LN_EPS) * ng_ref[:, hs]
        mout_ref[:, hs] = (hn * jax.nn.sigmoid(o_ref[:, hs])).astype(mout_ref.dtype)
        f_r = fcum[r:r + 1, H_M + h:H_M + h + 1]
        m_r = mrow[r:r + 1, :]
        w_last = jnp.where(rowv <= r, jnp.exp(f_r - f_col + i_col - m_r), 0.0)
        d_last = dec[r:r + 1, :]
        cs[h] = d_last * c0 + _dot_tn(v * w_last, k)
        ns[h] = d_last * n0 + jnp.sum(k * w_last, axis=0, keepdims=True)
        ms[h] = jnp.broadcast_to(m_r, (1, LANES))

    @pl.when(c == pl.num_programs(1) - 1)
    def _():
        c_ref[0] = cs[...]
        n_ref[0] = ns[...]
        m_ref[0] = ms[...]


def _mlstm(zm, zg, norm_g, c0, n0, m0, *, n_seq, L, l_valid):
    n_tok = zm.shape[0]
    t = n_tok // n_seq
    nc = t // L
    n0 = n0.reshape(n_seq, H_M, 1, DH_M)
    m0 = jnp.broadcast_to(m0.reshape(n_seq, H_M, 1, 1), (n_seq, H_M, 1, LANES))
    colblk = lambda j: (lambda b, c: (b * nc + c, j))
    st4 = lambda b, c: (b, 0, 0, 0)
    mout, c_new, n_new, m_new = pl.pallas_call(
        functools.partial(_mlstm_kernel, L=L, l_valid=l_valid),
        grid=(n_seq, nc),
        in_specs=[pl.BlockSpec((L, D_M), colblk(0)), pl.BlockSpec((L, D_M), colblk(1)),
                  pl.BlockSpec((L, D_M), colblk(2)), pl.BlockSpec((L, D_M), colblk(3)),
                  pl.BlockSpec((L, LANES), colblk(0)),
                  pl.BlockSpec((1, D_M), lambda b, c: (0, 0)),
                  pl.BlockSpec((1, H_M, DH_M, DH_M), st4),
                  pl.BlockSpec((1, H_M, 1, DH_M), st4),
                  pl.BlockSpec((1, H_M, 1, LANES), st4)],
        out_specs=[pl.BlockSpec((L, D_M), colblk(0)),
                   pl.BlockSpec((1, H_M, DH_M, DH_M), st4),
                   pl.BlockSpec((1, H_M, 1, DH_M), st4),
                   pl.BlockSpec((1, H_M, 1, LANES), st4)],
        out_shape=[jax.ShapeDtypeStruct((n_tok, D_M), BF16),
                   jax.ShapeDtypeStruct((n_seq, H_M, DH_M, DH_M), F32),
                   jax.ShapeDtypeStruct((n_seq, H_M, 1, DH_M), F32),
                   jax.ShapeDtypeStruct((n_seq, H_M, 1, LANES), F32)],
        scratch_shapes=[pltpu.VMEM((H_M, DH_M, DH_M), F32), pltpu.VMEM((H_M, 1, DH_M), F32),
                        pltpu.VMEM((H_M, 1, LANES), F32)],
        compiler_params=_cparams("parallel", "arbitrary"),
    )(zm, zm, zm, zm, zg, norm_g.reshape(1, D_M), c0, n0, m0)
    return mout, c_new, n_new.reshape(n_seq, H_M, DH_M), m_new[:, :, 0, 0]


def _cmp_partial_kernel(x_ref, w_ref, o_ref):
    @pl.when(pl.program_id(2) == 0)
    def _():
        o_ref[...] = jnp.zeros_like(o_ref)

    o_ref[0] += _dot(x_ref[...], w_ref[0, 0])


def _cmp_partial(rows16, w1p):
    n_half = rows16.shape[0]
    tm = _pick_tile(n_half, (2048, 1024, 512, 256, 128, 64, 32, 16, 8))
    hid2 = 2 * HKV * CMP_HID
    return pl.pallas_call(
        _cmp_partial_kernel,
        grid=(2, n_half // tm, CMP_STRIDE),
        in_specs=[pl.BlockSpec((tm, LANES), lambda s, i, r: (i, 2 * r + s)),
                  pl.BlockSpec((1, 1, LANES, hid2), lambda s, i, r: (s, r, 0, 0))],
        out_specs=pl.BlockSpec((1, tm, hid2), lambda s, i, r: (s, i, 0)),
        out_shape=jax.ShapeDtypeStruct((2, n_half, hid2), F32),
        compiler_params=_cparams("parallel", "parallel", "arbitrary"),
    )(rows16, w1p)


def _cmp_partial_paged_kernel(x_ref, w_ref, o_ref, tok, *, pages, page):
    half_per_page = page // CMP_STRIDE
    for s in range(2):
        def to_token_major(p, carry, s=s):
            r0 = pl.multiple_of(p * page, page)
            tok[pl.ds(r0, page), :] = x_ref[p, s * LANES:(s + 1) * LANES, :].T
            return carry

        lax.fori_loop(0, pages, to_token_major, 0)
        acc = None
        n = pages * half_per_page
        for r in range(0, CMP_STRIDE, 2):
            lhs = jnp.concatenate([tok[pl.ds(r, n, stride=CMP_STRIDE), :],
                                   tok[pl.ds(r + 1, n, stride=CMP_STRIDE), :]], axis=1)
            part = _dot(lhs, w_ref[s, r // 2])
            acc = part if acc is None else acc + part
        o_ref[s] = acc


def _cmp_partial_paged(cache_t, w1p, layer, n_pool):
    page = cache_t.shape[2]
    pages = _pick_tile(n_pool, (64, 32, 16, 8, 4, 2, 1))
    hid2 = 2 * HKV * CMP_HID
    n_half = n_pool * page // CMP_STRIDE
    rows = pages * page // CMP_STRIDE
    base = layer * (n_pool // pages)
    return pl.pallas_call(
        functools.partial(_cmp_partial_paged_kernel, pages=pages, page=page),
        grid=(n_pool // pages,),
        in_specs=[pl.BlockSpec((pages, 2 * LANES, page), lambda i: (base + i, 0, 0)),
                  pl.BlockSpec(w1p.shape, lambda i: (0, 0, 0, 0))],
        out_specs=pl.BlockSpec((2, rows, hid2), lambda i: (0, i, 0)),
        out_shape=jax.ShapeDtypeStruct((2, n_half, hid2), F32),
        scratch_shapes=[pltpu.VMEM((pages * page, LANES), F32)],
        compiler_params=_cparams("parallel"),
    )(cache_t, w1p)


def _gelu_tanh(x):
    return 0.5 * x * (1.0 + jnp.tanh(math.sqrt(2.0 / math.pi) * (x + 0.044715 * (x * x * x))))


def _cmp_finish_kernel(*refs, n_in):
    p_refs = refs[-(n_in + 2):-2]
    w2_ref, o_ref = refs[-2], refs[-1]
    half = HKV * CMP_HID
    for s in range(2):
        p = jnp.concatenate([pr[s] for pr in p_refs], axis=0) if n_in > 1 else p_refs[0][s]
        n = p.shape[0]
        first, second = p[:, :half], p[:, half:]
        hid = _gelu_tanh(first + pltpu.roll(second, n - 1, 0))
        o_ref[0, s] = _dot(hid, w2_ref[s])


def _cmp_finish_prompt(part, w2p, n_seq):
    n_half = part.shape[1] // n_seq
    hid2 = part.shape[2]
    return pl.pallas_call(
        functools.partial(_cmp_finish_kernel, n_in=1),
        grid=(n_seq,),
        in_specs=[pl.BlockSpec((2, n_half, hid2), lambda b: (0, b, 0)),
                  pl.BlockSpec(w2p.shape, lambda b: (0, 0, 0))],
        out_specs=pl.BlockSpec((1, 2, n_half, LANES), lambda b: (b, 0, 0, 0)),
        out_shape=jax.ShapeDtypeStruct((n_seq, 2, n_half, LANES), F32),
        compiler_params=_cparams("parallel"),
    )(part, w2p)


def _cmp_finish_paged(part, w2p, page_table, page_base, half_per_page):
    n_seq, n_pages = page_table.shape
    hid2 = part.shape[2]
    n_half = n_pages * half_per_page
    page_spec = lambda p: pl.BlockSpec((2, half_per_page, hid2), lambda b, pt: (0, page_base + pt[b, p], 0))
    return pl.pallas_call(
        functools.partial(_cmp_finish_kernel, n_in=n_pages),
        grid_spec=pltpu.PrefetchScalarGridSpec(
            num_scalar_prefetch=1, grid=(n_seq,),
            in_specs=[page_spec(p) for p in range(n_pages)] + [pl.BlockSpec(w2p.shape, lambda b, pt: (0, 0, 0))],
            out_specs=pl.BlockSpec((1, 2, n_half, LANES), lambda b, pt: (b, 0, 0, 0))),
        out_shape=jax.ShapeDtypeStruct((n_seq, 2, n_half, LANES), F32),
        compiler_params=_cparams("parallel"),
    )(page_table, *([part] * n_pages), w2p)


def _cmp_branch(q, ck, cv, bias_ref, ocmp_ref):
    nq = q.shape[0]
    qpad = jnp.concatenate([_padded_queries(q, g, F32) for g in range(HKV)], axis=0)
    bias = bias_ref[...].reshape(HQ * nq, -1)
    s = _dot_nt_f32(qpad, ck) + bias
    e = jnp.exp(s - jnp.max(s, axis=1, keepdims=True))
    p = jnp.where(bias > 0.5 * NEG, e / jnp.sum(e, axis=1, keepdims=True), 0.0)
    _store_all_heads(ocmp_ref, _dot(p, cv), nq)
    psums = []
    for g in range(HKV):
        heads = [p[(g * REP + r) * nq:(g * REP + r + 1) * nq] for r in range(REP)]
        psums.append(functools.reduce(lambda a, b: a + b, heads))
    return psums


def _select_blocks(imp, qpos, blk_axis, n_blocks):
    blk = lax.broadcasted_iota(jnp.int32, imp.shape, blk_axis)
    cur = (qpos // SLC_LEN) == blk
    avail = blk * SLC_LEN <= qpos
    imp = jnp.where(cur, -NEG, jnp.where(avail, imp, NEG))
    cnt = jnp.zeros(imp.shape, F32)
    for j in range(n_blocks):
        other = lax.slice_in_dim(imp, j, j + 1, axis=blk_axis)
        cnt = cnt + jnp.where(blk > j, jnp.where(other >= imp, 1.0, 0.0), jnp.where(other > imp, 1.0, 0.0))
    return jnp.where((cnt < N_SEL) & (imp > 0.5 * NEG), 1.0, 0.0)


def _padded_queries(q, g, dtype=BF16):
    nq = q.shape[0]
    zero = jnp.zeros((nq, HD), F32)
    parts = []
    for r in range(REP):
        h = g * REP + r
        piece = q[:, h * HD:(h + 1) * HD]
        parts.append(jnp.concatenate([piece, zero] if g == 0 else [zero, piece], axis=1))
    return jnp.concatenate(parts, axis=0).astype(dtype)


def _store_all_heads(o_ref, o, nq):
    for h in range(HQ):
        g = h // REP
        o_ref[:, h * HD:(h + 1) * HD] = o[h * nq:(h + 1) * nq, g * HD:(g + 1) * HD]


def _nsa_cmp_prompt_kernel(q_ref, ckv_ref, bias_ref, ovt_ref, ocmp_ref, sel_ref):
    qb = pl.program_id(1)
    q = q_ref[...] * (HD ** -0.5)
    ck, cv = ckv_ref[0, 0], ckv_ref[0, 1]
    qpos = qb * QT + lax.broadcasted_iota(jnp.int32, (1, QT), 1)
    for g, psum in enumerate(_cmp_branch(q, ck, cv, bias_ref, ocmp_ref)):
        imp_t = _dot_nt_f32(ovt_ref[...], psum)
        sel_ref[:, g * NSP:(g + 1) * NSP] = _select_blocks(imp_t, qpos, 0, NSP).T


def _nsa_cmp_prompt(zq, ckv, bias_cmp, ov_t, n_seq):
    n_tok = zq.shape[0]
    nqb = n_tok // n_seq // QT
    ncp = ckv.shape[2]
    tok = lambda b, i: (b * nqb + i, 0)
    return pl.pallas_call(
        _nsa_cmp_prompt_kernel,
        grid=(n_seq, nqb),
        in_specs=[pl.BlockSpec((QT, D_A), tok),
                  pl.BlockSpec((1, 2, ncp, LANES), lambda b, i: (b, 0, 0, 0)),
                  pl.BlockSpec((HQ, QT, ncp), lambda b, i: (0, i, 0)),
                  pl.BlockSpec(ov_t.shape, lambda b, i: (0, 0))],
        out_specs=[pl.BlockSpec((QT, D_A), tok), pl.BlockSpec((QT, HKV * NSP), tok)],
        out_shape=[jax.ShapeDtypeStruct((n_tok, D_A), F32), jax.ShapeDtypeStruct((n_tok, HKV * NSP), F32)],
        compiler_params=_cparams("parallel", "parallel"),
    )(zq, ckv, bias_cmp, ov_t)


def _flash_prompt_kernel(*refs, window, n_delta):
    if window:
        q_ref, kt_ref, vt_ref, bias_ref, o_ref, s_scr, p_scr, m_scr, l_scr, a_scr, acc_scr = refs
        sel_ref = pick_scr = None
    else:
        q_ref, kt_ref, vt_ref, bias_ref, sel_ref, o_ref, s_scr, p_scr, m_scr, l_scr, a_scr, acc_scr, pick_scr = refs
    g = pl.program_id(0)
    qb = pl.program_id(2)
    q = q_ref[...] * (HD ** -0.5)
    in_half = lax.broadcasted_iota(jnp.int32, (QT, LANES), 1) // HD == g
    parts = []
    for r in range(REP):
        piece = q[:, r * HD:(r + 1) * HD]
        parts.append(jnp.where(in_half, jnp.concatenate([piece, piece], axis=1), 0.0))
    qpad = jnp.concatenate(parts, axis=0).astype(BF16)
    sel_g = None if window else sel_ref[...].astype(BF16)
    hi = (qb * QT) // KT + 1
    lo = jnp.maximum(qb * QT - (WINDOW - 1), 0) // KT if window else 0

    m_scr[...] = jnp.full(m_scr.shape, NEG, F32)
    l_scr[...] = jnp.zeros(l_scr.shape, F32)
    acc_scr[...] = jnp.zeros(acc_scr.shape, F32)

    def body(kt, carry):
        k0 = pl.multiple_of(kt * KT, KT)
        delta = jnp.minimum(qb - (KT // QT) * kt, n_delta - 1)
        s_scr[...] = jnp.dot(qpad, kt_ref[0, :, pl.ds(k0, KT)], preferred_element_type=F32)
        if not window:
            blk = lax.broadcasted_iota(jnp.int32, (HKV * NSP, KT), 0)
            key = lax.broadcasted_iota(jnp.int32, (HKV * NSP, KT), 1)
            expand = jnp.where(blk == g * NSP + (k0 + key) // SLC_LEN, 1.0, 0.0).astype(BF16)
            pick_scr[...] = (1.0 - jnp.dot(sel_g, expand, preferred_element_type=F32)) * NEG
        for c in range(REP * QT // ROWS):
            r, qc = divmod(c, QT // ROWS)
            rows, qrows = pl.ds(c * ROWS, ROWS), pl.ds(qc * ROWS, ROWS)
            s = s_scr[rows, :] + bias_ref[delta, r, qrows, :]
            if not window:
                s = s + pick_scr[qrows, :]
            m_old = m_scr[rows, :]
            m_new = jnp.maximum(m_old, jnp.max(s, axis=1, keepdims=True))
            p = jnp.exp(s - jnp.concatenate([m_new] * (KT // LANES), axis=1))
            alpha = jnp.exp(m_old - m_new)
            l_scr[rows, :] = alpha * l_scr[rows, :] + jnp.sum(p, axis=1, keepdims=True)
            m_scr[rows, :] = m_new
            a_scr[rows, :] = alpha
            p_scr[rows, :] = p.astype(BF16)
        acc_scr[...] = a_scr[...] * acc_scr[...] + lax.dot_general(
            p_scr[...], vt_ref[0, :, pl.ds(k0, KT)], (((1,), (1,)), ((), ())), preferred_element_type=F32)
        return carry

    lax.fori_loop(lo, hi, body, 0)
    o = acc_scr[...] / l_scr[...]
    for r in range(REP):
        o_r = o[r * QT:(r + 1) * QT]
        o_ref[:, r * HD:(r + 1) * HD] = jnp.where(g == 0, o_r[:, :HD], o_r[:, HD:])


def _flash_prompt(zq, kv_t, krow, vrow, bias, sel, n_seq, *, window):
    n_tok = zq.shape[0]
    t = kv_t.shape[2]
    nqb = t // QT
    in_specs = [pl.BlockSpec((QT, REP * HD), lambda g, b, i: (b * nqb + i, g)),
                pl.BlockSpec((1, LANES, t), lambda g, b, i: (b, krow, 0)),
                pl.BlockSpec((1, LANES, t), lambda g, b, i: (b, vrow, 0)),
                pl.BlockSpec((bias.shape[0], REP, QT, KT), lambda g, b, i: (0, g, 0, 0))]
    args = [zq, kv_t, kv_t, bias]
    rows = REP * QT
    scratch = [pltpu.VMEM((rows, KT), F32), pltpu.VMEM((rows, KT), BF16), pltpu.VMEM((rows, LANES), F32),
               pltpu.VMEM((rows, LANES), F32), pltpu.VMEM((rows, LANES), F32), pltpu.VMEM((rows, LANES), F32)]
    if not window:
        in_specs.append(pl.BlockSpec((QT, HKV * NSP), lambda g, b, i: (b * nqb + i, 0)))
        args.append(sel)
        scratch.append(pltpu.VMEM((QT, KT), F32))
    return pl.pallas_call(
        functools.partial(_flash_prompt_kernel, window=window, n_delta=bias.shape[0]),
        grid=(HKV, n_seq, nqb),
        in_specs=in_specs,
        out_specs=pl.BlockSpec((QT, REP * HD), lambda g, b, i: (b * nqb + i, g)),
        out_shape=jax.ShapeDtypeStruct((n_tok, D_A), F32),
        scratch_shapes=scratch,
        compiler_params=_cparams("parallel", "parallel", "parallel"),
    )(*args)


def _nsa_sample_kernel(*refs, n_pages, nq, past, n_item):
    all_pages = refs[1:1 + n_item * n_pages]
    (q_ref, kvn_ref, wn_ref, wbuf_ref, ckv_ref, bcmp_ref, bslc_ref, bwin_ref, ov_ref, exp_ref,
     ocmp_ref, oslc_ref, owin_ref) = refs[1 + n_item * n_pages:14 + n_item * n_pages]
    scratch = refs[14 + n_item * n_pages:]
    for it in range(n_item):
        _nsa_sample_item(all_pages[it * n_pages:(it + 1) * n_pages], q_ref.at[it], kvn_ref.at[it], wn_ref.at[it],
                         wbuf_ref.at[it], ckv_ref.at[it], bcmp_ref, bslc_ref, bwin_ref, ov_ref, exp_ref,
                         ocmp_ref.at[it], oslc_ref.at[it], owin_ref.at[it], scratch[2 * it], scratch[2 * it + 1],
                         nq=nq, past=past)


def _nsa_sample_item(page_refs, q_ref, kvn_ref, wn_ref, wbuf_ref, ckv_ref, bcmp_ref, bslc_ref, bwin_ref, ov_ref,
                     exp_ref, ocmp_ref, oslc_ref, owin_ref, new_kv, new_w, *, nq, past):
    new_kv[nq:, :] = jnp.zeros((LANES - nq, 4 * LANES), F32)
    new_kv[:nq, :] = kvn_ref[...]
    new_w[nq:, :] = jnp.zeros((LANES - nq, 2 * LANES), F32)
    new_w[:nq, :] = wn_ref[...]

    q = q_ref[...] * (HD ** -0.5)
    ck, cv = ckv_ref[0], ckv_ref[1]
    qpos = past + lax.broadcasted_iota(jnp.int32, (nq, 1), 0)

    def attend(qpad, old_k, old_v, new_k, new_v, bias):
        s = jnp.concatenate([jnp.dot(qpad, k_t.astype(BF16), preferred_element_type=F32) for k_t in old_k]
                            + [_dot_nt(qpad, new_k)], axis=1) + bias
        e = jnp.exp(s - jnp.max(s, axis=1, keepdims=True))
        o, c0 = _dot(e[:, s.shape[1] - LANES:], new_v), 0
        for v_t in old_v:
            o = o + _dot_nt(e[:, c0:c0 + v_t.shape[1]], v_t)
            c0 += v_t.shape[1]
        return o / jnp.sum(e, axis=1, keepdims=True)

    picked = []
    n_blocks = -(-(past + nq) // SLC_LEN)
    for psum in _cmp_branch(q, ck, cv, bcmp_ref, ocmp_ref):
        sel = _select_blocks(_dot_f32(psum, ov_ref[...]), qpos, 1, n_blocks)
        picked += [jnp.dot(sel.astype(BF16), exp_ref[...], preferred_element_type=F32)] * REP
    qpad = jnp.concatenate([_padded_queries(q, g) for g in range(HKV)], axis=0)
    bslc = jnp.where(jnp.concatenate(picked, axis=0) > 0.5, bslc_ref[...].reshape(HQ * nq, -1), NEG)
    o = attend(qpad, [pr[0, 2 * LANES:3 * LANES, :] for pr in page_refs],
               [pr[0, 3 * LANES:4 * LANES, :] for pr in page_refs],
               new_kv[:, 2 * LANES:3 * LANES], new_kv[:, 3 * LANES:4 * LANES], bslc)
    _store_all_heads(oslc_ref, o, nq)
    o = attend(qpad, [wbuf_ref[:LANES, :]], [wbuf_ref[LANES:, :]], new_w[:, :LANES], new_w[:, LANES:],
               bwin_ref[...].reshape(HQ * nq, -1))
    _store_all_heads(owin_ref, o, nq)


def _nsa_sample(zq, kvn, wn, cache_t, page_base, page_table, win_t, win_base, ckv,
                bias_cmp, bias_slc, bias_win, ov, expand):
    n_seq, nq, _ = zq.shape
    n_pages = page_table.shape[1]
    page = cache_t.shape[2]
    past = n_pages * page
    win_buf = win_t.shape[2]
    ncp = ckv.shape[2]
    n_item = _pick_tile(n_seq, (2, 1))
    seq3 = lambda b, pt: (b, 0, 0)
    const = lambda nd: (lambda b, pt: (0,) * nd)
    page_spec = lambda it, p: pl.BlockSpec((1, 4 * LANES, page),
                                           lambda b, pt: (page_base + pt[b * n_item + it, p], 0, 0))
    in_specs = [page_spec(it, p) for it in range(n_item) for p in range(n_pages)] + [
        pl.BlockSpec((n_item, nq, D_A), seq3),
        pl.BlockSpec((n_item, nq, 4 * LANES), seq3),
        pl.BlockSpec((n_item, nq, 2 * LANES), seq3),
        pl.BlockSpec((n_item, 2 * LANES, win_buf), lambda b, pt: (win_base // n_item + b, 0, 0)),
        pl.BlockSpec((n_item, 2, ncp, LANES), lambda b, pt: (b, 0, 0, 0)),
        pl.BlockSpec(bias_cmp.shape, const(3)),
        pl.BlockSpec(bias_slc.shape, const(3)),
        pl.BlockSpec(bias_win.shape, const(3)),
        pl.BlockSpec(ov.shape, const(2)),
        pl.BlockSpec(expand.shape, const(2))]
    out = pl.BlockSpec((n_item, nq, D_A), seq3)
    return pl.pallas_call(
        functools.partial(_nsa_sample_kernel, n_pages=n_pages, nq=nq, past=past, n_item=n_item),
        grid_spec=pltpu.PrefetchScalarGridSpec(
            num_scalar_prefetch=1, grid=(n_seq // n_item,), in_specs=in_specs, out_specs=[out, out, out],
            scratch_shapes=[pltpu.VMEM((LANES, 4 * LANES), F32), pltpu.VMEM((LANES, 2 * LANES), F32)] * n_item),
        out_shape=[jax.ShapeDtypeStruct((n_seq, nq, D_A), F32)] * 3,
        compiler_params=_cparams("parallel"),
    )(page_table, *([cache_t] * (n_item * n_pages)), zq, kvn, wn, win_t, ckv, bias_cmp, bias_slc, bias_win, ov, expand)


def _merge_kernel(x_ref, mout_ref, ocmp_ref, oslc_ref, owin_ref, zg_ref, zgm_ref, gexp_ref, wb_ref, wo_ref,
                  lg_ref, lb_ref, o_ref, *, alpha):
    d = x_ref.shape[1]
    gate = jax.nn.sigmoid(zg_ref[...])
    a = None
    for br, src in enumerate((ocmp_ref, oslc_ref, owin_ref)):
        term = _dot_f32(gate, gexp_ref[br]) * src[...]
        a = term if a is None else a + term
    u = (jax.nn.sigmoid(zgm_ref[:, :d]) * _dot(mout_ref[...], wb_ref[0])
         + jax.nn.sigmoid(zgm_ref[:, d:]) * _dot(a, wb_ref[1]))
    y = _dot(u, wo_ref[...])
    o_ref[...] = _layer_norm(alpha * x_ref[...] + y, lg_ref[...], lb_ref[...])


def _merge(x, mout, ocmp, oslc, owin, zg, zgm, gexp, wb, wo, lg, lb, alpha):
    n_tok, d = x.shape
    tm = _pick_tile(n_tok, (256, 128, 64, 32, 16, 8))
    row = lambda i: (i, 0)
    c2 = lambda i: (0, 0)
    c3 = lambda i: (0, 0, 0)
    return pl.pallas_call(
        functools.partial(_merge_kernel, alpha=alpha),
        grid=(n_tok // tm,),
        in_specs=[pl.BlockSpec((tm, d), row), pl.BlockSpec((tm, D_M), row), pl.BlockSpec((tm, D_A), row),
                  pl.BlockSpec((tm, D_A), row), pl.BlockSpec((tm, D_A), row), pl.BlockSpec((tm, LANES), row),
                  pl.BlockSpec((tm, 2 * d), row), pl.BlockSpec(gexp.shape, c3), pl.BlockSpec(wb.shape, c3),
                  pl.BlockSpec(wo.shape, c2), pl.BlockSpec((1, d), c2), pl.BlockSpec((1, d), c2)],
        out_specs=pl.BlockSpec((tm, d), row),
        out_shape=jax.ShapeDtypeStruct((n_tok, d), F32),
        compiler_params=_cparams("parallel"),
    )(x, mout, ocmp, oslc, owin, zg, zgm, gexp, wb, wo, lg.reshape(1, d), lb.reshape(1, d))


def _route(aff, sel):
    def top2_sum(a, b, c, d):
        x, x2, y, y2 = jnp.maximum(a, b), jnp.minimum(a, b), jnp.maximum(c, d), jnp.minimum(c, d)
        return jnp.maximum(x, y) + jnp.maximum(jnp.minimum(x, y), jnp.maximum(x2, y2))

    gsum = [top2_sum(*sel[EXP_PER_GROUP * gi:EXP_PER_GROUP * (gi + 1)]) for gi in range(N_GROUPS)]
    gmax = functools.reduce(jnp.maximum, gsum)
    chosen, taken = [], None
    for gi in range(N_GROUPS):
        is_best = gsum[gi] == gmax if taken is None else (gsum[gi] == gmax) & jnp.logical_not(taken)
        taken = is_best if taken is None else taken | is_best
        members = range(EXP_PER_GROUP * gi, EXP_PER_GROUP * (gi + 1))
        for e in members:
            ahead = None
            for e2 in members:
                if e2 == e:
                    continue
                before = (sel[e2] >= sel[e]) if e2 < e else (sel[e2] > sel[e])
                cnt = jnp.where(before, 1.0, 0.0)
                ahead = cnt if ahead is None else ahead + cnt
            chosen.append(is_best & (ahead < TOP_K))
    picked = [jnp.where(c, a, 0.0) for c, a in zip(chosen, aff)]
    total = functools.reduce(lambda a, b: a + b, picked)
    return [p / total for p in picked]


def _moe_kernel(x_ref, wr_ref, rb_ref, wg_ref, wu_ref, wd_ref, lg_ref, lb_ref, o_ref, xb, comb, acc, *, alpha):
    e = pl.program_id(1)
    tm = x_ref.shape[0]
    lane = lax.broadcasted_iota(jnp.int32, (tm, LANES), 1)

    @pl.when(e == 0)
    def _():
        x = x_ref[...]
        xb[...] = x.astype(BF16)
        aff_all = jax.nn.sigmoid(_dot_f32(x, wr_ref[...]))
        sel_all = aff_all + rb_ref[...]
        aff = [aff_all[:, i:i + 1] for i in range(N_EXP)]
        sel = [sel_all[:, i:i + 1] for i in range(N_EXP)]
        weights = _route(aff, sel)
        c = jnp.zeros((tm, LANES), F32)
        for i, w in enumerate(weights):
            c = jnp.where(lane == i, w, c)
        comb[...] = c
        acc[...] = jnp.zeros_like(acc)

    x16 = xb[...]
    w_e = jnp.sum(jnp.where(lane == e, comb[...], 0.0), axis=1, keepdims=True)
    hg = jnp.dot(x16, wg_ref[0, 0], preferred_element_type=F32)
    hu = jnp.dot(x16, wu_ref[0, 0], preferred_element_type=F32)
    h = hg * jax.nn.sigmoid(hg) * hu * w_e
    acc[...] += _dot(h, wd_ref[0, 0])

    @pl.when(e == pl.num_programs(1) - 1)
    def _():
        o_ref[...] = _layer_norm(alpha * x_ref[...] + acc[...], lg_ref[...], lb_ref[...])


def _moe(x, wr, rb, wg, wu, wd, layer, lg, lb, alpha):
    n_tok, d = x.shape
    tm = _pick_tile(n_tok, (1024, 512, 256, 128, 64, 32, 16, 8))
    row = lambda i, e: (i, 0)
    c2 = lambda i, e: (0, 0)
    wsel = lambda i, e: (layer, e, 0, 0)
    return pl.pallas_call(
        functools.partial(_moe_kernel, alpha=alpha),
        grid=(n_tok // tm, N_EXP),
        in_specs=[pl.BlockSpec((tm, d), row), pl.BlockSpec(wr.shape, c2), pl.BlockSpec(rb.shape, c2),
                  pl.BlockSpec((1, 1, d, D_EXP), wsel), pl.BlockSpec((1, 1, d, D_EXP), wsel),
                  pl.BlockSpec((1, 1, D_EXP, d), wsel), pl.BlockSpec((1, d), c2), pl.BlockSpec((1, d), c2)],
        out_specs=pl.BlockSpec((tm, d), row),
        out_shape=jax.ShapeDtypeStruct((n_tok, d), F32),
        scratch_shapes=[pltpu.VMEM((tm, d), BF16), pltpu.VMEM((tm, LANES), F32), pltpu.VMEM((tm, d), F32)],
        compiler_params=_cparams("parallel", "arbitrary"),
    )(x, wr, rb, wg, wu, wd, lg.reshape(1, d), lb.reshape(1, d))


def _bucket_np(dist):
    n = np.maximum(dist, 0)
    exact = N_BUCKETS // 2
    nf = np.maximum(n, 1).astype(np.float32)
    large = exact + (np.log(nf / np.float32(exact)) / np.float32(math.log(REL_MAX_DIST / exact))
                     * np.float32(N_BUCKETS - exact)).astype(np.int32)
    return np.where(n < exact, n, np.minimum(large, N_BUCKETS - 1)).astype(np.int32)


def _bias_table(rel_bias, dist, valid):
    tab = rel_bias.astype(F32)[jnp.asarray(_bucket_np(dist))]
    tab = jnp.where(jnp.asarray(valid)[..., None], tab, NEG)
    return jnp.moveaxis(tab, -1, 0)


def _bias_tiles_kernel(rb_ref, o_ref, *, tile_step, key_stride, key_off, hi_valid, upper, lead):
    t = pl.program_id(0)
    shape = o_ref.shape[-2:]
    dist = (tile_step * t - key_off + lax.broadcasted_iota(jnp.int32, shape, 0)
            - key_stride * lax.broadcasted_iota(jnp.int32, shape, 1))
    valid = dist >= 0 if hi_valid is None else (dist >= 0) & (dist < hi_valid)
    acc = [jnp.full(shape, rb_ref[N_BUCKETS - 1, h], F32) for h in range(HQ)]
    for b in range(N_BUCKETS - 2, -1, -1):
        below = dist < upper[b]
        acc = [jnp.where(below, rb_ref[b, h], a) for h, a in enumerate(acc)]
    for h in range(HQ):
        if lead:
            o_ref[0, h] = jnp.where(valid, acc[h], NEG)
        else:
            o_ref[h] = jnp.where(valid, acc[h], NEG)


def _bias_tiles(rel_bias, n_tiles, n_keys, *, tile_step, key_stride, key_off, hi_valid, lead):
    buckets = _bucket_np(np.arange(8 * REL_MAX_DIST))
    upper = tuple(int(np.searchsorted(buckets, b, side='right')) for b in range(N_BUCKETS - 1))
    if lead:
        out_spec = pl.BlockSpec((1, HQ, QT, n_keys), lambda t: (t, 0, 0, 0))
        out_shape = jax.ShapeDtypeStruct((n_tiles, HQ, QT, n_keys), F32)
    else:
        out_spec = pl.BlockSpec((HQ, QT, n_keys), lambda t: (0, t, 0))
        out_shape = jax.ShapeDtypeStruct((HQ, n_tiles * QT, n_keys), F32)
    return pl.pallas_call(
        functools.partial(_bias_tiles_kernel, tile_step=tile_step, key_stride=key_stride, key_off=key_off,
                          hi_valid=hi_valid, upper=upper, lead=lead),
        grid=(n_tiles,),
        in_specs=[pl.BlockSpec(memory_space=pltpu.SMEM)],
        out_specs=out_spec, out_shape=out_shape,
        compiler_params=_cparams("parallel"),
    )(rel_bias.astype(F32))


def _overlap_np(ncp, n_cmp, n_slc):
    c0 = np.arange(ncp)[:, None] * CMP_STRIDE
    j0 = np.arange(NSP)[None, :] * SLC_LEN
    ov = (c0 < j0 + SLC_LEN) & (c0 + CMP_LEN > j0)
    ov &= (np.arange(ncp)[:, None] < n_cmp) & (np.arange(NSP)[None, :] < n_slc)
    return ov.astype(np.float32)


def kernel(x_prompt, x_sample, cache_kv, state_win_kv, state_mlstm_C, state_mlstm_n, state_mlstm_m, page_table,
           w_in, b_in, w_cmp1, w_cmp2, mh_norm_g, w_branch, w_out, ln1_g, ln1_b, ln2_g, ln2_b,
           w_router, router_bias, w_gate_e, w_up_e, w_down_e, rel_bias):
    B, T, D = x_prompt.shape
    DB, TS, _ = x_sample.shape
    depth, n_pool, page = cache_kv.shape[:3]
    n_pages = page_table.shape[1]
    past = n_pages * page
    win_buf = state_win_kv.shape[2]
    alpha = (2 * depth) ** 0.25
    assert T % KT == 0 and T // SLC_LEN <= NSP and page % CMP_STRIDE == 0 and past % SLC_LEN == 0
    assert win_buf == min(WINDOW, past) and T >= win_buf

    off = np.cumsum((0, D_M, D_M, D_M, D_M, H_M, H_M, D_A, 6 * HKV * HD, 3 * HQ, 2 * D)).tolist()
    seg = lambda a, i, j: a[..., off[i]:off[j]]
    small = lambda a: jnp.concatenate(
        [seg(a, 4, 6), seg(a, 8, 9), jnp.zeros(a.shape[:-1] + (LANES - 2 * H_M - 3 * HQ,), a.dtype)], axis=-1)
    kv_cmp, kv_mid = off[7] + 2 * HKV * HD, off[7] + 4 * HKV * HD
    groups_s = lambda a: (seg(a, 0, 4), small(a), seg(a, 6, 7), a[..., off[7]:kv_mid], a[..., kv_mid:off[8]], seg(a, 9, 10))
    groups_p = lambda a: (seg(a, 0, 4), small(a), seg(a, 6, 7), a[..., off[7]:kv_cmp], seg(a, 9, 10))
    ws_s = [w.astype(BF16) for w in groups_s(w_in)]
    bs_s = [b[:, None, :] for b in groups_s(b_in)]
    ws_p = [w.astype(BF16) for w in groups_p(w_in)]
    bs_p = [b[:, None, :] for b in groups_p(b_in)]
    w_in_t = jnp.transpose(w_in, (0, 2, 1))
    wt_p = w_in_t[:, off[7]:off[8]].astype(BF16)
    bt_p = b_in[:, off[7]:off[8], None]
    wb16, wo16 = w_branch.astype(BF16), w_out.astype(BF16)
    wg16, wu16, wd16 = w_gate_e.astype(BF16), w_up_e.astype(BF16), w_down_e.astype(BF16)
    wr_pad = jnp.pad(w_router, ((0, 0), (0, LANES - N_EXP)))
    rb_pad = jnp.pad(router_bias, (0, LANES - N_EXP)).reshape(1, LANES)
    eye = jnp.eye(HKV, dtype=F32)
    w1 = w_cmp1.reshape(depth, 2, 2, CMP_STRIDE, HD, CMP_HID)
    w1p = jnp.einsum('lshrdf,gG->lsrgdhGf', w1, eye).reshape(depth, 2, CMP_STRIDE, HKV * HD, 2 * HKV * CMP_HID)
    w1p = w1p.astype(BF16)
    w2p = jnp.einsum('lsfd,gG->lsgfGd', w_cmp2, eye).reshape(depth, 2, HKV * CMP_HID, HKV * HD).astype(BF16)
    gexp = np.zeros((3, LANES, D_A), np.float32)
    for br in range(3):
        for h in range(HQ):
            gexp[br, 2 * H_M + br * HQ + h, h * HD:(h + 1) * HD] = 1.0
    gexp = jnp.asarray(gexp)

    ncp_p, n_cmp_p = T // CMP_STRIDE, (T - CMP_LEN) // CMP_STRIDE + 1
    assert n_cmp_p * CMP_STRIDE + CMP_LEN - 1 > T - 1
    bias_cmp_p = _bias_tiles(rel_bias, T // QT, ncp_p, tile_step=QT, key_stride=CMP_STRIDE, key_off=CMP_LEN - 1,
                             hi_valid=None, lead=False)
    ov_t_p = jnp.asarray(_overlap_np(ncp_p, n_cmp_p, -(-T // SLC_LEN)).T)
    n_far = -(-(KT - 1 + REL_MAX_DIST) // QT)
    bias_slc_p = _bias_tiles(rel_bias, n_far + 1, KT, tile_step=QT, key_stride=1, key_off=0, hi_valid=None, lead=True)
    bias_win_p = _bias_tiles(rel_bias, (WINDOW + KT) // QT, KT, tile_step=QT, key_stride=1, key_off=0,
                             hi_valid=WINDOW, lead=True)

    ncp_s = past // CMP_STRIDE
    n_cmp_s = (past + TS - CMP_LEN) // CMP_STRIDE + 1
    assert n_cmp_s <= ncp_s and -(-(past + TS) // SLC_LEN) <= NSP and TS <= NQS
    qs = past + np.arange(NQS)[:, None]
    d_cmp_s = qs - (np.arange(ncp_s)[None, :] * CMP_STRIDE + CMP_LEN - 1)
    bias_cmp_s = _bias_table(rel_bias, d_cmp_s, (d_cmp_s >= 0) & (np.arange(ncp_s)[None, :] < n_cmp_s))
    ov_s = jnp.asarray(_overlap_np(ncp_s, n_cmp_s, -(-(past + TS) // SLC_LEN)))
    key_s = np.arange(past + LANES)[None, :]
    d_slc_s = qs - key_s
    bias_slc_s = _bias_table(rel_bias, d_slc_s, (d_slc_s >= 0) & (key_s < past + TS))
    expand_s = jnp.asarray((np.arange(NSP)[:, None] == key_s // SLC_LEN).astype(np.float32)).astype(BF16)
    idx_w = np.arange(win_buf + LANES)[None, :]
    d_win_s = qs - (past - win_buf + idx_w)
    bias_win_s = _bias_table(rel_bias, d_win_s, (d_win_s >= 0) & (d_win_s < WINDOW) & (idx_w < win_buf + TS))

    to_feature_major = lambda a: jnp.transpose(a, (0, 1, 3, 4, 5, 2))
    cache_t = to_feature_major(cache_kv).reshape(depth * n_pool, 4 * LANES, page)
    win_t = to_feature_major(state_win_kv).reshape(depth * DB, 2 * LANES, win_buf)
    from_feature_major = lambda a, slots: jnp.transpose(
        a.reshape(a.shape[:2] + (slots, HKV, HD, a.shape[-1])), (0, 1, 5, 2, 3, 4))
    half_per_page = page // CMP_STRIDE

    LP = _pick_tile(T, (256, 128, 64))
    LS = NQS
    zeros_state = (jnp.zeros((B, H_M, DH_M, DH_M), F32), jnp.zeros((B, H_M, DH_M), F32), jnp.zeros((B, H_M), F32))

    xp = x_prompt.reshape(B * T, D)
    xs = x_sample.reshape(DB * TS, D)
    outs = [[] for _ in range(10)]
    for l in range(depth):
        zm, zg, zq, zc, zgm, zkv_t, zw_t, z16_t = _inproj(xp, [w[l] for w in ws_p], [b[l] for b in bs_p],
                                                          wt_p[l], bt_p[l], t_split=4 * LANES, n_seq=B)
        mout, c_p, n_p, m_p = _mlstm(zm, zg, mh_norm_g[l], *zeros_state, n_seq=B, L=LP, l_valid=LP)
        part = _cmp_partial(zc.reshape(B * T // CMP_STRIDE, CMP_STRIDE * 2 * LANES), w1p[l])
        ckv = _cmp_finish_prompt(part, w2p[l], B)
        ocmp, sel = _nsa_cmp_prompt(zq, ckv, bias_cmp_p, ov_t_p, B)
        oslc = _flash_prompt(zq, z16_t, 2, 3, bias_slc_p, sel, B, window=False)
        owin = _flash_prompt(zq, z16_t, 4, 5, bias_win_p, None, B, window=True)
        x1 = _merge(xp, mout, ocmp, oslc, owin, zg, zgm, gexp, wb16[l], wo16[l], ln1_g[l], ln1_b[l], alpha)
        xp = _moe(x1, wr_pad, rb_pad, wg16, wu16, wd16, l, ln2_g[l], ln2_b[l], alpha)
        outs[0].append(zkv_t)
        outs[1].append(zw_t[:, :, T - win_buf:])
        outs[2].append(c_p)
        outs[3].append(n_p)
        outs[4].append(m_p)
        zm, zg, zq, zkv, zw, zgm = _inproj(xs, [w[l] for w in ws_s], [b[l] for b in bs_s])
        padt = lambda a: jnp.pad(a.reshape(DB, TS, -1), ((0, 0), (0, LS - TS), (0, 0))).reshape(DB * LS, -1)
        mout, c_s, n_s, m_s = _mlstm(padt(zm), padt(zg), mh_norm_g[l], state_mlstm_C[l], state_mlstm_n[l],
                                     state_mlstm_m[l], n_seq=DB, L=LS, l_valid=TS)
        mout = mout.reshape(DB, LS, D_M)[:, :TS].reshape(DB * TS, D_M)
        part = _cmp_partial_paged(cache_t, w1p[l].reshape(2, CMP_STRIDE // 2, 2 * LANES, -1), l, n_pool)
        ckv = _cmp_finish_paged(part, w2p[l], page_table, 0, half_per_page)
        padq = lambda a: jnp.pad(a.reshape(DB, TS, -1), ((0, 0), (0, NQS - TS), (0, 0)))
        o3 = _nsa_sample(padq(zq), padq(zkv), padq(zw), cache_t, l * n_pool, page_table, win_t, l * DB, ckv,
                         bias_cmp_s, bias_slc_s, bias_win_s, ov_s, expand_s)
        ocmp, oslc, owin = [o[:, :TS].reshape(DB * TS, D_A) for o in o3]
        x1 = _merge(xs, mout, ocmp, oslc, owin, zg, zgm, gexp, wb16[l], wo16[l], ln1_g[l], ln1_b[l], alpha)
        xs = _moe(x1, wr_pad, rb_pad, wg16, wu16, wd16, l, ln2_g[l], ln2_b[l], alpha)
        zw5 = zw.reshape(DB, TS, 2, HKV, HD)
        outs[5].append(zkv.reshape(DB, TS, 4, HKV, HD))
        outs[6].append(jnp.concatenate([state_win_kv[l], zw5], axis=1)[:, TS:])
        outs[7].append(c_s)
        outs[8].append(n_s)
        outs[9].append(m_s)
    stacked = [jnp.stack(o) for o in outs]
    stacked[0] = from_feature_major(stacked[0], 4)
    stacked[1] = from_feature_major(stacked[1], 2)
    return (xp.reshape(B, T, D), xs.reshape(DB, TS, D)) + tuple(stacked)
```

```python
import functools
import math

import numpy as np
import jax
import jax.numpy as jnp
from jax import lax
from jax.experimental import pallas as pl
from jax.experimental.pallas import tpu as pltpu

F32 = jnp.float32
BF16 = jnp.bfloat16
HIGHEST = lax.Precision.HIGHEST

H_M, DH_M = 4, 128
D_M = H_M * DH_M
HQ, HKV, HD = 8, 2, 64
REP = HQ // HKV
D_A = HQ * HD
CMP_LEN, CMP_STRIDE, CMP_HID = 32, 16, 256
SLC_LEN, N_SEL, WINDOW = 64, 16, 512
N_BUCKETS, REL_MAX_DIST = 32, 128
N_EXP, N_GROUPS, TOP_K, D_EXP = 16, 4, 2, 256
EXP_PER_GROUP = N_EXP // N_GROUPS
EXP_PER_STEP = 2
LN_EPS = 1e-5
NEG = -1e30

LANES = 128
QT = 128
KT = 512
ROWS = 32
NSP = 64
NQS = 8
VMEM_LIMIT = 56 * 1024 * 1024


def _cparams(*sem):
    return pltpu.CompilerParams(dimension_semantics=sem, vmem_limit_bytes=VMEM_LIMIT)


def _dot(a, b):
    return jnp.dot(a.astype(BF16), b.astype(BF16), preferred_element_type=F32)


def _dot_nt(a, b):
    return lax.dot_general(a.astype(BF16), b.astype(BF16), (((1,), (1,)), ((), ())), preferred_element_type=F32)


def _dot_tn(a, b):
    return lax.dot_general(a.astype(BF16), b.astype(BF16), (((0,), (0,)), ((), ())), preferred_element_type=F32)


def _dot_f32(a, b):
    return jnp.dot(a, b, precision=HIGHEST, preferred_element_type=F32)


def _dot_nt_f32(a, b):
    return lax.dot_general(a, b, (((1,), (1,)), ((), ())), precision=HIGHEST, preferred_element_type=F32)


def _pick_tile(n, cands):
    for c in cands:
        if n % c == 0:
            return c
    raise ValueError(f"no tile for {n}")


def _layer_norm(y, g, b):
    mu = jnp.mean(y, axis=-1, keepdims=True)
    yc = y - mu
    var = jnp.mean(yc * yc, axis=-1, keepdims=True)
    return yc * lax.rsqrt(var + LN_EPS) * g + b


def _inproj_kernel(x_ref, *refs, n_plain, t_split):
    n_w = n_plain + (1 if t_split else 0)
    w_refs, b_refs, o_refs = refs[:n_plain], refs[n_w:n_w + n_plain], refs[2 * n_w:]
    x = x_ref[...].astype(BF16)
    for w_ref, b_ref, o_ref in zip(w_refs, b_refs, o_refs):
        o_ref[...] = jnp.dot(x, w_ref[...], preferred_element_type=F32) + b_ref[...]
    if t_split:
        wt_ref, bt_ref = refs[n_plain], refs[n_w + n_plain]
        zt = lax.dot_general(wt_ref[...], x, (((1,), (1,)), ((), ())), preferred_element_type=F32) + bt_ref[...]
        lo_ref, hi_ref, all16_ref = o_refs[n_plain:]
        lo_ref[0] = zt[:t_split]
        hi_ref[0] = zt[t_split:]
        all16_ref[0] = zt.astype(BF16)


def _inproj(x, ws, bs, wt=None, bt=None, t_split=0, n_seq=1):
    n_tok, d = x.shape
    t = n_tok // n_seq
    tm = _pick_tile(t, (256, 128, 64, 32, 16, 8))
    nt = t // tm
    full = lambda i: (0, 0)
    row = lambda i: (i, 0)
    tspec = lambda n: pl.BlockSpec((1, n, tm), lambda i: (i // nt, 0, i % nt))
    extra_w = [] if wt is None else [wt]
    extra_b = [] if wt is None else [bt]
    n_t = 0 if wt is None else wt.shape[0]
    t_specs = [] if wt is None else [tspec(t_split), tspec(n_t - t_split), tspec(n_t)]
    t_shapes = [] if wt is None else [jax.ShapeDtypeStruct((n_seq, t_split, t), F32),
                                      jax.ShapeDtypeStruct((n_seq, n_t - t_split, t), F32),
                                      jax.ShapeDtypeStruct((n_seq, n_t, t), BF16)]
    return pl.pallas_call(
        functools.partial(_inproj_kernel, n_plain=len(ws), t_split=t_split if wt is not None else 0),
        grid=(n_tok // tm,),
        in_specs=[pl.BlockSpec((tm, d), row)]
        + [pl.BlockSpec(w.shape, full) for w in (*ws, *extra_w)]
        + [pl.BlockSpec(b.shape, full) for b in (*bs, *extra_b)],
        out_specs=[pl.BlockSpec((tm, w.shape[1]), row) for w in ws] + t_specs,
        out_shape=[jax.ShapeDtypeStruct((n_tok, w.shape[1]), F32) for w in ws] + t_shapes,
        compiler_params=_cparams("parallel"),
    )(x, *ws, *extra_w, *bs, *extra_b)


def _log_sigmoid(x):
    return jnp.minimum(x, 0.0) - jnp.log(1.0 + jnp.exp(-jnp.abs(x)))


def _mlstm_kernel(q_ref, k_ref, v_ref, o_ref, g_ref, ng_ref, c0_ref, n0_ref, m0_ref,
                  mout_ref, c_ref, n_ref, m_ref, cs, ns, ms, *, L, l_valid):
    c = pl.program_id(1)

    @pl.when(c == 0)
    def _():
        cs[...] = c0_ref[0]
        ns[...] = n0_ref[0]
        ms[...] = m0_ref[0]

    g = g_ref[...]
    row = lax.broadcasted_iota(jnp.int32, (L, L), 0)
    col = lax.broadcasted_iota(jnp.int32, (L, L), 1)
    causal = row >= col
    fcum = _dot_f32(causal.astype(F32), _log_sigmoid(g))
    lane = lax.broadcasted_iota(jnp.int32, (L, LANES), 1)
    y = jnp.where(lane < H_M, g - pltpu.roll(fcum, LANES - H_M, 1), fcum)
    yt = y.T
    rowv = lax.broadcasted_iota(jnp.int32, (L, 1), 0)
    r = l_valid - 1
    for h in range(H_M):
        hs = slice(h * DH_M, (h + 1) * DH_M)
        f_col = fcum[:, H_M + h:H_M + h + 1]
        a_row = yt[h:h + 1, :]
        i_col = g[:, h:h + 1]
        m0 = ms[h][:, 0:1]
        n0 = ns[h]
        c0 = cs[h]
        dm = jnp.where(causal, f_col + a_row, NEG)
        b = f_col + m0
        mrow = jnp.maximum(b, jnp.max(dm, axis=1, keepdims=True))
        w = jnp.exp(dm - mrow)
        dec = jnp.exp(b - mrow)
        q = q_ref[:, hs]
        k = k_ref[:, hs] * (DH_M ** -0.5)
        v = v_ref[:, hs]
        s = _dot_nt(q, k) * w
        num = dec * _dot_nt(q, c0) + _dot(s, v)
        den = dec * jnp.sum(q * n0, axis=1, keepdims=True) + jnp.sum(s, axis=1, keepdims=True)
        hh = num / jnp.maximum(jnp.abs(den), jnp.exp(-mrow))
        mu = jnp.mean(hh, axis=1, keepdims=True)
        hc = hh - mu
        var = jnp.mean(hc * hc, axis=1, keepdims=True)
        hn = hc * lax.rsqrt(var + LN_EPS) * ng_ref[:, hs]
        mout_ref[:, hs] = (hn * jax.nn.sigmoid(o_ref[:, hs])).astype(mout_ref.dtype)
        f_r = fcum[r:r + 1, H_M + h:H_M + h + 1]
        m_r = mrow[r:r + 1, :]
        w_last = jnp.where(rowv <= r, jnp.exp(f_r - f_col + i_col - m_r), 0.0)
        d_last = dec[r:r + 1, :]
        cs[h] = d_last * c0 + _dot_tn(v * w_last, k)
        ns[h] = d_last * n0 + jnp.sum(k * w_last, axis=0, keepdims=True)
        ms[h] = jnp.broadcast_to(m_r, (1, LANES))

    @pl.when(c == pl.num_programs(1) - 1)
    def _():
        c_ref[0] = cs[...]
        n_ref[0] = ns[...]
        m_ref[0] = ms[...]


def _mlstm(zm, zg, norm_g, c0, n0, m0, *, n_seq, L, l_valid):
    n_tok = zm.shape[0]
    t = n_tok // n_seq
    nc = t // L
    n0 = n0.reshape(n_seq, H_M, 1, DH_M)
    m0 = jnp.broadcast_to(m0.reshape(n_seq, H_M, 1, 1), (n_seq, H_M, 1, LANES))
    colblk = lambda j: (lambda b, c: (b * nc + c, j))
    st4 = lambda b, c: (b, 0, 0, 0)
    mout, c_new, n_new, m_new = pl.pallas_call(
        functools.partial(_mlstm_kernel, L=L, l_valid=l_valid),
        grid=(n_seq, nc),
        in_specs=[pl.BlockSpec((L, D_M), colblk(0)), pl.BlockSpec((L, D_M), colblk(1)),
                  pl.BlockSpec((L, D_M), colblk(2)), pl.BlockSpec((L, D_M), colblk(3)),
                  pl.BlockSpec((L, LANES), colblk(0)),
                  pl.BlockSpec((1, D_M), lambda b, c: (0, 0)),
                  pl.BlockSpec((1, H_M, DH_M, DH_M), st4),
                  pl.BlockSpec((1, H_M, 1, DH_M), st4),
                  pl.BlockSpec((1, H_M, 1, LANES), st4)],
        out_specs=[pl.BlockSpec((L, D_M), colblk(0)),
                   pl.BlockSpec((1, H_M, DH_M, DH_M), st4),
                   pl.BlockSpec((1, H_M, 1, DH_M), st4),
                   pl.BlockSpec((1, H_M, 1, LANES), st4)],
        out_shape=[jax.ShapeDtypeStruct((n_tok, D_M), BF16),
                   jax.ShapeDtypeStruct((n_seq, H_M, DH_M, DH_M), F32),
                   jax.ShapeDtypeStruct((n_seq, H_M, 1, DH_M), F32),
                   jax.ShapeDtypeStruct((n_seq, H_M, 1, LANES), F32)],
        scratch_shapes=[pltpu.VMEM((H_M, DH_M, DH_M), F32), pltpu.VMEM((H_M, 1, DH_M), F32),
                        pltpu.VMEM((H_M, 1, LANES), F32)],
        compiler_params=_cparams("parallel", "arbitrary"),
    )(zm, zm, zm, zm, zg, norm_g.reshape(1, D_M), c0, n0, m0)
    return mout, c_new, n_new.reshape(n_seq, H_M, DH_M), m_new[:, :, 0, 0]


def _cmp_partial_kernel(x_ref, w_ref, o_ref):
    @pl.when(pl.program_id(2) == 0)
    def _():
        o_ref[...] = jnp.zeros_like(o_ref)

    o_ref[0] += _dot(x_ref[...], w_ref[0, 0])


def _cmp_partial(rows16, w1p):
    n_half = rows16.shape[0]
    tm = _pick_tile(n_half, (2048, 1024, 512, 256, 128, 64, 32, 16, 8))
    hid2 = 2 * HKV * CMP_HID
    return pl.pallas_call(
        _cmp_partial_kernel,
        grid=(2, n_half // tm, CMP_STRIDE),
        in_specs=[pl.BlockSpec((tm, LANES), lambda s, i, r: (i, 2 * r + s)),
                  pl.BlockSpec((1, 1, LANES, hid2), lambda s, i, r: (s, r, 0, 0))],
        out_specs=pl.BlockSpec((1, tm, hid2), lambda s, i, r: (s, i, 0)),
        out_shape=jax.ShapeDtypeStruct((2, n_half, hid2), F32),
        compiler_params=_cparams("parallel", "parallel", "arbitrary"),
    )(rows16, w1p)


def _cmp_partial_paged_kernel(x_ref, perm_ref, w_ref, o_ref, tok, *, pages, page):
    half_per_page = page // CMP_STRIDE
    for s in range(2):
        for p in range(pages):
            xt = _dot_nt(perm_ref[...], x_ref[p, s * LANES:(s + 1) * LANES, :])
            tok[:, p * half_per_page:(p + 1) * half_per_page, :] = xt.reshape(CMP_STRIDE, half_per_page, LANES)
        acc = None
        for r in range(0, CMP_STRIDE, 2):
            part = _dot(jnp.concatenate([tok[r], tok[r + 1]], axis=1), w_ref[s, r // 2])
            acc = part if acc is None else acc + part
        o_ref[s] = acc


def _cmp_partial_paged(cache_t, w1p, layer, n_pool):
    page = cache_t.shape[2]
    pages = _pick_tile(n_pool, (64, 32, 16, 8, 4, 2, 1))
    hid2 = 2 * HKV * CMP_HID
    n_half = n_pool * page // CMP_STRIDE
    half_per_page = page // CMP_STRIDE
    rows = pages * half_per_page
    base = layer * (n_pool // pages)
    tok = np.arange(page)
    perm = np.zeros((page, page), np.float32)
    perm[(tok % CMP_STRIDE) * half_per_page + tok // CMP_STRIDE, tok] = 1.0
    return pl.pallas_call(
        functools.partial(_cmp_partial_paged_kernel, pages=pages, page=page),
        grid=(n_pool // pages,),
        in_specs=[pl.BlockSpec((pages, 2 * LANES, page), lambda i: (base + i, 0, 0)),
                  pl.BlockSpec((page, page), lambda i: (0, 0)),
                  pl.BlockSpec(w1p.shape, lambda i: (0, 0, 0, 0))],
        out_specs=pl.BlockSpec((2, rows, hid2), lambda i: (0, i, 0)),
        out_shape=jax.ShapeDtypeStruct((2, n_half, hid2), F32),
        scratch_shapes=[pltpu.VMEM((CMP_STRIDE, rows, LANES), F32)],
        compiler_params=_cparams("parallel"),
    )(cache_t, jnp.asarray(perm, dtype=BF16), w1p)


def _gelu_tanh(x):
    return 0.5 * x * (1.0 + jnp.tanh(math.sqrt(2.0 / math.pi) * (x + 0.044715 * (x * x * x))))


def _cmp_finish_kernel(*refs, n_in, n_item):
    p_refs = refs[-(n_item * n_in + 2):-2]
    w2_ref, o_ref = refs[-2], refs[-1]
    half = HKV * CMP_HID
    for s in range(2):
        hids = []
        for it in range(n_item):
            pieces = p_refs[it * n_in:(it + 1) * n_in]
            p = jnp.concatenate([pr[s] for pr in pieces], axis=0) if n_in > 1 else pieces[0][s]
            n = p.shape[0]
            first, second = p[:, :half], p[:, half:]
            hids.append(_gelu_tanh(first + pltpu.roll(second, n - 1, 0)))
        out = _dot(jnp.concatenate(hids, axis=0) if n_item > 1 else hids[0], w2_ref[s])
        for it in range(n_item):
            o_ref[it, s] = out[it * n:(it + 1) * n]


def _cmp_finish_prompt(part, w2p, n_seq):
    n_half = part.shape[1] // n_seq
    hid2 = part.shape[2]
    return pl.pallas_call(
        functools.partial(_cmp_finish_kernel, n_in=1, n_item=1),
        grid=(n_seq,),
        in_specs=[pl.BlockSpec((2, n_half, hid2), lambda b: (0, b, 0)),
                  pl.BlockSpec(w2p.shape, lambda b: (0, 0, 0))],
        out_specs=pl.BlockSpec((1, 2, n_half, LANES), lambda b: (b, 0, 0, 0)),
        out_shape=jax.ShapeDtypeStruct((n_seq, 2, n_half, LANES), F32),
        compiler_params=_cparams("parallel"),
    )(part, w2p)


def _cmp_finish_paged(part, w2p, page_table, page_base, half_per_page):
    n_seq, n_pages = page_table.shape
    hid2 = part.shape[2]
    n_half = n_pages * half_per_page
    n_item = _pick_tile(n_seq, (4, 2, 1))
    page_spec = lambda it, p: pl.BlockSpec((2, half_per_page, hid2),
                                           lambda b, pt: (0, page_base + pt[b * n_item + it, p], 0))
    return pl.pallas_call(
        functools.partial(_cmp_finish_kernel, n_in=n_pages, n_item=n_item),
        grid_spec=pltpu.PrefetchScalarGridSpec(
            num_scalar_prefetch=1, grid=(n_seq // n_item,),
            in_specs=[page_spec(it, p) for it in range(n_item) for p in range(n_pages)]
            + [pl.BlockSpec(w2p.shape, lambda b, pt: (0, 0, 0))],
            out_specs=pl.BlockSpec((n_item, 2, n_half, LANES), lambda b, pt: (b, 0, 0, 0))),
        out_shape=jax.ShapeDtypeStruct((n_seq, 2, n_half, LANES), F32),
        compiler_params=_cparams("parallel"),
    )(page_table, *([part] * (n_item * n_pages)), w2p)


def _cmp_branch(q, ck, cv, bias_ref, ocmp_ref):
    nq = q.shape[0]
    qpad = jnp.concatenate([_padded_queries(q, g) for g in range(HKV)], axis=0)
    bias = bias_ref[...].reshape(HQ * nq, -1)
    s = _dot_nt(qpad, ck) + bias
    e = jnp.exp(s - jnp.max(s, axis=1, keepdims=True))
    p = jnp.where(bias > 0.5 * NEG, e / jnp.sum(e, axis=1, keepdims=True), 0.0)
    _store_all_heads(ocmp_ref, _dot(p, cv), nq)
    psums = []
    for g in range(HKV):
        heads = [p[(g * REP + r) * nq:(g * REP + r + 1) * nq] for r in range(REP)]
        psums.append(functools.reduce(lambda a, b: a + b, heads))
    return psums


def _select_blocks(imp, qpos, blk_axis, n_blocks):
    blk = lax.broadcasted_iota(jnp.int32, imp.shape, blk_axis)
    cur = (qpos // SLC_LEN) == blk
    avail = blk * SLC_LEN <= qpos
    imp = jnp.where(cur, -NEG, jnp.where(avail, imp, NEG))
    cnt = jnp.zeros(imp.shape, F32)
    for j in range(n_blocks):
        other = lax.slice_in_dim(imp, j, j + 1, axis=blk_axis)
        cnt = cnt + jnp.where(blk > j, jnp.where(other >= imp, 1.0, 0.0), jnp.where(other > imp, 1.0, 0.0))
    return jnp.where((cnt < N_SEL) & (imp > 0.5 * NEG), 1.0, 0.0)


def _padded_queries(q, g, dtype=BF16):
    nq = q.shape[0]
    zero = jnp.zeros((nq, HD), F32)
    parts = []
    for r in range(REP):
        h = g * REP + r
        piece = q[:, h * HD:(h + 1) * HD]
        parts.append(jnp.concatenate([piece, zero] if g == 0 else [zero, piece], axis=1))
    return jnp.concatenate(parts, axis=0).astype(dtype)


def _store_all_heads(o_ref, o, nq):
    for h in range(HQ):
        g = h // REP
        o_ref[:, h * HD:(h + 1) * HD] = o[h * nq:(h + 1) * nq, g * HD:(g + 1) * HD]


def _nsa_cmp_prompt_kernel(q_ref, ckv_ref, bias_ref, ovt_ref, ocmp_ref, sel_ref):
    qb = pl.program_id(1)
    q = q_ref[...] * (HD ** -0.5)
    ck, cv = ckv_ref[0, 0], ckv_ref[0, 1]
    qpos = qb * QT + lax.broadcasted_iota(jnp.int32, (1, QT), 1)
    for g, psum in enumerate(_cmp_branch(q, ck, cv, bias_ref, ocmp_ref)):
        imp_t = _dot_nt_f32(ovt_ref[...], psum)
        sel_ref[:, g * NSP:(g + 1) * NSP] = _select_blocks(imp_t, qpos, 0, NSP).T


def _nsa_cmp_prompt(zq, ckv, bias_cmp, ov_t, n_seq):
    n_tok = zq.shape[0]
    nqb = n_tok // n_seq // QT
    ncp = ckv.shape[2]
    tok = lambda b, i: (b * nqb + i, 0)
    return pl.pallas_call(
        _nsa_cmp_prompt_kernel,
        grid=(n_seq, nqb),
        in_specs=[pl.BlockSpec((QT, D_A), tok),
                  pl.BlockSpec((1, 2, ncp, LANES), lambda b, i: (b, 0, 0, 0)),
                  pl.BlockSpec((HQ, QT, ncp), lambda b, i: (0, i, 0)),
                  pl.BlockSpec(ov_t.shape, lambda b, i: (0, 0))],
        out_specs=[pl.BlockSpec((QT, D_A), tok), pl.BlockSpec((QT, HKV * NSP), tok)],
        out_shape=[jax.ShapeDtypeStruct((n_tok, D_A), F32), jax.ShapeDtypeStruct((n_tok, HKV * NSP), F32)],
        compiler_params=_cparams("parallel", "parallel"),
    )(zq, ckv, bias_cmp, ov_t)


def _flash_prompt_kernel(*refs, window, n_delta):
    if window:
        q_ref, kt_ref, vt_ref, bias_ref, o_ref, s_scr, p_scr, m_scr, l_scr, a_scr, acc_scr = refs
        sel_ref = pick_scr = None
    else:
        q_ref, kt_ref, vt_ref, bias_ref, sel_ref, o_ref, s_scr, p_scr, m_scr, l_scr, a_scr, acc_scr, pick_scr = refs
    g = pl.program_id(0)
    qb = pl.program_id(2)
    q = q_ref[...] * (HD ** -0.5)
    in_half = lax.broadcasted_iota(jnp.int32, (QT, LANES), 1) // HD == g
    parts = []
    for r in range(REP):
        piece = q[:, r * HD:(r + 1) * HD]
        parts.append(jnp.where(in_half, jnp.concatenate([piece, piece], axis=1), 0.0))
    qpad = jnp.concatenate(parts, axis=0).astype(BF16)
    sel_g = None if window else sel_ref[...].astype(BF16)
    hi = (qb * QT) // KT + 1
    lo = jnp.maximum(qb * QT - (WINDOW - 1), 0) // KT if window else 0

    m_scr[...] = jnp.full(m_scr.shape, NEG, F32)
    l_scr[...] = jnp.zeros(l_scr.shape, F32)
    acc_scr[...] = jnp.zeros(acc_scr.shape, F32)

    def body(kt, carry):
        k0 = pl.multiple_of(kt * KT, KT)
        delta = jnp.minimum(qb - (KT // QT) * kt, n_delta - 1)
        s_scr[...] = jnp.dot(qpad, kt_ref[0, :, pl.ds(k0, KT)], preferred_element_type=F32)
        if not window:
            blk = lax.broadcasted_iota(jnp.int32, (HKV * NSP, KT), 0)
            key = lax.broadcasted_iota(jnp.int32, (HKV * NSP, KT), 1)
            expand = jnp.where(blk == g * NSP + (k0 + key) // SLC_LEN, 1.0, 0.0).astype(BF16)
            pick_scr[...] = (1.0 - jnp.dot(sel_g, expand, preferred_element_type=F32)) * NEG
        for c in range(REP * QT // ROWS):
            r, qc = divmod(c, QT // ROWS)
            rows, qrows = pl.ds(c * ROWS, ROWS), pl.ds(qc * ROWS, ROWS)
            s = s_scr[rows, :] + bias_ref[delta, r, qrows, :]
            if not window:
                s = s + pick_scr[qrows, :]
            m_old = m_scr[rows, :]
            m_new = jnp.maximum(m_old, jnp.max(s, axis=1, keepdims=True))
            p = jnp.exp(s - jnp.concatenate([m_new] * (KT // LANES), axis=1))
            alpha = jnp.exp(m_old - m_new)
            l_scr[rows, :] = alpha * l_scr[rows, :] + jnp.sum(p, axis=1, keepdims=True)
            m_scr[rows, :] = m_new
            a_scr[rows, :] = alpha
            p_scr[rows, :] = p.astype(BF16)
        acc_scr[...] = a_scr[...] * acc_scr[...] + lax.dot_general(
            p_scr[...], vt_ref[0, :, pl.ds(k0, KT)], (((1,), (1,)), ((), ())), preferred_element_type=F32)
        return carry

    lax.fori_loop(lo, hi, body, 0)
    o = acc_scr[...] / l_scr[...]
    for r in range(REP):
        o_r = o[r * QT:(r + 1) * QT]
        o_ref[:, r * HD:(r + 1) * HD] = jnp.where(g == 0, o_r[:, :HD], o_r[:, HD:])


def _flash_prompt(zq, kv_t, krow, vrow, bias, sel, n_seq, *, window):
    n_tok = zq.shape[0]
    t = kv_t.shape[2]
    nqb = t // QT
    in_specs = [pl.BlockSpec((QT, REP * HD), lambda g, b, i: (b * nqb + i, g)),
                pl.BlockSpec((1, LANES, t), lambda g, b, i: (b, krow, 0)),
                pl.BlockSpec((1, LANES, t), lambda g, b, i: (b, vrow, 0)),
                pl.BlockSpec((bias.shape[0], REP, QT, KT), lambda g, b, i: (0, g, 0, 0))]
    args = [zq, kv_t, kv_t, bias]
    rows = REP * QT
    scratch = [pltpu.VMEM((rows, KT), F32), pltpu.VMEM((rows, KT), BF16), pltpu.VMEM((rows, LANES), F32),
               pltpu.VMEM((rows, LANES), F32), pltpu.VMEM((rows, LANES), F32), pltpu.VMEM((rows, LANES), F32)]
    if not window:
        in_specs.append(pl.BlockSpec((QT, HKV * NSP), lambda g, b, i: (b * nqb + i, 0)))
        args.append(sel)
        scratch.append(pltpu.VMEM((QT, KT), F32))
    return pl.pallas_call(
        functools.partial(_flash_prompt_kernel, window=window, n_delta=bias.shape[0]),
        grid=(HKV, n_seq, nqb),
        in_specs=in_specs,
        out_specs=pl.BlockSpec((QT, REP * HD), lambda g, b, i: (b * nqb + i, g)),
        out_shape=jax.ShapeDtypeStruct((n_tok, D_A), F32),
        scratch_shapes=scratch,
        compiler_params=_cparams("parallel", "parallel", "parallel"),
    )(*args)


def _nsa_sample_kernel(*refs, n_pages, nq, past, n_item):
    all_pages = refs[1:1 + n_item * n_pages]
    (q_ref, kvn_ref, wn_ref, wbuf_ref, ckv_ref, bcmp_ref, bslc_ref, bwin_ref, ov_ref, exp_ref,
     ocmp_ref, oslc_ref, owin_ref) = refs[1 + n_item * n_pages:14 + n_item * n_pages]
    scratch = refs[14 + n_item * n_pages:]
    for it in range(n_item):
        _nsa_sample_item(all_pages[it * n_pages:(it + 1) * n_pages], q_ref.at[it], kvn_ref.at[it], wn_ref.at[it],
                         wbuf_ref.at[it], ckv_ref.at[it], bcmp_ref, bslc_ref, bwin_ref, ov_ref, exp_ref,
                         ocmp_ref.at[it], oslc_ref.at[it], owin_ref.at[it], scratch[2 * it], scratch[2 * it + 1],
                         nq=nq, past=past)


def _nsa_sample_item(page_refs, q_ref, kvn_ref, wn_ref, wbuf_ref, ckv_ref, bcmp_ref, bslc_ref, bwin_ref, ov_ref,
                     exp_ref, ocmp_ref, oslc_ref, owin_ref, new_kv, new_w, *, nq, past):
    new_kv[nq:, :] = jnp.zeros((LANES - nq, 4 * LANES), F32)
    new_kv[:nq, :] = kvn_ref[...]
    new_w[nq:, :] = jnp.zeros((LANES - nq, 2 * LANES), F32)
    new_w[:nq, :] = wn_ref[...]

    q = q_ref[...] * (HD ** -0.5)
    ck, cv = ckv_ref[0], ckv_ref[1]
    qpos = past + lax.broadcasted_iota(jnp.int32, (nq, 1), 0)

    def attend(qpad, old_k, old_v, new_k, new_v, bias):
        s = jnp.concatenate([jnp.dot(qpad, k_t.astype(BF16), preferred_element_type=F32) for k_t in old_k]
                            + [_dot_nt(qpad, new_k)], axis=1) + bias
        e = jnp.exp(s - jnp.max(s, axis=1, keepdims=True))
        o, c0 = _dot(e[:, s.shape[1] - LANES:], new_v), 0
        for v_t in old_v:
            o = o + _dot_nt(e[:, c0:c0 + v_t.shape[1]], v_t)
            c0 += v_t.shape[1]
        return o / jnp.sum(e, axis=1, keepdims=True)

    picked = []
    n_blocks = -(-(past + nq) // SLC_LEN)
    for psum in _cmp_branch(q, ck, cv, bcmp_ref, ocmp_ref):
        sel = _select_blocks(_dot_f32(psum, ov_ref[...]), qpos, 1, n_blocks)
        picked += [jnp.dot(sel.astype(BF16), exp_ref[...], preferred_element_type=F32)] * REP
    qpad = jnp.concatenate([_padded_queries(q, g) for g in range(HKV)], axis=0)
    bslc = jnp.where(jnp.concatenate(picked, axis=0) > 0.5, bslc_ref[...].reshape(HQ * nq, -1), NEG)
    o = attend(qpad, [pr[0, 2 * LANES:3 * LANES, :] for pr in page_refs],
               [pr[0, 3 * LANES:4 * LANES, :] for pr in page_refs],
               new_kv[:, 2 * LANES:3 * LANES], new_kv[:, 3 * LANES:4 * LANES], bslc)
    _store_all_heads(oslc_ref, o, nq)
    o = attend(qpad, [wbuf_ref[:LANES, :]], [wbuf_ref[LANES:, :]], new_w[:, :LANES], new_w[:, LANES:],
               bwin_ref[...].reshape(HQ * nq, -1))
    _store_all_heads(owin_ref, o, nq)


def _nsa_sample(zq, kvn, wn, cache_t, page_base, page_table, win_t, win_base, ckv,
                bias_cmp, bias_slc, bias_win, ov, expand):
    n_seq, nq, _ = zq.shape
    n_pages = page_table.shape[1]
    page = cache_t.shape[2]
    past = n_pages * page
    win_buf = win_t.shape[2]
    ncp = ckv.shape[2]
    n_item = _pick_tile(n_seq, (2, 1))
    seq3 = lambda b, pt: (b, 0, 0)
    const = lambda nd: (lambda b, pt: (0,) * nd)
    page_spec = lambda it, p: pl.BlockSpec((1, 4 * LANES, page),
                                           lambda b, pt: (page_base + pt[b * n_item + it, p], 0, 0))
    in_specs = [page_spec(it, p) for it in range(n_item) for p in range(n_pages)] + [
        pl.BlockSpec((n_item, nq, D_A), seq3),
        pl.BlockSpec((n_item, nq, 4 * LANES), seq3),
        pl.BlockSpec((n_item, nq, 2 * LANES), seq3),
        pl.BlockSpec((n_item, 2 * LANES, win_buf), lambda b, pt: (win_base // n_item + b, 0, 0)),
        pl.BlockSpec((n_item, 2, ncp, LANES), lambda b, pt: (b, 0, 0, 0)),
        pl.BlockSpec(bias_cmp.shape, const(3)),
        pl.BlockSpec(bias_slc.shape, const(3)),
        pl.BlockSpec(bias_win.shape, const(3)),
        pl.BlockSpec(ov.shape, const(2)),
        pl.BlockSpec(expand.shape, const(2))]
    out = pl.BlockSpec((n_item, nq, D_A), seq3)
    return pl.pallas_call(
        functools.partial(_nsa_sample_kernel, n_pages=n_pages, nq=nq, past=past, n_item=n_item),
        grid_spec=pltpu.PrefetchScalarGridSpec(
            num_scalar_prefetch=1, grid=(n_seq // n_item,), in_specs=in_specs, out_specs=[out, out, out],
            scratch_shapes=[pltpu.VMEM((LANES, 4 * LANES), F32), pltpu.VMEM((LANES, 2 * LANES), F32)] * n_item),
        out_shape=[jax.ShapeDtypeStruct((n_seq, nq, D_A), F32)] * 3,
        compiler_params=_cparams("parallel"),
    )(page_table, *([cache_t] * (n_item * n_pages)), zq, kvn, wn, win_t, ckv, bias_cmp, bias_slc, bias_win, ov, expand)


def _merge_kernel(x_ref, mout_ref, ocmp_ref, oslc_ref, owin_ref, zg_ref, zgm_ref, gexp_ref, wb_ref, wo_ref,
                  lg_ref, lb_ref, o_ref, *, alpha):
    d = x_ref.shape[1]
    gate = jax.nn.sigmoid(zg_ref[...])
    a = None
    for br, src in enumerate((ocmp_ref, oslc_ref, owin_ref)):
        term = _dot_f32(gate, gexp_ref[br]) * src[...]
        a = term if a is None else a + term
    u = (jax.nn.sigmoid(zgm_ref[:, :d]) * _dot(mout_ref[...], wb_ref[0])
         + jax.nn.sigmoid(zgm_ref[:, d:]) * _dot(a, wb_ref[1]))
    y = _dot(u, wo_ref[...])
    o_ref[...] = _layer_norm(alpha * x_ref[...] + y, lg_ref[...], lb_ref[...])


def _merge(x, mout, ocmp, oslc, owin, zg, zgm, gexp, wb, wo, lg, lb, alpha):
    n_tok, d = x.shape
    tm = _pick_tile(n_tok, (256, 128, 64, 32, 16, 8))
    row = lambda i: (i, 0)
    c2 = lambda i: (0, 0)
    c3 = lambda i: (0, 0, 0)
    return pl.pallas_call(
        functools.partial(_merge_kernel, alpha=alpha),
        grid=(n_tok // tm,),
        in_specs=[pl.BlockSpec((tm, d), row), pl.BlockSpec((tm, D_M), row), pl.BlockSpec((tm, D_A), row),
                  pl.BlockSpec((tm, D_A), row), pl.BlockSpec((tm, D_A), row), pl.BlockSpec((tm, LANES), row),
                  pl.BlockSpec((tm, 2 * d), row), pl.BlockSpec(gexp.shape, c3), pl.BlockSpec(wb.shape, c3),
                  pl.BlockSpec(wo.shape, c2), pl.BlockSpec((1, d), c2), pl.BlockSpec((1, d), c2)],
        out_specs=pl.BlockSpec((tm, d), row),
        out_shape=jax.ShapeDtypeStruct((n_tok, d), F32),
        compiler_params=_cparams("parallel"),
    )(x, mout, ocmp, oslc, owin, zg, zgm, gexp, wb, wo, lg.reshape(1, d), lb.reshape(1, d))


def _route(aff, sel):
    def top2_sum(a, b, c, d):
        x, x2, y, y2 = jnp.maximum(a, b), jnp.minimum(a, b), jnp.maximum(c, d), jnp.minimum(c, d)
        return jnp.maximum(x, y) + jnp.maximum(jnp.minimum(x, y), jnp.maximum(x2, y2))

    gsum = [top2_sum(*sel[EXP_PER_GROUP * gi:EXP_PER_GROUP * (gi + 1)]) for gi in range(N_GROUPS)]
    gmax = functools.reduce(jnp.maximum, gsum)
    chosen, taken = [], None
    for gi in range(N_GROUPS):
        is_best = gsum[gi] == gmax if taken is None else (gsum[gi] == gmax) & jnp.logical_not(taken)
        taken = is_best if taken is None else taken | is_best
        members = range(EXP_PER_GROUP * gi, EXP_PER_GROUP * (gi + 1))
        for e in members:
            ahead = None
            for e2 in members:
                if e2 == e:
                    continue
                before = (sel[e2] >= sel[e]) if e2 < e else (sel[e2] > sel[e])
                cnt = jnp.where(before, 1.0, 0.0)
                ahead = cnt if ahead is None else ahead + cnt
            chosen.append(is_best & (ahead < TOP_K))
    picked = [jnp.where(c, a, 0.0) for c, a in zip(chosen, aff)]
    total = functools.reduce(lambda a, b: a + b, picked)
    return [p / total for p in picked]


def _moe_kernel(x_ref, wr_ref, rb_ref, wg_ref, wu_ref, wd_ref, lg_ref, lb_ref, o_ref, xb, comb, acc, *, alpha):
    step = pl.program_id(1)
    tm = x_ref.shape[0]
    lane = lax.broadcasted_iota(jnp.int32, (tm, LANES), 1)

    @pl.when(step == 0)
    def _():
        x16 = x_ref[...].astype(BF16)
        xb[...] = x16
        aff_t = jax.nn.sigmoid(_dot_nt(wr_ref[...], x16))
        sel_t = aff_t + rb_ref[...]
        weights = _route([aff_t[i:i + 1, :] for i in range(N_EXP)], [sel_t[i:i + 1, :] for i in range(N_EXP)])
        expert = lax.broadcasted_iota(jnp.int32, (N_EXP, tm), 0)
        comb_t = jnp.zeros((N_EXP, tm), F32)
        for i, w in enumerate(weights):
            comb_t = jnp.where(expert == i, w, comb_t)
        comb[...] = jnp.concatenate([comb_t, jnp.zeros((LANES - N_EXP, tm), F32)], axis=0).T
        acc[...] = jnp.zeros_like(acc)

    x16 = xb[...]
    hs = []
    for j in range(EXP_PER_STEP):
        w_e = jnp.sum(jnp.where(lane == EXP_PER_STEP * step + j, comb[...], 0.0), axis=1, keepdims=True)
        hg = jnp.dot(x16, wg_ref[0, j], preferred_element_type=F32)
        hu = jnp.dot(x16, wu_ref[0, j], preferred_element_type=F32)
        hs.append((hg * jax.nn.sigmoid(hg) * hu * w_e).astype(BF16))
    d = acc.shape[1]
    acc[...] += jnp.dot(jnp.concatenate(hs, axis=1), wd_ref[0].reshape(EXP_PER_STEP * D_EXP, d),
                        preferred_element_type=F32)

    @pl.when(step == pl.num_programs(1) - 1)
    def _():
        o_ref[...] = _layer_norm(alpha * x_ref[...] + acc[...], lg_ref[...], lb_ref[...])


def _moe(x, wr, rb, wg, wu, wd, layer, lg, lb, alpha):
    n_tok, d = x.shape
    tm = _pick_tile(n_tok, (1024, 512, 256, 128, 64, 32, 16, 8))
    row = lambda i, e: (i, 0)
    c2 = lambda i, e: (0, 0)
    wsel = lambda i, e: (layer, e, 0, 0)
    return pl.pallas_call(
        functools.partial(_moe_kernel, alpha=alpha),
        grid=(n_tok // tm, N_EXP // EXP_PER_STEP),
        in_specs=[pl.BlockSpec((tm, d), row), pl.BlockSpec(wr.shape, c2), pl.BlockSpec(rb.shape, c2),
                  pl.BlockSpec((1, EXP_PER_STEP, d, D_EXP), wsel), pl.BlockSpec((1, EXP_PER_STEP, d, D_EXP), wsel),
                  pl.BlockSpec((1, EXP_PER_STEP, D_EXP, d), wsel), pl.BlockSpec((1, d), c2), pl.BlockSpec((1, d), c2)],
        out_specs=pl.BlockSpec((tm, d), row),
        out_shape=jax.ShapeDtypeStruct((n_tok, d), F32),
        scratch_shapes=[pltpu.VMEM((tm, d), BF16), pltpu.VMEM((tm, LANES), F32), pltpu.VMEM((tm, d), F32)],
        compiler_params=_cparams("parallel", "arbitrary"),
    )(x, wr, rb, wg, wu, wd, lg.reshape(1, d), lb.reshape(1, d))


def _bucket_np(dist):
    n = np.maximum(dist, 0)
    exact = N_BUCKETS // 2
    nf = np.maximum(n, 1).astype(np.float32)
    large = exact + (np.log(nf / np.float32(exact)) / np.float32(math.log(REL_MAX_DIST / exact))
                     * np.float32(N_BUCKETS - exact)).astype(np.int32)
    return np.where(n < exact, n, np.minimum(large, N_BUCKETS - 1)).astype(np.int32)


def _bias_table(rel_bias, dist, valid):
    tab = rel_bias.astype(F32)[jnp.asarray(_bucket_np(dist))]
    tab = jnp.where(jnp.asarray(valid)[..., None], tab, NEG)
    return jnp.moveaxis(tab, -1, 0)


def _bias_tiles_kernel(rb_ref, o_ref, *, tile_step, key_stride, key_off, hi_valid, upper, lead):
    t = pl.program_id(0)
    shape = o_ref.shape[-2:]
    dist = (tile_step * t - key_off + lax.broadcasted_iota(jnp.int32, shape, 0)
            - key_stride * lax.broadcasted_iota(jnp.int32, shape, 1))
    valid = dist >= 0 if hi_valid is None else (dist >= 0) & (dist < hi_valid)
    acc = [jnp.full(shape, rb_ref[N_BUCKETS - 1, h], F32) for h in range(HQ)]
    for b in range(N_BUCKETS - 2, -1, -1):
        below = dist < upper[b]
        acc = [jnp.where(below, rb_ref[b, h], a) for h, a in enumerate(acc)]
    for h in range(HQ):
        if lead:
            o_ref[0, h] = jnp.where(valid, acc[h], NEG)
        else:
            o_ref[h] = jnp.where(valid, acc[h], NEG)


def _bias_tiles(rel_bias, n_tiles, n_keys, *, tile_step, key_stride, key_off, hi_valid, lead):
    buckets = _bucket_np(np.arange(8 * REL_MAX_DIST))
    upper = tuple(int(np.searchsorted(buckets, b, side='right')) for b in range(N_BUCKETS - 1))
    if lead:
        out_spec = pl.BlockSpec((1, HQ, QT, n_keys), lambda t: (t, 0, 0, 0))
        out_shape = jax.ShapeDtypeStruct((n_tiles, HQ, QT, n_keys), F32)
    else:
        out_spec = pl.BlockSpec((HQ, QT, n_keys), lambda t: (0, t, 0))
        out_shape = jax.ShapeDtypeStruct((HQ, n_tiles * QT, n_keys), F32)
    return pl.pallas_call(
        functools.partial(_bias_tiles_kernel, tile_step=tile_step, key_stride=key_stride, key_off=key_off,
                          hi_valid=hi_valid, upper=upper, lead=lead),
        grid=(n_tiles,),
        in_specs=[pl.BlockSpec(memory_space=pltpu.SMEM)],
        out_specs=out_spec, out_shape=out_shape,
        compiler_params=_cparams("parallel"),
    )(rel_bias.astype(F32))


def _overlap_np(ncp, n_cmp, n_slc):
    c0 = np.arange(ncp)[:, None] * CMP_STRIDE
    j0 = np.arange(NSP)[None, :] * SLC_LEN
    ov = (c0 < j0 + SLC_LEN) & (c0 + CMP_LEN > j0)
    ov &= (np.arange(ncp)[:, None] < n_cmp) & (np.arange(NSP)[None, :] < n_slc)
    return ov.astype(np.float32)


def kernel(x_prompt, x_sample, cache_kv, state_win_kv, state_mlstm_C, state_mlstm_n, state_mlstm_m, page_table,
           w_in, b_in, w_cmp1, w_cmp2, mh_norm_g, w_branch, w_out, ln1_g, ln1_b, ln2_g, ln2_b,
           w_router, router_bias, w_gate_e, w_up_e, w_down_e, rel_bias):
    B, T, D = x_prompt.shape
    DB, TS, _ = x_sample.shape
    depth, n_pool, page = cache_kv.shape[:3]
    n_pages = page_table.shape[1]
    past = n_pages * page
    win_buf = state_win_kv.shape[2]
    alpha = (2 * depth) ** 0.25
    assert T % KT == 0 and T // SLC_LEN <= NSP and page % CMP_STRIDE == 0 and past % SLC_LEN == 0
    assert win_buf == min(WINDOW, past) and T >= win_buf

    off = np.cumsum((0, D_M, D_M, D_M, D_M, H_M, H_M, D_A, 6 * HKV * HD, 3 * HQ, 2 * D)).tolist()
    seg = lambda a, i, j: a[..., off[i]:off[j]]
    small = lambda a: jnp.concatenate(
        [seg(a, 4, 6), seg(a, 8, 9), jnp.zeros(a.shape[:-1] + (LANES - 2 * H_M - 3 * HQ,), a.dtype)], axis=-1)
    kv_cmp, kv_mid = off[7] + 2 * HKV * HD, off[7] + 4 * HKV * HD
    groups_s = lambda a: (seg(a, 0, 4), small(a), seg(a, 6, 7), a[..., off[7]:kv_mid], a[..., kv_mid:off[8]], seg(a, 9, 10))
    groups_p = lambda a: (seg(a, 0, 4), small(a), seg(a, 6, 7), a[..., off[7]:kv_cmp], seg(a, 9, 10))
    ws_s = [w.astype(BF16) for w in groups_s(w_in)]
    bs_s = [b[:, None, :] for b in groups_s(b_in)]
    ws_p = [w.astype(BF16) for w in groups_p(w_in)]
    bs_p = [b[:, None, :] for b in groups_p(b_in)]
    w_in_t = jnp.transpose(w_in, (0, 2, 1))
    wt_p = w_in_t[:, off[7]:off[8]].astype(BF16)
    bt_p = b_in[:, off[7]:off[8], None]
    wb16, wo16 = w_branch.astype(BF16), w_out.astype(BF16)
    wg16, wu16, wd16 = w_gate_e.astype(BF16), w_up_e.astype(BF16), w_down_e.astype(BF16)
    wr_pad = jnp.transpose(w_router).astype(BF16)
    rb_pad = router_bias.astype(F32).reshape(N_EXP, 1)
    eye = jnp.eye(HKV, dtype=F32)
    w1 = w_cmp1.reshape(depth, 2, 2, CMP_STRIDE, HD, CMP_HID)
    w1p = jnp.einsum('lshrdf,gG->lsrgdhGf', w1, eye).reshape(depth, 2, CMP_STRIDE, HKV * HD, 2 * HKV * CMP_HID)
    w1p = w1p.astype(BF16)
    w2p = jnp.einsum('lsfd,gG->lsgfGd', w_cmp2, eye).reshape(depth, 2, HKV * CMP_HID, HKV * HD).astype(BF16)
    gexp = np.zeros((3, LANES, D_A), np.float32)
    for br in range(3):
        for h in range(HQ):
            gexp[br, 2 * H_M + br * HQ + h, h * HD:(h + 1) * HD] = 1.0
    gexp = jnp.asarray(gexp)

    ncp_p, n_cmp_p = T // CMP_STRIDE, (T - CMP_LEN) // CMP_STRIDE + 1
    assert n_cmp_p * CMP_STRIDE + CMP_LEN - 1 > T - 1
    bias_cmp_p = _bias_tiles(rel_bias, T // QT, ncp_p, tile_step=QT, key_stride=CMP_STRIDE, key_off=CMP_LEN - 1,
                             hi_valid=None, lead=False)
    ov_t_p = jnp.asarray(_overlap_np(ncp_p, n_cmp_p, -(-T // SLC_LEN)).T)
    n_far = -(-(KT - 1 + REL_MAX_DIST) // QT)
    bias_slc_p = _bias_tiles(rel_bias, n_far + 1, KT, tile_step=QT, key_stride=1, key_off=0, hi_valid=None, lead=True)
    bias_win_p = _bias_tiles(rel_bias, (WINDOW + KT) // QT, KT, tile_step=QT, key_stride=1, key_off=0,
                             hi_valid=WINDOW, lead=True)

    ncp_s = past // CMP_STRIDE
    n_cmp_s = (past + TS - CMP_LEN) // CMP_STRIDE + 1
    assert n_cmp_s <= ncp_s and -(-(past + TS) // SLC_LEN) <= NSP and TS <= NQS
    qs = past + np.arange(NQS)[:, None]
    d_cmp_s = qs - (np.arange(ncp_s)[None, :] * CMP_STRIDE + CMP_LEN - 1)
    bias_cmp_s = _bias_table(rel_bias, d_cmp_s, (d_cmp_s >= 0) & (np.arange(ncp_s)[None, :] < n_cmp_s))
    ov_s = jnp.asarray(_overlap_np(ncp_s, n_cmp_s, -(-(past + TS) // SLC_LEN)))
    key_s = np.arange(past + LANES)[None, :]
    d_slc_s = qs - key_s
    bias_slc_s = _bias_table(rel_bias, d_slc_s, (d_slc_s >= 0) & (key_s < past + TS))
    expand_s = jnp.asarray((np.arange(NSP)[:, None] == key_s // SLC_LEN).astype(np.float32)).astype(BF16)
    idx_w = np.arange(win_buf + LANES)[None, :]
    d_win_s = qs - (past - win_buf + idx_w)
    bias_win_s = _bias_table(rel_bias, d_win_s, (d_win_s >= 0) & (d_win_s < WINDOW) & (idx_w < win_buf + TS))

    to_feature_major = lambda a: jnp.transpose(a, (0, 1, 3, 4, 5, 2))
    cache_t = to_feature_major(cache_kv).reshape(depth * n_pool, 4 * LANES, page)
    win_t = to_feature_major(state_win_kv).reshape(depth * DB, 2 * LANES, win_buf)
    from_feature_major = lambda a, slots: jnp.transpose(
        a.reshape(a.shape[:2] + (slots, HKV, HD, a.shape[-1])), (0, 1, 5, 2, 3, 4))
    half_per_page = page // CMP_STRIDE

    LP = _pick_tile(T, (256, 128, 64))
    LS = NQS
    zeros_state = (jnp.zeros((B, H_M, DH_M, DH_M), F32), jnp.zeros((B, H_M, DH_M), F32), jnp.zeros((B, H_M), F32))

    xp = x_prompt.reshape(B * T, D)
    xs = x_sample.reshape(DB * TS, D)
    outs = [[] for _ in range(10)]
    for l in range(depth):
        zm, zg, zq, zc, zgm, zkv_t, zw_t, z16_t = _inproj(xp, [w[l] for w in ws_p], [b[l] for b in bs_p],
                                                          wt_p[l], bt_p[l], t_split=4 * LANES, n_seq=B)
        mout, c_p, n_p, m_p = _mlstm(zm, zg, mh_norm_g[l], *zeros_state, n_seq=B, L=LP, l_valid=LP)
        part = _cmp_partial(zc.reshape(B * T // CMP_STRIDE, CMP_STRIDE * 2 * LANES), w1p[l])
        ckv = _cmp_finish_prompt(part, w2p[l], B)
        ocmp, sel = _nsa_cmp_prompt(zq, ckv, bias_cmp_p, ov_t_p, B)
        oslc = _flash_prompt(zq, z16_t, 2, 3, bias_slc_p, sel, B, window=False)
        owin = _flash_prompt(zq, z16_t, 4, 5, bias_win_p, None, B, window=True)
        x1 = _merge(xp, mout, ocmp, oslc, owin, zg, zgm, gexp, wb16[l], wo16[l], ln1_g[l], ln1_b[l], alpha)
        xp = _moe(x1, wr_pad, rb_pad, wg16, wu16, wd16, l, ln2_g[l], ln2_b[l], alpha)
        outs[0].append(zkv_t)
        outs[1].append(zw_t[:, :, T - win_buf:])
        outs[2].append(c_p)
        outs[3].append(n_p)
        outs[4].append(m_p)
        zm, zg, zq, zkv, zw, zgm = _inproj(xs, [w[l] for w in ws_s], [b[l] for b in bs_s])
        padt = lambda a: jnp.pad(a.reshape(DB, TS, -1), ((0, 0), (0, LS - TS), (0, 0))).reshape(DB * LS, -1)
        mout, c_s, n_s, m_s = _mlstm(padt(zm), padt(zg), mh_norm_g[l], state_mlstm_C[l], state_mlstm_n[l],
                                     state_mlstm_m[l], n_seq=DB, L=LS, l_valid=TS)
        mout = mout.reshape(DB, LS, D_M)[:, :TS].reshape(DB * TS, D_M)
        part = _cmp_partial_paged(cache_t, w1p[l].reshape(2, CMP_STRIDE // 2, 2 * LANES, -1), l, n_pool)
        ckv = _cmp_finish_paged(part, w2p[l], page_table, 0, half_per_page)
        padq = lambda a: jnp.pad(a.reshape(DB, TS, -1), ((0, 0), (0, NQS - TS), (0, 0)))
        o3 = _nsa_sample(padq(zq), padq(zkv), padq(zw), cache_t, l * n_pool, page_table, win_t, l * DB, ckv,
                         bias_cmp_s, bias_slc_s, bias_win_s, ov_s, expand_s)
        ocmp, oslc, owin = [o[:, :TS].reshape(DB * TS, D_A) for o in o3]
        x1 = _merge(xs, mout, ocmp, oslc, owin, zg, zgm, gexp, wb16[l], wo16[l], ln1_g[l], ln1_b[l], alpha)
        xs = _moe(x1, wr_pad, rb_pad, wg16, wu16, wd16, l, ln2_g[l], ln2_b[l], alpha)
        zw5 = zw.reshape(DB, TS, 2, HKV, HD)
        outs[5].append(zkv.reshape(DB, TS, 4, HKV, HD))
        outs[6].append(jnp.concatenate([state_win_kv[l], zw5], axis=1)[:, TS:])
        outs[7].append(c_s)
        outs[8].append(n_s)
        outs[9].append(m_s)
    stacked = [jnp.stack(o) for o in outs]
    stacked[0] = from_feature_major(stacked[0], 4)
    stacked[1] = from_feature_major(stacked[1], 2)
    return (xp.reshape(B, T, D), xs.reshape(DB, TS, D)) + tuple(stacked)
```

```python
import functools
import math

import numpy as np
import jax
import jax.numpy as jnp
from jax import lax
from jax.experimental import pallas as pl
from jax.experimental.pallas import tpu as pltpu

F32 = jnp.float32
BF16 = jnp.bfloat16
HIGHEST = lax.Precision.HIGHEST

H_M, DH_M = 4, 128
D_M = H_M * DH_M
HQ, HKV, HD = 8, 2, 64
REP = HQ // HKV
D_A = HQ * HD
CMP_LEN, CMP_STRIDE, CMP_HID = 32, 16, 256
SLC_LEN, N_SEL, WINDOW = 64, 16, 512
N_BUCKETS, REL_MAX_DIST = 32, 128
N_EXP, N_GROUPS, TOP_K, D_EXP = 16, 4, 2, 256
EXP_PER_GROUP = N_EXP // N_GROUPS
EXP_PER_STEP = 2
LN_EPS = 1e-5
NEG = -1e30

LANES = 128
QT = 128
KT = 512
KT_WIN = 512
LOG2E = math.log2(math.e)
ROWS = 32
NSP = 64
NQS = 8
VMEM_LIMIT = 56 * 1024 * 1024


def _cparams(*sem):
    return pltpu.CompilerParams(dimension_semantics=sem, vmem_limit_bytes=VMEM_LIMIT)


def _dot(a, b):
    return jnp.dot(a.astype(BF16), b.astype(BF16), preferred_element_type=F32)


def _dot_nt(a, b):
    return lax.dot_general(a.astype(BF16), b.astype(BF16), (((1,), (1,)), ((), ())), preferred_element_type=F32)


def _dot_tn(a, b):
    return lax.dot_general(a.astype(BF16), b.astype(BF16), (((0,), (0,)), ((), ())), preferred_element_type=F32)


def _dot_f32(a, b):
    return jnp.dot(a, b, precision=HIGHEST, preferred_element_type=F32)


def _dot_nt_f32(a, b):
    return lax.dot_general(a, b, (((1,), (1,)), ((), ())), precision=HIGHEST, preferred_element_type=F32)


def _pick_tile(n, cands):
    for c in cands:
        if n % c == 0:
            return c
    raise ValueError(f"no tile for {n}")


def _layer_norm(y, g, b):
    mu = jnp.mean(y, axis=-1, keepdims=True)
    yc = y - mu
    var = jnp.mean(yc * yc, axis=-1, keepdims=True)
    return yc * lax.rsqrt(var + LN_EPS) * g + b


def _inproj_kernel(x_ref, *refs, n_plain, t_split):
    n_w = n_plain + (1 if t_split else 0)
    w_refs, b_refs, o_refs = refs[:n_plain], refs[n_w:n_w + n_plain], refs[2 * n_w:]
    x = x_ref[...].astype(BF16)
    for w_ref, b_ref, o_ref in zip(w_refs, b_refs, o_refs):
        o_ref[...] = jnp.dot(x, w_ref[...], preferred_element_type=F32) + b_ref[...]
    if t_split:
        wt_ref, bt_ref = refs[n_plain], refs[n_w + n_plain]
        zt = lax.dot_general(wt_ref[...], x, (((1,), (1,)), ((), ())), preferred_element_type=F32) + bt_ref[...]
        lo_ref, hi_ref, all16_ref = o_refs[n_plain:]
        lo_ref[0] = zt[:t_split]
        hi_ref[0] = zt[t_split:]
        all16_ref[0] = zt.astype(BF16)


def _inproj(x, ws, bs, wt=None, bt=None, t_split=0, n_seq=1):
    n_tok, d = x.shape
    t = n_tok // n_seq
    tm = _pick_tile(t, (256, 128, 64, 32, 16, 8))
    nt = t // tm
    full = lambda i: (0, 0)
    row = lambda i: (i, 0)
    tspec = lambda n: pl.BlockSpec((1, n, tm), lambda i: (i // nt, 0, i % nt))
    extra_w = [] if wt is None else [wt]
    extra_b = [] if wt is None else [bt]
    n_t = 0 if wt is None else wt.shape[0]
    t_specs = [] if wt is None else [tspec(t_split), tspec(n_t - t_split), tspec(n_t)]
    t_shapes = [] if wt is None else [jax.ShapeDtypeStruct((n_seq, t_split, t), F32),
                                      jax.ShapeDtypeStruct((n_seq, n_t - t_split, t), F32),
                                      jax.ShapeDtypeStruct((n_seq, n_t, t), BF16)]
    return pl.pallas_call(
        functools.partial(_inproj_kernel, n_plain=len(ws), t_split=t_split if wt is not None else 0),
        grid=(n_tok // tm,),
        in_specs=[pl.BlockSpec((tm, d), row)]
        + [pl.BlockSpec(w.shape, full) for w in (*ws, *extra_w)]
        + [pl.BlockSpec(b.shape, full) for b in (*bs, *extra_b)],
        out_specs=[pl.BlockSpec((tm, w.shape[1]), row) for w in ws] + t_specs,
        out_shape=[jax.ShapeDtypeStruct((n_tok, w.shape[1]), F32) for w in ws] + t_shapes,
        compiler_params=_cparams("parallel"),
    )(x, *ws, *extra_w, *bs, *extra_b)


def _log_sigmoid(x):
    return jnp.minimum(x, 0.0) - jnp.log(1.0 + jnp.exp(-jnp.abs(x)))


def _mlstm_kernel(q_ref, k_ref, v_ref, o_ref, g_ref, ng_ref, c0_ref, n0_ref, m0_ref,
                  mout_ref, c_ref, n_ref, m_ref, cs, ns, ms, *, L, l_valid):
    c = pl.program_id(1)

    @pl.when(c == 0)
    def _():
        cs[...] = c0_ref[0]
        ns[...] = n0_ref[0]
        ms[...] = m0_ref[0]

    g = g_ref[...]
    row = lax.broadcasted_iota(jnp.int32, (L, L), 0)
    col = lax.broadcasted_iota(jnp.int32, (L, L), 1)
    causal = row >= col
    fcum = _dot_f32(causal.astype(F32), _log_sigmoid(g))
    lane = lax.broadcasted_iota(jnp.int32, (L, LANES), 1)
    y = jnp.where(lane < H_M, g - pltpu.roll(fcum, LANES - H_M, 1), fcum)
    yt = y.T
    rowv = lax.broadcasted_iota(jnp.int32, (L, 1), 0)
    r = l_valid - 1
    for h in range(H_M):
        hs = slice(h * DH_M, (h + 1) * DH_M)
        f_col = fcum[:, H_M + h:H_M + h + 1]
        a_row = yt[h:h + 1, :]
        i_col = g[:, h:h + 1]
        m0 = ms[h][:, 0:1]
        n0 = ns[h]
        c0 = cs[h]
        dm = jnp.where(causal, f_col + a_row, NEG)
        b = f_col + m0
        mrow = jnp.maximum(b, jnp.max(dm, axis=1, keepdims=True))
        w = jnp.exp(dm - mrow)
        dec = jnp.exp(b - mrow)
        q = q_ref[:, hs]
        k = k_ref[:, hs] * (DH_M ** -0.5)
        v = v_ref[:, hs]
        s = _dot_nt(q, k) * w
        num = dec * _dot_nt(q, c0) + _dot(s, v)
        den = dec * jnp.sum(q * n0, axis=1, keepdims=True) + jnp.sum(s, axis=1, keepdims=True)
        hh = num / jnp.maximum(jnp.abs(den), jnp.exp(-mrow))
        mu = jnp.mean(hh, axis=1, keepdims=True)
        hc = hh - mu
        var = jnp.mean(hc * hc, axis=1, keepdims=True)
        hn = hc * lax.rsqrt(var + LN_EPS) * ng_ref[:, hs]
        mout_ref[:, hs] = (hn * jax.nn.sigmoid(o_ref[:, hs])).astype(mout_ref.dtype)
        f_r = fcum[r:r + 1, H_M + h:H_M + h + 1]
        m_r = mrow[r:r + 1, :]
        w_last = jnp.where(rowv <= r, jnp.exp(f_r - f_col + i_col - m_r), 0.0)
        d_last = dec[r:r + 1, :]
        cs[h] = d_last * c0 + _dot_tn(v * w_last, k)
        ns[h] = d_last * n0 + jnp.sum(k * w_last, axis=0, keepdims=True)
        ms[h] = jnp.broadcast_to(m_r, (1, LANES))

    @pl.when(c == pl.num_programs(1) - 1)
    def _():
        c_ref[0] = cs[...]
        n_ref[0] = ns[...]
        m_ref[0] = ms[...]


def _mlstm(zm, zg, norm_g, c0, n0, m0, *, n_seq, L, l_valid, c_base=0):
    n_tok = zm.shape[0]
    t = n_tok // n_seq
    nc = t // L
    n0 = n0.reshape(n_seq, H_M, 1, DH_M)
    m0 = jnp.broadcast_to(m0.reshape(n_seq, H_M, 1, 1), (n_seq, H_M, 1, LANES))
    colblk = lambda j: (lambda b, c: (b * nc + c, j))
    st4 = lambda b, c: (b, 0, 0, 0)
    c0_spec = pl.BlockSpec((1, H_M, DH_M, DH_M), lambda b, c: (c_base + b, 0, 0, 0))
    mout, c_new, n_new, m_new = pl.pallas_call(
        functools.partial(_mlstm_kernel, L=L, l_valid=l_valid),
        grid=(n_seq, nc),
        in_specs=[pl.BlockSpec((L, D_M), colblk(0)), pl.BlockSpec((L, D_M), colblk(1)),
                  pl.BlockSpec((L, D_M), colblk(2)), pl.BlockSpec((L, D_M), colblk(3)),
                  pl.BlockSpec((L, LANES), colblk(0)),
                  pl.BlockSpec((1, D_M), lambda b, c: (0, 0)),
                  c0_spec,
                  pl.BlockSpec((1, H_M, 1, DH_M), st4),
                  pl.BlockSpec((1, H_M, 1, LANES), st4)],
        out_specs=[pl.BlockSpec((L, D_M), colblk(0)),
                   pl.BlockSpec((1, H_M, DH_M, DH_M), st4),
                   pl.BlockSpec((1, H_M, 1, DH_M), st4),
                   pl.BlockSpec((1, H_M, 1, LANES), st4)],
        out_shape=[jax.ShapeDtypeStruct((n_tok, D_M), BF16),
                   jax.ShapeDtypeStruct((n_seq, H_M, DH_M, DH_M), F32),
                   jax.ShapeDtypeStruct((n_seq, H_M, 1, DH_M), F32),
                   jax.ShapeDtypeStruct((n_seq, H_M, 1, LANES), F32)],
        scratch_shapes=[pltpu.VMEM((H_M, DH_M, DH_M), F32), pltpu.VMEM((H_M, 1, DH_M), F32),
                        pltpu.VMEM((H_M, 1, LANES), F32)],
        compiler_params=_cparams("parallel", "arbitrary"),
    )(zm, zm, zm, zm, zg, norm_g.reshape(1, D_M), c0, n0, m0)
    return mout, c_new, n_new.reshape(n_seq, H_M, DH_M), m_new[:, :, 0, 0]


def _cmp_partial_kernel(x_ref, w_ref, o_ref):
    @pl.when(pl.program_id(2) == 0)
    def _():
        o_ref[...] = jnp.zeros_like(o_ref)

    o_ref[0] += _dot(x_ref[...], w_ref[0, 0])


def _cmp_partial(rows16, w1p):
    n_half = rows16.shape[0]
    tm = _pick_tile(n_half, (2048, 1024, 512, 256, 128, 64, 32, 16, 8))
    hid2 = 2 * HKV * CMP_HID
    return pl.pallas_call(
        _cmp_partial_kernel,
        grid=(2, n_half // tm, CMP_STRIDE),
        in_specs=[pl.BlockSpec((tm, LANES), lambda s, i, r: (i, 2 * r + s)),
                  pl.BlockSpec((1, 1, LANES, hid2), lambda s, i, r: (s, r, 0, 0))],
        out_specs=pl.BlockSpec((1, tm, hid2), lambda s, i, r: (s, i, 0)),
        out_shape=jax.ShapeDtypeStruct((2, n_half, hid2), F32),
        compiler_params=_cparams("parallel", "parallel", "arbitrary"),
    )(rows16, w1p)


def _cmp_partial_paged_kernel(x_ref, perm_ref, w_ref, o_ref, tok, *, pages, page):
    half_per_page = page // CMP_STRIDE
    for s in range(2):
        for p in range(pages):
            xt = _dot_nt(perm_ref[...], x_ref[p, s * LANES:(s + 1) * LANES, :])
            tok[:, p * half_per_page:(p + 1) * half_per_page, :] = xt.reshape(CMP_STRIDE, half_per_page, LANES)
        acc = None
        for r in range(0, CMP_STRIDE, 2):
            part = _dot(jnp.concatenate([tok[r], tok[r + 1]], axis=1), w_ref[s, r // 2])
            acc = part if acc is None else acc + part
        o_ref[s] = acc


def _cmp_partial_paged(cache_t, w1p, layer, n_pool):
    page = cache_t.shape[2]
    pages = _pick_tile(n_pool, (64, 32, 16, 8, 4, 2, 1))
    hid2 = 2 * HKV * CMP_HID
    n_half = n_pool * page // CMP_STRIDE
    half_per_page = page // CMP_STRIDE
    rows = pages * half_per_page
    base = layer * (n_pool // pages)
    tok = np.arange(page)
    perm = np.zeros((page, page), np.float32)
    perm[(tok % CMP_STRIDE) * half_per_page + tok // CMP_STRIDE, tok] = 1.0
    return pl.pallas_call(
        functools.partial(_cmp_partial_paged_kernel, pages=pages, page=page),
        grid=(n_pool // pages,),
        in_specs=[pl.BlockSpec((pages, 2 * LANES, page), lambda i: (base + i, 0, 0)),
                  pl.BlockSpec((page, page), lambda i: (0, 0)),
                  pl.BlockSpec(w1p.shape, lambda i: (0, 0, 0, 0))],
        out_specs=pl.BlockSpec((2, rows, hid2), lambda i: (0, i, 0)),
        out_shape=jax.ShapeDtypeStruct((2, n_half, hid2), F32),
        scratch_shapes=[pltpu.VMEM((CMP_STRIDE, rows, LANES), F32)],
        compiler_params=_cparams("parallel"),
    )(cache_t, jnp.asarray(perm, dtype=BF16), w1p)


def _gelu_tanh(x):
    return 0.5 * x * (1.0 + jnp.tanh(math.sqrt(2.0 / math.pi) * (x + 0.044715 * (x * x * x))))


def _cmp_finish_kernel(*refs, n_in, n_item):
    p_refs = refs[-(n_item * n_in + 2):-2]
    w2_ref, o_ref = refs[-2], refs[-1]
    half = HKV * CMP_HID
    for s in range(2):
        hids = []
        for it in range(n_item):
            pieces = p_refs[it * n_in:(it + 1) * n_in]
            p = jnp.concatenate([pr[s] for pr in pieces], axis=0) if n_in > 1 else pieces[0][s]
            n = p.shape[0]
            first, second = p[:, :half], p[:, half:]
            hids.append(_gelu_tanh(first + pltpu.roll(second, n - 1, 0)))
        out = _dot(jnp.concatenate(hids, axis=0) if n_item > 1 else hids[0], w2_ref[s])
        for it in range(n_item):
            o_ref[it, s] = out[it * n:(it + 1) * n]


def _cmp_finish_prompt(part, w2p, n_seq):
    n_half = part.shape[1] // n_seq
    hid2 = part.shape[2]
    return pl.pallas_call(
        functools.partial(_cmp_finish_kernel, n_in=1, n_item=1),
        grid=(n_seq,),
        in_specs=[pl.BlockSpec((2, n_half, hid2), lambda b: (0, b, 0)),
                  pl.BlockSpec(w2p.shape, lambda b: (0, 0, 0))],
        out_specs=pl.BlockSpec((1, 2, n_half, LANES), lambda b: (b, 0, 0, 0)),
        out_shape=jax.ShapeDtypeStruct((n_seq, 2, n_half, LANES), F32),
        compiler_params=_cparams("parallel"),
    )(part, w2p)


def _cmp_finish_paged(part, w2p, page_table, page_base, half_per_page):
    n_seq, n_pages = page_table.shape
    hid2 = part.shape[2]
    n_half = n_pages * half_per_page
    n_item = _pick_tile(n_seq, (4, 2, 1))
    page_spec = lambda it, p: pl.BlockSpec((2, half_per_page, hid2),
                                           lambda b, pt: (0, page_base + pt[b * n_item + it, p], 0))
    return pl.pallas_call(
        functools.partial(_cmp_finish_kernel, n_in=n_pages, n_item=n_item),
        grid_spec=pltpu.PrefetchScalarGridSpec(
            num_scalar_prefetch=1, grid=(n_seq // n_item,),
            in_specs=[page_spec(it, p) for it in range(n_item) for p in range(n_pages)]
            + [pl.BlockSpec(w2p.shape, lambda b, pt: (0, 0, 0))],
            out_specs=pl.BlockSpec((n_item, 2, n_half, LANES), lambda b, pt: (b, 0, 0, 0))),
        out_shape=jax.ShapeDtypeStruct((n_seq, 2, n_half, LANES), F32),
        compiler_params=_cparams("parallel"),
    )(page_table, *([part] * (n_item * n_pages)), w2p)


def _cmp_branch(q, ck, cv, bias_ref, ocmp_ref):
    nq = q.shape[0]
    qpad = jnp.concatenate([_padded_queries(q, g) for g in range(HKV)], axis=0)
    bias = bias_ref[...].reshape(HQ * nq, -1)
    s = _dot_nt(qpad, ck) + bias
    e = jnp.exp(s - jnp.max(s, axis=1, keepdims=True))
    p = jnp.where(bias > 0.5 * NEG, e / jnp.sum(e, axis=1, keepdims=True), 0.0)
    _store_all_heads(ocmp_ref, _dot(p, cv), nq)
    psums = []
    for g in range(HKV):
        heads = [p[(g * REP + r) * nq:(g * REP + r + 1) * nq] for r in range(REP)]
        psums.append(functools.reduce(lambda a, b: a + b, heads))
    return psums


def _select_blocks(imp, qpos, blk_axis, n_blocks):
    blk = lax.broadcasted_iota(jnp.int32, imp.shape, blk_axis)
    cur = (qpos // SLC_LEN) == blk
    avail = blk * SLC_LEN <= qpos
    imp = jnp.where(cur, -NEG, jnp.where(avail, imp, NEG))
    cnt = jnp.zeros(imp.shape, F32)
    for j in range(n_blocks):
        other = lax.slice_in_dim(imp, j, j + 1, axis=blk_axis)
        cnt = cnt + jnp.where(blk > j, jnp.where(other >= imp, 1.0, 0.0), jnp.where(other > imp, 1.0, 0.0))
    return jnp.where((cnt < N_SEL) & (imp > 0.5 * NEG), 1.0, 0.0)


def _padded_queries(q, g, dtype=BF16):
    nq = q.shape[0]
    zero = jnp.zeros((nq, HD), F32)
    parts = []
    for r in range(REP):
        h = g * REP + r
        piece = q[:, h * HD:(h + 1) * HD]
        parts.append(jnp.concatenate([piece, zero] if g == 0 else [zero, piece], axis=1))
    return jnp.concatenate(parts, axis=0).astype(dtype)


def _store_all_heads(o_ref, o, nq):
    for h in range(HQ):
        g = h // REP
        o_ref[:, h * HD:(h + 1) * HD] = o[h * nq:(h + 1) * nq, g * HD:(g + 1) * HD]


def _nsa_cmp_prompt_kernel(q_ref, ckv_ref, bias_ref, ovt_ref, ocmp_ref, sel_ref):
    qb = pl.program_id(1)
    q = q_ref[...] * (HD ** -0.5)
    ck, cv = ckv_ref[0, 0], ckv_ref[0, 1]
    qpos = qb * QT + lax.broadcasted_iota(jnp.int32, (1, QT), 1)
    for g, psum in enumerate(_cmp_branch(q, ck, cv, bias_ref, ocmp_ref)):
        imp_t = _dot_nt_f32(ovt_ref[...], psum)
        sel_ref[:, g * NSP:(g + 1) * NSP] = _select_blocks(imp_t, qpos, 0, NSP).T


def _nsa_cmp_prompt(zq, ckv, bias_cmp, ov_t, n_seq):
    n_tok = zq.shape[0]
    nqb = n_tok // n_seq // QT
    ncp = ckv.shape[2]
    tok = lambda b, i: (b * nqb + i, 0)
    return pl.pallas_call(
        _nsa_cmp_prompt_kernel,
        grid=(n_seq, nqb),
        in_specs=[pl.BlockSpec((QT, D_A), tok),
                  pl.BlockSpec((1, 2, ncp, LANES), lambda b, i: (b, 0, 0, 0)),
                  pl.BlockSpec((HQ, QT, ncp), lambda b, i: (0, i, 0)),
                  pl.BlockSpec(ov_t.shape, lambda b, i: (0, 0))],
        out_specs=[pl.BlockSpec((QT, D_A), tok), pl.BlockSpec((QT, HKV * NSP), tok)],
        out_shape=[jax.ShapeDtypeStruct((n_tok, D_A), F32), jax.ShapeDtypeStruct((n_tok, HKV * NSP), F32)],
        compiler_params=_cparams("parallel", "parallel"),
    )(zq, ckv, bias_cmp, ov_t)


def _flash_prompt_kernel(*refs, window, n_delta, kt_len):
    if window:
        q_ref, kt_ref, vt_ref, bias_ref, o_ref, s_scr, p_scr, m_scr, l_scr, a_scr, acc_scr = refs
    else:
        q_ref, kt_ref, vt_ref, bias_ref, sel_ref, o_ref, s_scr, p_scr, m_scr, l_scr, a_scr, acc_scr = refs
    g = pl.program_id(0)
    qb = pl.program_id(2)
    q = q_ref[...] * (HD ** -0.5 * LOG2E)
    if window:
        parts = [q[:, r * HD:(r + 1) * HD] for r in range(REP)]
    else:
        sel = sel_ref[...]
        unpicked = (1.0 - jnp.where(g == 0, sel[:, :NSP], sel[:, NSP:])) * NEG
        parts = [jnp.concatenate([q[:, r * HD:(r + 1) * HD], unpicked], axis=1) for r in range(REP)]
    qrows = jnp.concatenate(parts, axis=0).astype(BF16)
    hi = (qb * QT) // kt_len + 1
    lo = jnp.maximum(qb * QT - (WINDOW - 1), 0) // kt_len if window else 0

    m_scr[...] = jnp.full(m_scr.shape, NEG, F32)
    l_scr[...] = jnp.zeros(l_scr.shape, F32)
    acc_scr[...] = jnp.zeros(acc_scr.shape, F32)

    def body(kt, carry):
        k0 = pl.multiple_of(kt * kt_len, kt_len)
        delta = jnp.minimum(qb - (kt_len // QT) * kt, n_delta - 1)
        k_t = kt_ref[0, :, pl.ds(k0, kt_len)]
        if not window:
            blk = lax.broadcasted_iota(jnp.int32, (NSP, kt_len), 0)
            key = lax.broadcasted_iota(jnp.int32, (NSP, kt_len), 1)
            one_hot = jnp.where(blk == (k0 + key) // SLC_LEN, 1.0, 0.0).astype(BF16)
            k_t = jnp.concatenate([k_t, one_hot], axis=0)
        s_scr[...] = jnp.dot(qrows, k_t, preferred_element_type=F32)
        for c in range(REP * QT // ROWS):
            r, qc = divmod(c, QT // ROWS)
            rows = pl.ds(c * ROWS, ROWS)
            s = s_scr[rows, :] + bias_ref[delta, r, pl.ds(qc * ROWS, ROWS), :]
            m_old = m_scr[rows, :]
            m_new = jnp.maximum(m_old, jnp.max(s, axis=1, keepdims=True))
            p = jnp.exp2(s - jnp.concatenate([m_new] * (kt_len // LANES), axis=1))
            alpha = jnp.exp2(m_old - m_new)
            l_scr[rows, :] = alpha * l_scr[rows, :] + jnp.sum(p, axis=1, keepdims=True)
            m_scr[rows, :] = m_new
            a_scr[rows, :] = alpha
            p_scr[rows, :] = p.astype(BF16)
        acc_scr[...] = a_scr[...] * acc_scr[...] + lax.dot_general(
            p_scr[...], vt_ref[0, :, pl.ds(k0, kt_len)], (((1,), (1,)), ((), ())), preferred_element_type=F32)
        return carry

    lax.fori_loop(lo, hi, body, 0)
    o = acc_scr[...] / l_scr[...]
    for r in range(REP):
        o_r = o[r * QT:(r + 1) * QT]
        o_ref[:, r * HD:(r + 1) * HD] = jnp.where(g == 0, o_r[:, :HD], o_r[:, HD:])


def _flash_prompt(zq, kv_t, kslot, vslot, bias, sel, n_seq, *, window):
    n_tok = zq.shape[0]
    t = kv_t.shape[2]
    nqb = t // QT
    kt_len = bias.shape[3]
    in_specs = [pl.BlockSpec((QT, REP * HD), lambda g, b, i: (b * nqb + i, g)),
                pl.BlockSpec((1, HD, t), lambda g, b, i: (b, HKV * kslot + g, 0)),
                pl.BlockSpec((1, LANES, t), lambda g, b, i: (b, vslot, 0)),
                pl.BlockSpec((bias.shape[0], REP, QT, kt_len), lambda g, b, i: (0, g, 0, 0))]
    args = [zq, kv_t, kv_t, bias]
    rows = REP * QT
    scratch = [pltpu.VMEM((rows, kt_len), F32), pltpu.VMEM((rows, kt_len), BF16), pltpu.VMEM((rows, LANES), F32),
               pltpu.VMEM((rows, LANES), F32), pltpu.VMEM((rows, LANES), F32), pltpu.VMEM((rows, LANES), F32)]
    if not window:
        in_specs.append(pl.BlockSpec((QT, HKV * NSP), lambda g, b, i: (b * nqb + i, 0)))
        args.append(sel)
    return pl.pallas_call(
        functools.partial(_flash_prompt_kernel, window=window, n_delta=bias.shape[0], kt_len=kt_len),
        grid=(HKV, n_seq, nqb),
        in_specs=in_specs,
        out_specs=pl.BlockSpec((QT, REP * HD), lambda g, b, i: (b * nqb + i, g)),
        out_shape=jax.ShapeDtypeStruct((n_tok, D_A), F32),
        scratch_shapes=scratch,
        compiler_params=_cparams("parallel", "parallel", "parallel"),
    )(*args)


def _nsa_sample_kernel(*refs, n_pages, nq, past, n_item):
    all_pages = refs[1:1 + n_item * n_pages]
    (q_ref, kvn_ref, wn_ref, wbuf_ref, ckv_ref, bcmp_ref, bslc_ref, bwin_ref, ov_ref, exp_ref,
     ocmp_ref, oslc_ref, owin_ref) = refs[1 + n_item * n_pages:14 + n_item * n_pages]
    scratch = refs[14 + n_item * n_pages:]
    for it in range(n_item):
        _nsa_sample_item(all_pages[it * n_pages:(it + 1) * n_pages], q_ref.at[it], kvn_ref.at[it], wn_ref.at[it],
                         wbuf_ref.at[it], ckv_ref.at[it], bcmp_ref, bslc_ref, bwin_ref, ov_ref, exp_ref,
                         ocmp_ref.at[it], oslc_ref.at[it], owin_ref.at[it], scratch[2 * it], scratch[2 * it + 1],
                         nq=nq, past=past)


def _nsa_sample_item(page_refs, q_ref, kvn_ref, wn_ref, wbuf_ref, ckv_ref, bcmp_ref, bslc_ref, bwin_ref, ov_ref,
                     exp_ref, ocmp_ref, oslc_ref, owin_ref, new_kv, new_w, *, nq, past):
    new_kv[nq:, :] = jnp.zeros((LANES - nq, 4 * LANES), F32)
    new_kv[:nq, :] = kvn_ref[...]
    new_w[nq:, :] = jnp.zeros((LANES - nq, 2 * LANES), F32)
    new_w[:nq, :] = wn_ref[...]

    q = q_ref[...] * (HD ** -0.5)
    ck, cv = ckv_ref[0], ckv_ref[1]
    qpos = past + lax.broadcasted_iota(jnp.int32, (nq, 1), 0)

    def attend(qpad, old_k, old_v, new_k, new_v, bias):
        s = jnp.concatenate([jnp.dot(qpad, k_t.astype(BF16), preferred_element_type=F32) for k_t in old_k]
                            + [_dot_nt(qpad, new_k)], axis=1) + bias
        e = jnp.exp(s - jnp.max(s, axis=1, keepdims=True))
        o, c0 = _dot(e[:, s.shape[1] - LANES:], new_v), 0
        for v_t in old_v:
            o = o + _dot_nt(e[:, c0:c0 + v_t.shape[1]], v_t)
            c0 += v_t.shape[1]
        return o / jnp.sum(e, axis=1, keepdims=True)

    picked = []
    n_blocks = -(-(past + nq) // SLC_LEN)
    for psum in _cmp_branch(q, ck, cv, bcmp_ref, ocmp_ref):
        sel = _select_blocks(_dot_f32(psum, ov_ref[...]), qpos, 1, n_blocks)
        picked += [jnp.dot(sel.astype(BF16), exp_ref[...], preferred_element_type=F32)] * REP
    qpad = jnp.concatenate([_padded_queries(q, g) for g in range(HKV)], axis=0)
    bslc = jnp.where(jnp.concatenate(picked, axis=0) > 0.5, bslc_ref[...].reshape(HQ * nq, -1), NEG)
    o = attend(qpad, [pr[0, 2 * LANES:3 * LANES, :] for pr in page_refs],
               [pr[0, 3 * LANES:4 * LANES, :] for pr in page_refs],
               new_kv[:, 2 * LANES:3 * LANES], new_kv[:, 3 * LANES:4 * LANES], bslc)
    _store_all_heads(oslc_ref, o, nq)
    o = attend(qpad, [wbuf_ref[:LANES, :]], [wbuf_ref[LANES:, :]], new_w[:, :LANES], new_w[:, LANES:],
               bwin_ref[...].reshape(HQ * nq, -1))
    _store_all_heads(owin_ref, o, nq)


def _nsa_sample(zq, kvn, wn, cache_t, page_base, page_table, win_t, win_base, ckv,
                bias_cmp, bias_slc, bias_win, ov, expand):
    n_seq, nq, _ = zq.shape
    n_pages = page_table.shape[1]
    page = cache_t.shape[2]
    past = n_pages * page
    win_buf = win_t.shape[2]
    ncp = ckv.shape[2]
    n_item = _pick_tile(n_seq, (2, 1))
    seq3 = lambda b, pt: (b, 0, 0)
    const = lambda nd: (lambda b, pt: (0,) * nd)
    page_spec = lambda it, p: pl.BlockSpec((1, 4 * LANES, page),
                                           lambda b, pt: (page_base + pt[b * n_item + it, p], 0, 0))
    in_specs = [page_spec(it, p) for it in range(n_item) for p in range(n_pages)] + [
        pl.BlockSpec((n_item, nq, D_A), seq3),
        pl.BlockSpec((n_item, nq, 4 * LANES), seq3),
        pl.BlockSpec((n_item, nq, 2 * LANES), seq3),
        pl.BlockSpec((n_item, 2 * LANES, win_buf), lambda b, pt: (win_base // n_item + b, 0, 0)),
        pl.BlockSpec((n_item, 2, ncp, LANES), lambda b, pt: (b, 0, 0, 0)),
        pl.BlockSpec(bias_cmp.shape, const(3)),
        pl.BlockSpec(bias_slc.shape, const(3)),
        pl.BlockSpec(bias_win.shape, const(3)),
        pl.BlockSpec(ov.shape, const(2)),
        pl.BlockSpec(expand.shape, const(2))]
    out = pl.BlockSpec((n_item, nq, D_A), seq3)
    return pl.pallas_call(
        functools.partial(_nsa_sample_kernel, n_pages=n_pages, nq=nq, past=past, n_item=n_item),
        grid_spec=pltpu.PrefetchScalarGridSpec(
            num_scalar_prefetch=1, grid=(n_seq // n_item,), in_specs=in_specs, out_specs=[out, out, out],
            scratch_shapes=[pltpu.VMEM((LANES, 4 * LANES), F32), pltpu.VMEM((LANES, 2 * LANES), F32)] * n_item),
        out_shape=[jax.ShapeDtypeStruct((n_seq, nq, D_A), F32)] * 3,
        compiler_params=_cparams("parallel"),
    )(page_table, *([cache_t] * (n_item * n_pages)), zq, kvn, wn, win_t, ckv, bias_cmp, bias_slc, bias_win, ov, expand)


def _merge_kernel(x_ref, mout_ref, ocmp_ref, oslc_ref, owin_ref, zg_ref, zgm_ref, gexp_ref, wb_ref, wo_ref,
                  lg_ref, lb_ref, o_ref, *, alpha):
    d = x_ref.shape[1]
    gate = jax.nn.sigmoid(zg_ref[...])
    a = None
    for br, src in enumerate((ocmp_ref, oslc_ref, owin_ref)):
        term = _dot_f32(gate, gexp_ref[br]) * src[...]
        a = term if a is None else a + term
    u = (jax.nn.sigmoid(zgm_ref[:, :d]) * _dot(mout_ref[...], wb_ref[0])
         + jax.nn.sigmoid(zgm_ref[:, d:]) * _dot(a, wb_ref[1]))
    y = _dot(u, wo_ref[...])
    o_ref[...] = _layer_norm(alpha * x_ref[...] + y, lg_ref[...], lb_ref[...])


def _merge(x, mout, ocmp, oslc, owin, zg, zgm, gexp, wb, wo, lg, lb, alpha):
    n_tok, d = x.shape
    tm = _pick_tile(n_tok, (256, 128, 64, 32, 16, 8))
    row = lambda i: (i, 0)
    c2 = lambda i: (0, 0)
    c3 = lambda i: (0, 0, 0)
    return pl.pallas_call(
        functools.partial(_merge_kernel, alpha=alpha),
        grid=(n_tok // tm,),
        in_specs=[pl.BlockSpec((tm, d), row), pl.BlockSpec((tm, D_M), row), pl.BlockSpec((tm, D_A), row),
                  pl.BlockSpec((tm, D_A), row), pl.BlockSpec((tm, D_A), row), pl.BlockSpec((tm, LANES), row),
                  pl.BlockSpec((tm, 2 * d), row), pl.BlockSpec(gexp.shape, c3), pl.BlockSpec(wb.shape, c3),
                  pl.BlockSpec(wo.shape, c2), pl.BlockSpec((1, d), c2), pl.BlockSpec((1, d), c2)],
        out_specs=pl.BlockSpec((tm, d), row),
        out_shape=jax.ShapeDtypeStruct((n_tok, d), F32),
        compiler_params=_cparams("parallel"),
    )(x, mout, ocmp, oslc, owin, zg, zgm, gexp, wb, wo, lg.reshape(1, d), lb.reshape(1, d))


def _route(aff, sel):
    def top2_sum(a, b, c, d):
        x, x2, y, y2 = jnp.maximum(a, b), jnp.minimum(a, b), jnp.maximum(c, d), jnp.minimum(c, d)
        return jnp.maximum(x, y) + jnp.maximum(jnp.minimum(x, y), jnp.maximum(x2, y2))

    gsum = [top2_sum(*sel[EXP_PER_GROUP * gi:EXP_PER_GROUP * (gi + 1)]) for gi in range(N_GROUPS)]
    gmax = functools.reduce(jnp.maximum, gsum)
    chosen, taken = [], None
    for gi in range(N_GROUPS):
        is_best = gsum[gi] == gmax if taken is None else (gsum[gi] == gmax) & jnp.logical_not(taken)
        taken = is_best if taken is None else taken | is_best
        members = range(EXP_PER_GROUP * gi, EXP_PER_GROUP * (gi + 1))
        for e in members:
            ahead = None
            for e2 in members:
                if e2 == e:
                    continue
                before = (sel[e2] >= sel[e]) if e2 < e else (sel[e2] > sel[e])
                cnt = jnp.where(before, 1.0, 0.0)
                ahead = cnt if ahead is None else ahead + cnt
            chosen.append(is_best & (ahead < TOP_K))
    picked = [jnp.where(c, a, 0.0) for c, a in zip(chosen, aff)]
    total = functools.reduce(lambda a, b: a + b, picked)
    return [p / total for p in picked]


def _moe_kernel(x_ref, wr_ref, rb_ref, wg_ref, wu_ref, wd_ref, lg_ref, lb_ref, o_ref, xb, comb, acc, *, alpha):
    step = pl.program_id(1)
    tm = x_ref.shape[0]
    lane = lax.broadcasted_iota(jnp.int32, (tm, LANES), 1)

    @pl.when(step == 0)
    def _():
        x16 = x_ref[...].astype(BF16)
        xb[...] = x16
        aff_t = jax.nn.sigmoid(_dot_nt(wr_ref[...], x16))
        sel_t = aff_t + rb_ref[...]
        weights = _route([aff_t[i:i + 1, :] for i in range(N_EXP)], [sel_t[i:i + 1, :] for i in range(N_EXP)])
        expert = lax.broadcasted_iota(jnp.int32, (N_EXP, tm), 0)
        comb_t = jnp.zeros((N_EXP, tm), F32)
        for i, w in enumerate(weights):
            comb_t = jnp.where(expert == i, w, comb_t)
        comb[...] = jnp.concatenate([comb_t, jnp.zeros((LANES - N_EXP, tm), F32)], axis=0).T
        acc[...] = jnp.zeros_like(acc)

    x16 = xb[...]
    hs = []
    for j in range(EXP_PER_STEP):
        w_e = jnp.sum(jnp.where(lane == EXP_PER_STEP * step + j, comb[...], 0.0), axis=1, keepdims=True)
        hg = jnp.dot(x16, wg_ref[0, j], preferred_element_type=F32)
        hu = jnp.dot(x16, wu_ref[0, j], preferred_element_type=F32)
        hs.append((hg * jax.nn.sigmoid(hg) * hu * w_e).astype(BF16))
    d = acc.shape[1]
    acc[...] += jnp.dot(jnp.concatenate(hs, axis=1), wd_ref[0].reshape(EXP_PER_STEP * D_EXP, d),
                        preferred_element_type=F32)

    @pl.when(step == pl.num_programs(1) - 1)
    def _():
        o_ref[...] = _layer_norm(alpha * x_ref[...] + acc[...], lg_ref[...], lb_ref[...])


def _moe(x, wr, rb, wg, wu, wd, layer, lg, lb, alpha):
    n_tok, d = x.shape
    tm = _pick_tile(n_tok, (1024, 512, 256, 128, 64, 32, 16, 8))
    row = lambda i, e: (i, 0)
    c2 = lambda i, e: (0, 0)
    wsel = lambda i, e: (layer, e, 0, 0)
    return pl.pallas_call(
        functools.partial(_moe_kernel, alpha=alpha),
        grid=(n_tok // tm, N_EXP // EXP_PER_STEP),
        in_specs=[pl.BlockSpec((tm, d), row), pl.BlockSpec(wr.shape, c2), pl.BlockSpec(rb.shape, c2),
                  pl.BlockSpec((1, EXP_PER_STEP, d, D_EXP), wsel), pl.BlockSpec((1, EXP_PER_STEP, d, D_EXP), wsel),
                  pl.BlockSpec((1, EXP_PER_STEP, D_EXP, d), wsel), pl.BlockSpec((1, d), c2), pl.BlockSpec((1, d), c2)],
        out_specs=pl.BlockSpec((tm, d), row),
        out_shape=jax.ShapeDtypeStruct((n_tok, d), F32),
        scratch_shapes=[pltpu.VMEM((tm, d), BF16), pltpu.VMEM((tm, LANES), F32), pltpu.VMEM((tm, d), F32)],
        compiler_params=_cparams("parallel", "arbitrary"),
    )(x, wr, rb, wg, wu, wd, lg.reshape(1, d), lb.reshape(1, d))


def _bucket_np(dist):
    n = np.maximum(dist, 0)
    exact = N_BUCKETS // 2
    nf = np.maximum(n, 1).astype(np.float32)
    large = exact + (np.log(nf / np.float32(exact)) / np.float32(math.log(REL_MAX_DIST / exact))
                     * np.float32(N_BUCKETS - exact)).astype(np.int32)
    return np.where(n < exact, n, np.minimum(large, N_BUCKETS - 1)).astype(np.int32)


def _bias_table(rel_bias, dist, valid):
    tab = rel_bias.astype(F32)[jnp.asarray(_bucket_np(dist))]
    tab = jnp.where(jnp.asarray(valid)[..., None], tab, NEG)
    return jnp.moveaxis(tab, -1, 0)


def _bias_tiles_kernel(rb_ref, o_ref, *, tile_step, key_stride, key_off, hi_valid, upper, lead):
    t = pl.program_id(0)
    shape = o_ref.shape[-2:]
    dist = (tile_step * t - key_off + lax.broadcasted_iota(jnp.int32, shape, 0)
            - key_stride * lax.broadcasted_iota(jnp.int32, shape, 1))
    valid = dist >= 0 if hi_valid is None else (dist >= 0) & (dist < hi_valid)
    acc = [jnp.full(shape, rb_ref[N_BUCKETS - 1, h], F32) for h in range(HQ)]
    for b in range(N_BUCKETS - 2, -1, -1):
        below = dist < upper[b]
        acc = [jnp.where(below, rb_ref[b, h], a) for h, a in enumerate(acc)]
    for h in range(HQ):
        if lead:
            o_ref[0, h] = jnp.where(valid, acc[h], NEG)
        else:
            o_ref[h] = jnp.where(valid, acc[h], NEG)


def _bias_tiles(rel_bias, n_tiles, n_keys, *, tile_step, key_stride, key_off, hi_valid, lead, scale=1.0):
    buckets = _bucket_np(np.arange(8 * REL_MAX_DIST))
    upper = tuple(int(np.searchsorted(buckets, b, side='right')) for b in range(N_BUCKETS - 1))
    if lead:
        out_spec = pl.BlockSpec((1, HQ, QT, n_keys), lambda t: (t, 0, 0, 0))
        out_shape = jax.ShapeDtypeStruct((n_tiles, HQ, QT, n_keys), F32)
    else:
        out_spec = pl.BlockSpec((HQ, QT, n_keys), lambda t: (0, t, 0))
        out_shape = jax.ShapeDtypeStruct((HQ, n_tiles * QT, n_keys), F32)
    return pl.pallas_call(
        functools.partial(_bias_tiles_kernel, tile_step=tile_step, key_stride=key_stride, key_off=key_off,
                          hi_valid=hi_valid, upper=upper, lead=lead),
        grid=(n_tiles,),
        in_specs=[pl.BlockSpec(memory_space=pltpu.SMEM)],
        out_specs=out_spec, out_shape=out_shape,
        compiler_params=_cparams("parallel"),
    )(rel_bias.astype(F32) * scale)


def _overlap_np(ncp, n_cmp, n_slc):
    c0 = np.arange(ncp)[:, None] * CMP_STRIDE
    j0 = np.arange(NSP)[None, :] * SLC_LEN
    ov = (c0 < j0 + SLC_LEN) & (c0 + CMP_LEN > j0)
    ov &= (np.arange(ncp)[:, None] < n_cmp) & (np.arange(NSP)[None, :] < n_slc)
    return ov.astype(np.float32)


def kernel(x_prompt, x_sample, cache_kv, state_win_kv, state_mlstm_C, state_mlstm_n, state_mlstm_m, page_table,
           w_in, b_in, w_cmp1, w_cmp2, mh_norm_g, w_branch, w_out, ln1_g, ln1_b, ln2_g, ln2_b,
           w_router, router_bias, w_gate_e, w_up_e, w_down_e, rel_bias):
    B, T, D = x_prompt.shape
    DB, TS, _ = x_sample.shape
    depth, n_pool, page = cache_kv.shape[:3]
    n_pages = page_table.shape[1]
    past = n_pages * page
    win_buf = state_win_kv.shape[2]
    alpha = (2 * depth) ** 0.25
    assert T % KT == 0 and T // SLC_LEN <= NSP and page % CMP_STRIDE == 0 and past % SLC_LEN == 0
    assert win_buf == min(WINDOW, past) and T >= win_buf

    off = np.cumsum((0, D_M, D_M, D_M, D_M, H_M, H_M, D_A, 6 * HKV * HD, 3 * HQ, 2 * D)).tolist()
    seg = lambda a, i, j: a[..., off[i]:off[j]]
    small = lambda a: jnp.concatenate(
        [seg(a, 4, 6), seg(a, 8, 9), jnp.zeros(a.shape[:-1] + (LANES - 2 * H_M - 3 * HQ,), a.dtype)], axis=-1)
    kv_cmp, kv_mid = off[7] + 2 * HKV * HD, off[7] + 4 * HKV * HD
    groups_s = lambda a: (seg(a, 0, 4), small(a), seg(a, 6, 7), a[..., off[7]:kv_mid], a[..., kv_mid:off[8]], seg(a, 9, 10))
    groups_p = lambda a: (seg(a, 0, 4), small(a), seg(a, 6, 7), a[..., off[7]:kv_cmp], seg(a, 9, 10))
    ws_s = [w.astype(BF16) for w in groups_s(w_in)]
    bs_s = [b[:, None, :] for b in groups_s(b_in)]
    ws_p = [w.astype(BF16) for w in groups_p(w_in)]
    bs_p = [b[:, None, :] for b in groups_p(b_in)]
    w_in_t = jnp.transpose(w_in, (0, 2, 1))
    wt_p = w_in_t[:, off[7]:off[8]].astype(BF16)
    bt_p = b_in[:, off[7]:off[8], None]
    wb16, wo16 = w_branch.astype(BF16), w_out.astype(BF16)
    wg16, wu16, wd16 = w_gate_e.astype(BF16), w_up_e.astype(BF16), w_down_e.astype(BF16)
    wr_pad = jnp.transpose(w_router).astype(BF16)
    rb_pad = router_bias.astype(F32).reshape(N_EXP, 1)
    eye = jnp.eye(HKV, dtype=F32)
    w1 = w_cmp1.reshape(depth, 2, 2, CMP_STRIDE, HD, CMP_HID)
    w1p = jnp.einsum('lshrdf,gG->lsrgdhGf', w1, eye).reshape(depth, 2, CMP_STRIDE, HKV * HD, 2 * HKV * CMP_HID)
    w1p = w1p.astype(BF16)
    w2p = jnp.einsum('lsfd,gG->lsgfGd', w_cmp2, eye).reshape(depth, 2, HKV * CMP_HID, HKV * HD).astype(BF16)
    gexp = np.zeros((3, LANES, D_A), np.float32)
    for br in range(3):
        for h in range(HQ):
            gexp[br, 2 * H_M + br * HQ + h, h * HD:(h + 1) * HD] = 1.0
    gexp = jnp.asarray(gexp)

    ncp_p, n_cmp_p = T // CMP_STRIDE, (T - CMP_LEN) // CMP_STRIDE + 1
    assert n_cmp_p * CMP_STRIDE + CMP_LEN - 1 > T - 1
    bias_cmp_p = _bias_tiles(rel_bias, T // QT, ncp_p, tile_step=QT, key_stride=CMP_STRIDE, key_off=CMP_LEN - 1,
                             hi_valid=None, lead=False)
    ov_t_p = jnp.asarray(_overlap_np(ncp_p, n_cmp_p, -(-T // SLC_LEN)).T)
    n_far = -(-(KT - 1 + REL_MAX_DIST) // QT)
    bias_slc_p = _bias_tiles(rel_bias, n_far + 1, KT, tile_step=QT, key_stride=1, key_off=0, hi_valid=None, lead=True,
                             scale=LOG2E)
    bias_win_p = _bias_tiles(rel_bias, (WINDOW + KT_WIN) // QT, KT_WIN, tile_step=QT, key_stride=1, key_off=0,
                             hi_valid=WINDOW, lead=True, scale=LOG2E)

    ncp_s = past // CMP_STRIDE
    n_cmp_s = (past + TS - CMP_LEN) // CMP_STRIDE + 1
    assert n_cmp_s <= ncp_s and -(-(past + TS) // SLC_LEN) <= NSP and TS <= NQS
    qs = past + np.arange(NQS)[:, None]
    d_cmp_s = qs - (np.arange(ncp_s)[None, :] * CMP_STRIDE + CMP_LEN - 1)
    bias_cmp_s = _bias_table(rel_bias, d_cmp_s, (d_cmp_s >= 0) & (np.arange(ncp_s)[None, :] < n_cmp_s))
    ov_s = jnp.asarray(_overlap_np(ncp_s, n_cmp_s, -(-(past + TS) // SLC_LEN)))
    key_s = np.arange(past + LANES)[None, :]
    d_slc_s = qs - key_s
    bias_slc_s = _bias_table(rel_bias, d_slc_s, (d_slc_s >= 0) & (key_s < past + TS))
    expand_s = jnp.asarray((np.arange(NSP)[:, None] == key_s // SLC_LEN).astype(np.float32)).astype(BF16)
    idx_w = np.arange(win_buf + LANES)[None, :]
    d_win_s = qs - (past - win_buf + idx_w)
    bias_win_s = _bias_table(rel_bias, d_win_s, (d_win_s >= 0) & (d_win_s < WINDOW) & (idx_w < win_buf + TS))

    to_feature_major = lambda a: jnp.transpose(a, (0, 1, 3, 4, 5, 2))
    cache_t = to_feature_major(cache_kv).reshape(depth * n_pool, 4 * LANES, page)
    win_t = to_feature_major(state_win_kv).reshape(depth * DB, 2 * LANES, win_buf)
    from_feature_major = lambda a, slots: jnp.transpose(
        a.reshape(a.shape[:2] + (slots, HKV, HD, a.shape[-1])), (0, 1, 5, 2, 3, 4))
    half_per_page = page // CMP_STRIDE

    LP = _pick_tile(T, (256, 128, 64))
    LS = NQS
    zeros_state = (jnp.zeros((B, H_M, DH_M, DH_M), F32), jnp.zeros((B, H_M, DH_M), F32), jnp.zeros((B, H_M), F32))
    state_c_all = state_mlstm_C.reshape(depth * DB, H_M, DH_M, DH_M)

    xp = x_prompt.reshape(B * T, D)
    xs = x_sample.reshape(DB * TS, D)
    outs = [[] for _ in range(10)]
    for l in range(depth):
        zm, zg, zq, zc, zgm, zkv_t, zw_t, z16_t = _inproj(xp, [w[l] for w in ws_p], [b[l] for b in bs_p],
                                                          wt_p[l], bt_p[l], t_split=4 * LANES, n_seq=B)
        mout, c_p, n_p, m_p = _mlstm(zm, zg, mh_norm_g[l], *zeros_state, n_seq=B, L=LP, l_valid=LP)
        part = _cmp_partial(zc.reshape(B * T // CMP_STRIDE, CMP_STRIDE * 2 * LANES), w1p[l])
        ckv = _cmp_finish_prompt(part, w2p[l], B)
        ocmp, sel = _nsa_cmp_prompt(zq, ckv, bias_cmp_p, ov_t_p, B)
        oslc = _flash_prompt(zq, z16_t, 2, 3, bias_slc_p, sel, B, window=False)
        owin = _flash_prompt(zq, z16_t, 4, 5, bias_win_p, None, B, window=True)
        x1 = _merge(xp, mout, ocmp, oslc, owin, zg, zgm, gexp, wb16[l], wo16[l], ln1_g[l], ln1_b[l], alpha)
        xp = _moe(x1, wr_pad, rb_pad, wg16, wu16, wd16, l, ln2_g[l], ln2_b[l], alpha)
        outs[0].append(zkv_t)
        outs[1].append(zw_t[:, :, T - win_buf:])
        outs[2].append(c_p)
        outs[3].append(n_p)
        outs[4].append(m_p)
        zm, zg, zq, zkv, zw, zgm = _inproj(xs, [w[l] for w in ws_s], [b[l] for b in bs_s])
        padt = lambda a: jnp.pad(a.reshape(DB, TS, -1), ((0, 0), (0, LS - TS), (0, 0))).reshape(DB * LS, -1)
        mout, c_s, n_s, m_s = _mlstm(padt(zm), padt(zg), mh_norm_g[l], state_c_all, state_mlstm_n[l],
                                     state_mlstm_m[l], n_seq=DB, L=LS, l_valid=TS, c_base=l * DB)
        mout = mout.reshape(DB, LS, D_M)[:, :TS].reshape(DB * TS, D_M)
        part = _cmp_partial_paged(cache_t, w1p[l].reshape(2, CMP_STRIDE // 2, 2 * LANES, -1), l, n_pool)
        ckv = _cmp_finish_paged(part, w2p[l], page_table, 0, half_per_page)
        padq = lambda a: jnp.pad(a.reshape(DB, TS, -1), ((0, 0), (0, NQS - TS), (0, 0)))
        o3 = _nsa_sample(padq(zq), padq(zkv), padq(zw), cache_t, l * n_pool, page_table, win_t, l * DB, ckv,
                         bias_cmp_s, bias_slc_s, bias_win_s, ov_s, expand_s)
        ocmp, oslc, owin = [o[:, :TS].reshape(DB * TS, D_A) for o in o3]
        x1 = _merge(xs, mout, ocmp, oslc, owin, zg, zgm, gexp, wb16[l], wo16[l], ln1_g[l], ln1_b[l], alpha)
        xs = _moe(x1, wr_pad, rb_pad, wg16, wu16, wd16, l, ln2_g[l], ln2_b[l], alpha)
        zw5 = zw.reshape(DB, TS, 2, HKV, HD)
        outs[5].append(zkv.reshape(DB, TS, 4, HKV, HD))
        outs[6].append(jnp.concatenate([state_win_kv[l], zw5], axis=1)[:, TS:])
        outs[7].append(c_s)
        outs[8].append(n_s)
        outs[9].append(m_s)
    stacked = [jnp.stack(o) for o in outs]
    stacked[0] = from_feature_major(stacked[0], 4)
    stacked[1] = from_feature_major(stacked[1], 2)
    return (xp.reshape(B, T, D), xs.reshape(DB, TS, D)) + tuple(stacked)
```

```python
import functools
import math

import numpy as np
import jax
import jax.numpy as jnp
from jax import lax
from jax.experimental import pallas as pl
from jax.experimental.pallas import tpu as pltpu

F32 = jnp.float32
BF16 = jnp.bfloat16
HIGHEST = lax.Precision.HIGHEST

H_M, DH_M = 4, 128
D_M = H_M * DH_M
HQ, HKV, HD = 8, 2, 64
REP = HQ // HKV
D_A = HQ * HD
CMP_LEN, CMP_STRIDE, CMP_HID = 32, 16, 256
SLC_LEN, N_SEL, WINDOW = 64, 16, 512
N_BUCKETS, REL_MAX_DIST = 32, 128
N_EXP, N_GROUPS, TOP_K, D_EXP = 16, 4, 2, 256
EXP_PER_GROUP = N_EXP // N_GROUPS
EXP_PER_STEP = 2
LN_EPS = 1e-5
NEG = -1e30

LANES = 128
QT = 128
KT = 512
KT_WIN = 512
LOG2E = math.log2(math.e)
ROWS = 32
NSP = 64
NQS = 8
VMEM_LIMIT = 56 * 1024 * 1024


def _cparams(*sem):
    return pltpu.CompilerParams(dimension_semantics=sem, vmem_limit_bytes=VMEM_LIMIT)


def _dot(a, b):
    return jnp.dot(a.astype(BF16), b.astype(BF16), preferred_element_type=F32)


def _dot_nt(a, b):
    return lax.dot_general(a.astype(BF16), b.astype(BF16), (((1,), (1,)), ((), ())), preferred_element_type=F32)


def _dot_tn(a, b):
    return lax.dot_general(a.astype(BF16), b.astype(BF16), (((0,), (0,)), ((), ())), preferred_element_type=F32)


def _dot_f32(a, b):
    return jnp.dot(a, b, precision=HIGHEST, preferred_element_type=F32)


def _dot_nt_f32(a, b):
    return lax.dot_general(a, b, (((1,), (1,)), ((), ())), precision=HIGHEST, preferred_element_type=F32)


def _pick_tile(n, cands):
    for c in cands:
        if n % c == 0:
            return c
    raise ValueError(f"no tile for {n}")


def _layer_norm(y, g, b):
    mu = jnp.mean(y, axis=-1, keepdims=True)
    yc = y - mu
    var = jnp.mean(yc * yc, axis=-1, keepdims=True)
    return yc * lax.rsqrt(var + LN_EPS) * g + b


def _inproj_kernel(x_ref, *refs, n_plain, t_split, n_carried):
    n_w = n_plain + (1 if t_split else 0)
    w_refs, b_refs, o_refs = refs[:n_plain], refs[n_w:n_w + n_plain], refs[2 * n_w + n_carried:]
    x = x_ref[...].astype(BF16)
    for w_ref, b_ref, o_ref in zip(w_refs, b_refs, o_refs):
        o_ref[...] = jnp.dot(x, w_ref[...], preferred_element_type=F32) + b_ref[...]
    if t_split:
        wt_ref, bt_ref = refs[n_plain], refs[n_w + n_plain]
        zt = lax.dot_general(wt_ref[...], x, (((1,), (1,)), ((), ())), preferred_element_type=F32) + bt_ref[...]
        lo_ref, hi_ref, all16_ref = o_refs[n_plain:]
        lo_ref[0] = zt[:t_split]
        hi_ref[0] = zt[t_split:]
        all16_ref[0] = zt.astype(BF16)


def _carry_args(carried):
    if carried is None:
        return [], [], 0
    return [carried], [pl.BlockSpec(memory_space=pl.ANY)], 1


def _inproj(x, ws, bs, wt=None, bt=None, t_split=0, n_seq=1, layer=0, depth=1, carried=None):
    n_tok, d = x.shape
    t = n_tok // n_seq
    tm = _pick_tile(t, (256, 128, 64, 32, 16, 8))
    nt = t // tm
    full = lambda i: (0, 0)
    row = lambda i: (i, 0)
    tspec = lambda n, base: pl.BlockSpec((1, n, tm), lambda i: (base + i // nt, 0, i % nt))
    extra_w = [] if wt is None else [wt]
    extra_b = [] if wt is None else [bt]
    n_t = 0 if wt is None else wt.shape[0]
    t_specs = [] if wt is None else [tspec(t_split, layer * n_seq), tspec(n_t - t_split, 0), tspec(n_t, 0)]
    t_shapes = [] if wt is None else [jax.ShapeDtypeStruct((depth * n_seq, t_split, t), F32),
                                      jax.ShapeDtypeStruct((n_seq, n_t - t_split, t), F32),
                                      jax.ShapeDtypeStruct((n_seq, n_t, t), BF16)]
    c_args, c_specs, n_carried = _carry_args(carried)
    n_in = 1 + 2 * (len(ws) + len(extra_w))
    return pl.pallas_call(
        functools.partial(_inproj_kernel, n_plain=len(ws), t_split=t_split if wt is not None else 0,
                          n_carried=n_carried),
        grid=(n_tok // tm,),
        in_specs=[pl.BlockSpec((tm, d), row)]
        + [pl.BlockSpec(w.shape, full) for w in (*ws, *extra_w)]
        + [pl.BlockSpec(b.shape, full) for b in (*bs, *extra_b)] + c_specs,
        out_specs=[pl.BlockSpec((tm, w.shape[1]), row) for w in ws] + t_specs,
        out_shape=[jax.ShapeDtypeStruct((n_tok, w.shape[1]), F32) for w in ws] + t_shapes,
        input_output_aliases={n_in: len(ws)} if n_carried else {},
        compiler_params=_cparams("parallel"),
    )(x, *ws, *extra_w, *bs, *extra_b, *c_args)


def _log_sigmoid(x):
    return jnp.minimum(x, 0.0) - jnp.log(1.0 + jnp.exp(-jnp.abs(x)))


def _mlstm_kernel(q_ref, k_ref, v_ref, o_ref, g_ref, ng_ref, c0_ref, n0_ref, m0_ref, *rest, L, l_valid):
    mout_ref, c_ref, n_ref, m_ref, cs, ns, ms = rest[-7:]
    c = pl.program_id(1)

    @pl.when(c == 0)
    def _():
        cs[...] = c0_ref[0]
        ns[...] = n0_ref[0]
        ms[...] = m0_ref[0]

    g = g_ref[...]
    row = lax.broadcasted_iota(jnp.int32, (L, L), 0)
    col = lax.broadcasted_iota(jnp.int32, (L, L), 1)
    causal = row >= col
    fcum = _dot_f32(causal.astype(F32), _log_sigmoid(g))
    lane = lax.broadcasted_iota(jnp.int32, (L, LANES), 1)
    y = jnp.where(lane < H_M, g - pltpu.roll(fcum, LANES - H_M, 1), fcum)
    yt = y.T
    rowv = lax.broadcasted_iota(jnp.int32, (L, 1), 0)
    r = l_valid - 1
    for h in range(H_M):
        hs = slice(h * DH_M, (h + 1) * DH_M)
        f_col = fcum[:, H_M + h:H_M + h + 1]
        a_row = yt[h:h + 1, :]
        i_col = g[:, h:h + 1]
        m0 = ms[h][:, 0:1]
        n0 = ns[h]
        c0 = cs[h]
        dm = jnp.where(causal, f_col + a_row, NEG)
        b = f_col + m0
        mrow = jnp.maximum(b, jnp.max(dm, axis=1, keepdims=True))
        w = jnp.exp(dm - mrow)
        dec = jnp.exp(b - mrow)
        q = q_ref[:, hs]
        k = k_ref[:, hs] * (DH_M ** -0.5)
        v = v_ref[:, hs]
        s = _dot_nt(q, k) * w
        num = dec * _dot_nt(q, c0) + _dot(s, v)
        den = dec * jnp.sum(q * n0, axis=1, keepdims=True) + jnp.sum(s, axis=1, keepdims=True)
        hh = num / jnp.maximum(jnp.abs(den), jnp.exp(-mrow))
        mu = jnp.mean(hh, axis=1, keepdims=True)
        hc = hh - mu
        var = jnp.mean(hc * hc, axis=1, keepdims=True)
        hn = hc * lax.rsqrt(var + LN_EPS) * ng_ref[:, hs]
        mout_ref[:, hs] = (hn * jax.nn.sigmoid(o_ref[:, hs])).astype(mout_ref.dtype)
        f_r = fcum[r:r + 1, H_M + h:H_M + h + 1]
        m_r = mrow[r:r + 1, :]
        w_last = jnp.where(rowv <= r, jnp.exp(f_r - f_col + i_col - m_r), 0.0)
        d_last = dec[r:r + 1, :]
        cs[h] = d_last * c0 + _dot_tn(v * w_last, k)
        ns[h] = d_last * n0 + jnp.sum(k * w_last, axis=0, keepdims=True)
        ms[h] = jnp.broadcast_to(m_r, (1, LANES))

    @pl.when(c == pl.num_programs(1) - 1)
    def _():
        c_ref[0] = cs[...]
        n_ref[0] = ns[...]
        m_ref[0] = ms[...]


def _mlstm(zm, zg, norm_g, c0, n0, m0, *, n_seq, L, l_valid, c_base=0, c_layers=1, carried=None):
    n_tok = zm.shape[0]
    t = n_tok // n_seq
    nc = t // L
    n0 = n0.reshape(n_seq, H_M, 1, DH_M)
    m0 = jnp.broadcast_to(m0.reshape(n_seq, H_M, 1, 1), (n_seq, H_M, 1, LANES))
    colblk = lambda j: (lambda b, c: (b * nc + c, j))
    st4 = lambda b, c: (b, 0, 0, 0)
    c_spec = pl.BlockSpec((1, H_M, DH_M, DH_M), lambda b, c: (c_base + b, 0, 0, 0))
    c_args, c_specs, n_carried = _carry_args(carried)
    mout, c_new, n_new, m_new = pl.pallas_call(
        functools.partial(_mlstm_kernel, L=L, l_valid=l_valid),
        grid=(n_seq, nc),
        in_specs=[pl.BlockSpec((L, D_M), colblk(0)), pl.BlockSpec((L, D_M), colblk(1)),
                  pl.BlockSpec((L, D_M), colblk(2)), pl.BlockSpec((L, D_M), colblk(3)),
                  pl.BlockSpec((L, LANES), colblk(0)),
                  pl.BlockSpec((1, D_M), lambda b, c: (0, 0)),
                  c_spec,
                  pl.BlockSpec((1, H_M, 1, DH_M), st4),
                  pl.BlockSpec((1, H_M, 1, LANES), st4)] + c_specs,
        out_specs=[pl.BlockSpec((L, D_M), colblk(0)),
                   c_spec,
                   pl.BlockSpec((1, H_M, 1, DH_M), st4),
                   pl.BlockSpec((1, H_M, 1, LANES), st4)],
        out_shape=[jax.ShapeDtypeStruct((n_tok, D_M), BF16),
                   jax.ShapeDtypeStruct((c_layers * n_seq, H_M, DH_M, DH_M), F32),
                   jax.ShapeDtypeStruct((n_seq, H_M, 1, DH_M), F32),
                   jax.ShapeDtypeStruct((n_seq, H_M, 1, LANES), F32)],
        scratch_shapes=[pltpu.VMEM((H_M, DH_M, DH_M), F32), pltpu.VMEM((H_M, 1, DH_M), F32),
                        pltpu.VMEM((H_M, 1, LANES), F32)],
        input_output_aliases={9: 1} if n_carried else {},
        compiler_params=_cparams("parallel", "arbitrary"),
    )(zm, zm, zm, zm, zg, norm_g.reshape(1, D_M), c0, n0, m0, *c_args)
    return mout, c_new, n_new.reshape(n_seq, H_M, DH_M), m_new[:, :, 0, 0]


def _cmp_partial_kernel(x_ref, w_ref, o_ref):
    @pl.when(pl.program_id(2) == 0)
    def _():
        o_ref[...] = jnp.zeros_like(o_ref)

    o_ref[0] += _dot(x_ref[...], w_ref[0, 0])


def _cmp_partial(rows16, w1p):
    n_half = rows16.shape[0]
    tm = _pick_tile(n_half, (2048, 1024, 512, 256, 128, 64, 32, 16, 8))
    hid2 = 2 * HKV * CMP_HID
    return pl.pallas_call(
        _cmp_partial_kernel,
        grid=(2, n_half // tm, CMP_STRIDE),
        in_specs=[pl.BlockSpec((tm, LANES), lambda s, i, r: (i, 2 * r + s)),
                  pl.BlockSpec((1, 1, LANES, hid2), lambda s, i, r: (s, r, 0, 0))],
        out_specs=pl.BlockSpec((1, tm, hid2), lambda s, i, r: (s, i, 0)),
        out_shape=jax.ShapeDtypeStruct((2, n_half, hid2), F32),
        compiler_params=_cparams("parallel", "parallel", "arbitrary"),
    )(rows16, w1p)


def _cmp_partial_paged_kernel(x_ref, perm_ref, w_ref, o_ref, tok, *, pages, page):
    half_per_page = page // CMP_STRIDE
    for s in range(2):
        for p in range(pages):
            xt = _dot_nt(perm_ref[...], x_ref[p, s * LANES:(s + 1) * LANES, :])
            tok[:, p * half_per_page:(p + 1) * half_per_page, :] = xt.reshape(CMP_STRIDE, half_per_page, LANES)
        acc = None
        for r in range(0, CMP_STRIDE, 2):
            part = _dot(jnp.concatenate([tok[r], tok[r + 1]], axis=1), w_ref[s, r // 2])
            acc = part if acc is None else acc + part
        o_ref[s] = acc


def _cmp_partial_paged(cache_t, w1p, layer, n_pool):
    page = cache_t.shape[2]
    pages = _pick_tile(n_pool, (64, 32, 16, 8, 4, 2, 1))
    hid2 = 2 * HKV * CMP_HID
    n_half = n_pool * page // CMP_STRIDE
    half_per_page = page // CMP_STRIDE
    rows = pages * half_per_page
    base = layer * (n_pool // pages)
    tok = np.arange(page)
    perm = np.zeros((page, page), np.float32)
    perm[(tok % CMP_STRIDE) * half_per_page + tok // CMP_STRIDE, tok] = 1.0
    return pl.pallas_call(
        functools.partial(_cmp_partial_paged_kernel, pages=pages, page=page),
        grid=(n_pool // pages,),
        in_specs=[pl.BlockSpec((pages, 2 * LANES, page), lambda i: (base + i, 0, 0)),
                  pl.BlockSpec((page, page), lambda i: (0, 0)),
                  pl.BlockSpec(w1p.shape, lambda i: (0, 0, 0, 0))],
        out_specs=pl.BlockSpec((2, rows, hid2), lambda i: (0, i, 0)),
        out_shape=jax.ShapeDtypeStruct((2, n_half, hid2), F32),
        scratch_shapes=[pltpu.VMEM((CMP_STRIDE, rows, LANES), F32)],
        compiler_params=_cparams("parallel"),
    )(cache_t, jnp.asarray(perm, dtype=BF16), w1p)


def _gelu_tanh(x):
    return 0.5 * x * (1.0 + jnp.tanh(math.sqrt(2.0 / math.pi) * (x + 0.044715 * (x * x * x))))


def _cmp_finish_kernel(*refs, n_in, n_item):
    p_refs = refs[-(n_item * n_in + 2):-2]
    w2_ref, o_ref = refs[-2], refs[-1]
    half = HKV * CMP_HID
    for s in range(2):
        hids = []
        for it in range(n_item):
            pieces = p_refs[it * n_in:(it + 1) * n_in]
            p = jnp.concatenate([pr[s] for pr in pieces], axis=0) if n_in > 1 else pieces[0][s]
            n = p.shape[0]
            first, second = p[:, :half], p[:, half:]
            hids.append(_gelu_tanh(first + pltpu.roll(second, n - 1, 0)))
        out = _dot(jnp.concatenate(hids, axis=0) if n_item > 1 else hids[0], w2_ref[s])
        for it in range(n_item):
            o_ref[it, s] = out[it * n:(it + 1) * n]


def _cmp_finish_prompt(part, w2p, n_seq):
    n_half = part.shape[1] // n_seq
    hid2 = part.shape[2]
    return pl.pallas_call(
        functools.partial(_cmp_finish_kernel, n_in=1, n_item=1),
        grid=(n_seq,),
        in_specs=[pl.BlockSpec((2, n_half, hid2), lambda b: (0, b, 0)),
                  pl.BlockSpec(w2p.shape, lambda b: (0, 0, 0))],
        out_specs=pl.BlockSpec((1, 2, n_half, LANES), lambda b: (b, 0, 0, 0)),
        out_shape=jax.ShapeDtypeStruct((n_seq, 2, n_half, LANES), F32),
        compiler_params=_cparams("parallel"),
    )(part, w2p)


def _cmp_finish_paged(part, w2p, page_table, page_base, half_per_page):
    n_seq, n_pages = page_table.shape
    hid2 = part.shape[2]
    n_half = n_pages * half_per_page
    n_item = _pick_tile(n_seq, (4, 2, 1))
    page_spec = lambda it, p: pl.BlockSpec((2, half_per_page, hid2),
                                           lambda b, pt: (0, page_base + pt[b * n_item + it, p], 0))
    return pl.pallas_call(
        functools.partial(_cmp_finish_kernel, n_in=n_pages, n_item=n_item),
        grid_spec=pltpu.PrefetchScalarGridSpec(
            num_scalar_prefetch=1, grid=(n_seq // n_item,),
            in_specs=[page_spec(it, p) for it in range(n_item) for p in range(n_pages)]
            + [pl.BlockSpec(w2p.shape, lambda b, pt: (0, 0, 0))],
            out_specs=pl.BlockSpec((n_item, 2, n_half, LANES), lambda b, pt: (b, 0, 0, 0))),
        out_shape=jax.ShapeDtypeStruct((n_seq, 2, n_half, LANES), F32),
        compiler_params=_cparams("parallel"),
    )(page_table, *([part] * (n_item * n_pages)), w2p)


def _cmp_branch(q, ck, cv, bias_ref, ocmp_ref):
    nq = q.shape[0]
    qpad = jnp.concatenate([_padded_queries(q, g) for g in range(HKV)], axis=0)
    bias = bias_ref[...].reshape(HQ * nq, -1)
    s = _dot_nt(qpad, ck) + bias
    e = jnp.exp(s - jnp.max(s, axis=1, keepdims=True))
    p = jnp.where(bias > 0.5 * NEG, e / jnp.sum(e, axis=1, keepdims=True), 0.0)
    _store_all_heads(ocmp_ref, _dot(p, cv), nq)
    psums = []
    for g in range(HKV):
        heads = [p[(g * REP + r) * nq:(g * REP + r + 1) * nq] for r in range(REP)]
        psums.append(functools.reduce(lambda a, b: a + b, heads))
    return psums


def _select_blocks(imp, qpos, blk_axis, n_blocks):
    blk = lax.broadcasted_iota(jnp.int32, imp.shape, blk_axis)
    cur = (qpos // SLC_LEN) == blk
    avail = blk * SLC_LEN <= qpos
    imp = jnp.where(cur, -NEG, jnp.where(avail, imp, NEG))
    cnt = jnp.zeros(imp.shape, F32)
    for j in range(n_blocks):
        other = lax.slice_in_dim(imp, j, j + 1, axis=blk_axis)
        cnt = cnt + jnp.where(blk > j, jnp.where(other >= imp, 1.0, 0.0), jnp.where(other > imp, 1.0, 0.0))
    return jnp.where((cnt < N_SEL) & (imp > 0.5 * NEG), 1.0, 0.0)


def _padded_queries(q, g, dtype=BF16):
    nq = q.shape[0]
    zero = jnp.zeros((nq, HD), F32)
    parts = []
    for r in range(REP):
        h = g * REP + r
        piece = q[:, h * HD:(h + 1) * HD]
        parts.append(jnp.concatenate([piece, zero] if g == 0 else [zero, piece], axis=1))
    return jnp.concatenate(parts, axis=0).astype(dtype)


def _store_all_heads(o_ref, o, nq):
    for h in range(HQ):
        g = h // REP
        o_ref[:, h * HD:(h + 1) * HD] = o[h * nq:(h + 1) * nq, g * HD:(g + 1) * HD]


def _nsa_cmp_prompt_kernel(q_ref, ckv_ref, bias_ref, ovt_ref, ocmp_ref, sel_ref):
    qb = pl.program_id(1)
    q = q_ref[...] * (HD ** -0.5)
    ck, cv = ckv_ref[0, 0], ckv_ref[0, 1]
    qpos = qb * QT + lax.broadcasted_iota(jnp.int32, (1, QT), 1)
    for g, psum in enumerate(_cmp_branch(q, ck, cv, bias_ref, ocmp_ref)):
        imp_t = _dot_nt_f32(ovt_ref[...], psum)
        sel_ref[:, g * NSP:(g + 1) * NSP] = _select_blocks(imp_t, qpos, 0, NSP).T


def _nsa_cmp_prompt(zq, ckv, bias_cmp, ov_t, n_seq):
    n_tok = zq.shape[0]
    nqb = n_tok // n_seq // QT
    ncp = ckv.shape[2]
    tok = lambda b, i: (b * nqb + i, 0)
    return pl.pallas_call(
        _nsa_cmp_prompt_kernel,
        grid=(n_seq, nqb),
        in_specs=[pl.BlockSpec((QT, D_A), tok),
                  pl.BlockSpec((1, 2, ncp, LANES), lambda b, i: (b, 0, 0, 0)),
                  pl.BlockSpec((HQ, QT, ncp), lambda b, i: (0, i, 0)),
                  pl.BlockSpec(ov_t.shape, lambda b, i: (0, 0))],
        out_specs=[pl.BlockSpec((QT, D_A), tok), pl.BlockSpec((QT, HKV * NSP), tok)],
        out_shape=[jax.ShapeDtypeStruct((n_tok, D_A), F32), jax.ShapeDtypeStruct((n_tok, HKV * NSP), F32)],
        compiler_params=_cparams("parallel", "parallel"),
    )(zq, ckv, bias_cmp, ov_t)


def _flash_prompt_kernel(*refs, window, n_delta, kt_len):
    if window:
        q_ref, kt_ref, vt_ref, bias_ref, o_ref, s_scr, p_scr, m_scr, l_scr, a_scr, acc_scr = refs
    else:
        q_ref, kt_ref, vt_ref, bias_ref, sel_ref, o_ref, s_scr, p_scr, m_scr, l_scr, a_scr, acc_scr = refs
    g = pl.program_id(0)
    qb = pl.program_id(2)
    q = q_ref[...] * (HD ** -0.5 * LOG2E)
    if window:
        parts = [q[:, r * HD:(r + 1) * HD] for r in range(REP)]
    else:
        sel = sel_ref[...]
        unpicked = (1.0 - jnp.where(g == 0, sel[:, :NSP], sel[:, NSP:])) * NEG
        parts = [jnp.concatenate([q[:, r * HD:(r + 1) * HD], unpicked], axis=1) for r in range(REP)]
    qrows = jnp.concatenate(parts, axis=0).astype(BF16)
    hi = (qb * QT) // kt_len + 1
    lo = jnp.maximum(qb * QT - (WINDOW - 1), 0) // kt_len if window else 0

    m_scr[...] = jnp.full(m_scr.shape, NEG, F32)
    l_scr[...] = jnp.zeros(l_scr.shape, F32)
    acc_scr[...] = jnp.zeros(acc_scr.shape, F32)

    def body(kt, carry):
        k0 = pl.multiple_of(kt * kt_len, kt_len)
        delta = jnp.minimum(qb - (kt_len // QT) * kt, n_delta - 1)
        k_t = kt_ref[0, :, pl.ds(k0, kt_len)]
        if not window:
            blk = lax.broadcasted_iota(jnp.int32, (NSP, kt_len), 0)
            key = lax.broadcasted_iota(jnp.int32, (NSP, kt_len), 1)
            one_hot = jnp.where(blk == (k0 + key) // SLC_LEN, 1.0, 0.0).astype(BF16)
            k_t = jnp.concatenate([k_t, one_hot], axis=0)
        s_scr[...] = jnp.dot(qrows, k_t, preferred_element_type=F32)
        for c in range(REP * QT // ROWS):
            r, qc = divmod(c, QT // ROWS)
            rows = pl.ds(c * ROWS, ROWS)
            s = s_scr[rows, :] + bias_ref[delta, r, pl.ds(qc * ROWS, ROWS), :]
            m_old = m_scr[rows, :]
            m_new = jnp.maximum(m_old, jnp.max(s, axis=1, keepdims=True))
            p = jnp.exp2(s - jnp.concatenate([m_new] * (kt_len // LANES), axis=1))
            alpha = jnp.exp2(m_old - m_new)
            l_scr[rows, :] = alpha * l_scr[rows, :] + jnp.sum(p, axis=1, keepdims=True)
            m_scr[rows, :] = m_new
            a_scr[rows, :] = alpha
            p_scr[rows, :] = p.astype(BF16)
        acc_scr[...] = a_scr[...] * acc_scr[...] + lax.dot_general(
            p_scr[...], vt_ref[0, :, pl.ds(k0, kt_len)], (((1,), (1,)), ((), ())), preferred_element_type=F32)
        return carry

    lax.fori_loop(lo, hi, body, 0)
    o = acc_scr[...] / l_scr[...]
    for r in range(REP):
        o_r = o[r * QT:(r + 1) * QT]
        o_ref[:, r * HD:(r + 1) * HD] = jnp.where(g == 0, o_r[:, :HD], o_r[:, HD:])


def _flash_prompt(zq, kv_t, kslot, vslot, bias, sel, n_seq, *, window):
    n_tok = zq.shape[0]
    t = kv_t.shape[2]
    nqb = t // QT
    kt_len = bias.shape[3]
    in_specs = [pl.BlockSpec((QT, REP * HD), lambda g, b, i: (b * nqb + i, g)),
                pl.BlockSpec((1, HD, t), lambda g, b, i: (b, HKV * kslot + g, 0)),
                pl.BlockSpec((1, LANES, t), lambda g, b, i: (b, vslot, 0)),
                pl.BlockSpec((bias.shape[0], REP, QT, kt_len), lambda g, b, i: (0, g, 0, 0))]
    args = [zq, kv_t, kv_t, bias]
    rows = REP * QT
    scratch = [pltpu.VMEM((rows, kt_len), F32), pltpu.VMEM((rows, kt_len), BF16), pltpu.VMEM((rows, LANES), F32),
               pltpu.VMEM((rows, LANES), F32), pltpu.VMEM((rows, LANES), F32), pltpu.VMEM((rows, LANES), F32)]
    if not window:
        in_specs.append(pl.BlockSpec((QT, HKV * NSP), lambda g, b, i: (b * nqb + i, 0)))
        args.append(sel)
    return pl.pallas_call(
        functools.partial(_flash_prompt_kernel, window=window, n_delta=bias.shape[0], kt_len=kt_len),
        grid=(HKV, n_seq, nqb),
        in_specs=in_specs,
        out_specs=pl.BlockSpec((QT, REP * HD), lambda g, b, i: (b * nqb + i, g)),
        out_shape=jax.ShapeDtypeStruct((n_tok, D_A), F32),
        scratch_shapes=scratch,
        compiler_params=_cparams("parallel", "parallel", "parallel"),
    )(*args)


def _nsa_sample_kernel(*refs, n_pages, nq, n_new, past, n_item, n_carried):
    n_in = 1 + n_item * n_pages
    all_pages = refs[1:n_in]
    q_ref, kvn_ref, wn_ref, wbuf_ref, ckv_ref, bcmp_ref, bslc_ref, bwin_ref, ov_ref, exp_ref = refs[n_in:n_in + 10]
    ocmp_ref, oslc_ref, owin_ref, wout_ref = refs[n_in + 10 + n_carried:n_in + 14 + n_carried]
    scratch = refs[n_in + 14 + n_carried:]
    for it in range(n_item):
        _nsa_sample_item(all_pages[it * n_pages:(it + 1) * n_pages], q_ref.at[it], kvn_ref.at[it], wn_ref.at[it],
                         wbuf_ref.at[it], ckv_ref.at[it], bcmp_ref, bslc_ref, bwin_ref, ov_ref, exp_ref,
                         ocmp_ref.at[it], oslc_ref.at[it], owin_ref.at[it], wout_ref.at[it],
                         scratch[2 * it], scratch[2 * it + 1], nq=nq, n_new=n_new, past=past)


def _nsa_sample_item(page_refs, q_ref, kvn_ref, wn_ref, wbuf_ref, ckv_ref, bcmp_ref, bslc_ref, bwin_ref, ov_ref,
                     exp_ref, ocmp_ref, oslc_ref, owin_ref, wout_ref, new_kv, new_w, *, nq, n_new, past):
    new_kv[nq:, :] = jnp.zeros((LANES - nq, 4 * LANES), F32)
    new_kv[:nq, :] = kvn_ref[...]
    new_w[nq:, :] = jnp.zeros((LANES - nq, 2 * LANES), F32)
    new_w[:nq, :] = wn_ref[...]
    win_buf = wbuf_ref.shape[1]
    shifted = pltpu.roll(wbuf_ref[...], win_buf - n_new, 1)
    new_t = jnp.concatenate([new_w[:, :LANES].T, new_w[:, LANES:].T], axis=0)
    new_t = pltpu.roll(new_t, LANES - n_new, 1)
    tail = lax.broadcasted_iota(jnp.int32, (2 * LANES, LANES), 1) >= LANES - n_new
    wout_ref[:, :win_buf - LANES] = shifted[:, :win_buf - LANES]
    wout_ref[:, win_buf - LANES:] = jnp.where(tail, new_t, shifted[:, win_buf - LANES:])

    q = q_ref[...] * (HD ** -0.5)
    ck, cv = ckv_ref[0], ckv_ref[1]
    qpos = past + lax.broadcasted_iota(jnp.int32, (nq, 1), 0)

    def attend(qpad, old_k, old_v, new_k, new_v, bias):
        s = jnp.concatenate([jnp.dot(qpad, k_t.astype(BF16), preferred_element_type=F32) for k_t in old_k]
                            + [_dot_nt(qpad, new_k)], axis=1) + bias
        e = jnp.exp(s - jnp.max(s, axis=1, keepdims=True))
        o, c0 = _dot(e[:, s.shape[1] - LANES:], new_v), 0
        for v_t in old_v:
            o = o + _dot_nt(e[:, c0:c0 + v_t.shape[1]], v_t)
            c0 += v_t.shape[1]
        return o / jnp.sum(e, axis=1, keepdims=True)

    picked = []
    n_blocks = -(-(past + nq) // SLC_LEN)
    for psum in _cmp_branch(q, ck, cv, bcmp_ref, ocmp_ref):
        sel = _select_blocks(_dot_f32(psum, ov_ref[...]), qpos, 1, n_blocks)
        picked += [jnp.dot(sel.astype(BF16), exp_ref[...], preferred_element_type=F32)] * REP
    qpad = jnp.concatenate([_padded_queries(q, g) for g in range(HKV)], axis=0)
    bslc = jnp.where(jnp.concatenate(picked, axis=0) > 0.5, bslc_ref[...].reshape(HQ * nq, -1), NEG)
    o = attend(qpad, [pr[0, 2 * LANES:3 * LANES, :] for pr in page_refs],
               [pr[0, 3 * LANES:4 * LANES, :] for pr in page_refs],
               new_kv[:, 2 * LANES:3 * LANES], new_kv[:, 3 * LANES:4 * LANES], bslc)
    _store_all_heads(oslc_ref, o, nq)
    o = attend(qpad, [wbuf_ref[:LANES, :]], [wbuf_ref[LANES:, :]], new_w[:, :LANES], new_w[:, LANES:],
               bwin_ref[...].reshape(HQ * nq, -1))
    _store_all_heads(owin_ref, o, nq)


def _nsa_sample(zq, kvn, wn, n_new, cache_t, page_base, page_table, win_t, win_base, ckv,
                bias_cmp, bias_slc, bias_win, ov, expand, carried):
    n_seq, nq, _ = zq.shape
    n_pages = page_table.shape[1]
    page = cache_t.shape[2]
    past = n_pages * page
    win_buf = win_t.shape[2]
    ncp = ckv.shape[2]
    n_item = _pick_tile(n_seq, (2, 1))
    seq3 = lambda b, pt: (b, 0, 0)
    const = lambda nd: (lambda b, pt: (0,) * nd)
    page_spec = lambda it, p: pl.BlockSpec((1, 4 * LANES, page),
                                           lambda b, pt: (page_base + pt[b * n_item + it, p], 0, 0))
    in_specs = [page_spec(it, p) for it in range(n_item) for p in range(n_pages)] + [
        pl.BlockSpec((n_item, nq, D_A), seq3),
        pl.BlockSpec((n_item, nq, 4 * LANES), seq3),
        pl.BlockSpec((n_item, nq, 2 * LANES), seq3),
        pl.BlockSpec((n_item, 2 * LANES, win_buf), lambda b, pt: (win_base // n_item + b, 0, 0)),
        pl.BlockSpec((n_item, 2, ncp, LANES), lambda b, pt: (b, 0, 0, 0)),
        pl.BlockSpec(bias_cmp.shape, const(3)),
        pl.BlockSpec(bias_slc.shape, const(3)),
        pl.BlockSpec(bias_win.shape, const(3)),
        pl.BlockSpec(ov.shape, const(2)),
        pl.BlockSpec(expand.shape, const(2))]
    out = pl.BlockSpec((n_item, nq, D_A), seq3)
    wout = pl.BlockSpec((n_item, 2 * LANES, win_buf), lambda b, pt: (win_base // n_item + b, 0, 0))
    c_args, c_specs, n_carried = _carry_args(carried)
    n_in = 1 + n_item * n_pages + 10
    return pl.pallas_call(
        functools.partial(_nsa_sample_kernel, n_pages=n_pages, nq=nq, n_new=n_new, past=past, n_item=n_item,
                          n_carried=n_carried),
        grid_spec=pltpu.PrefetchScalarGridSpec(
            num_scalar_prefetch=1, grid=(n_seq // n_item,), in_specs=in_specs + c_specs,
            out_specs=[out, out, out, wout],
            scratch_shapes=[pltpu.VMEM((LANES, 4 * LANES), F32), pltpu.VMEM((LANES, 2 * LANES), F32)] * n_item),
        out_shape=[jax.ShapeDtypeStruct((n_seq, nq, D_A), F32)] * 3 + [jax.ShapeDtypeStruct(win_t.shape, F32)],
        input_output_aliases={n_in: 3} if n_carried else {},
        compiler_params=_cparams("parallel"),
    )(page_table, *([cache_t] * (n_item * n_pages)), zq, kvn, wn, win_t, ckv, bias_cmp, bias_slc, bias_win, ov, expand,
      *c_args)


def _merge_kernel(x_ref, mout_ref, ocmp_ref, oslc_ref, owin_ref, zg_ref, zgm_ref, gexp_ref, wb_ref, wo_ref,
                  lg_ref, lb_ref, o_ref, *, alpha):
    d = x_ref.shape[1]
    gate = jax.nn.sigmoid(zg_ref[...])
    terms, rest = [], gate
    for _ in range(3):
        terms.append(rest.astype(BF16))
        rest = rest - terms[-1].astype(F32)
    spread = jnp.dot(jnp.concatenate(terms, axis=1), gexp_ref[...], preferred_element_type=F32)
    a = None
    for br, src in enumerate((ocmp_ref, oslc_ref, owin_ref)):
        term = spread[:, br * D_A:(br + 1) * D_A] * src[...]
        a = term if a is None else a + term
    u = (jax.nn.sigmoid(zgm_ref[:, :d]) * _dot(mout_ref[...], wb_ref[0])
         + jax.nn.sigmoid(zgm_ref[:, d:]) * _dot(a, wb_ref[1]))
    y = _dot(u, wo_ref[...])
    o_ref[...] = _layer_norm(alpha * x_ref[...] + y, lg_ref[...], lb_ref[...])


def _merge(x, mout, ocmp, oslc, owin, zg, zgm, gexp, wb, wo, lg, lb, alpha):
    n_tok, d = x.shape
    tm = _pick_tile(n_tok, (256, 128, 64, 32, 16, 8))
    row = lambda i: (i, 0)
    c2 = lambda i: (0, 0)
    c3 = lambda i: (0, 0, 0)
    return pl.pallas_call(
        functools.partial(_merge_kernel, alpha=alpha),
        grid=(n_tok // tm,),
        in_specs=[pl.BlockSpec((tm, d), row), pl.BlockSpec((tm, D_M), row), pl.BlockSpec((tm, D_A), row),
                  pl.BlockSpec((tm, D_A), row), pl.BlockSpec((tm, D_A), row), pl.BlockSpec((tm, LANES), row),
                  pl.BlockSpec((tm, 2 * d), row), pl.BlockSpec(gexp.shape, c2), pl.BlockSpec(wb.shape, c3),
                  pl.BlockSpec(wo.shape, c2), pl.BlockSpec((1, d), c2), pl.BlockSpec((1, d), c2)],
        out_specs=pl.BlockSpec((tm, d), row),
        out_shape=jax.ShapeDtypeStruct((n_tok, d), F32),
        compiler_params=_cparams("parallel"),
    )(x, mout, ocmp, oslc, owin, zg, zgm, gexp, wb, wo, lg.reshape(1, d), lb.reshape(1, d))


def _route(aff, sel):
    def top2_sum(a, b, c, d):
        x, x2, y, y2 = jnp.maximum(a, b), jnp.minimum(a, b), jnp.maximum(c, d), jnp.minimum(c, d)
        return jnp.maximum(x, y) + jnp.maximum(jnp.minimum(x, y), jnp.maximum(x2, y2))

    gsum = [top2_sum(*sel[EXP_PER_GROUP * gi:EXP_PER_GROUP * (gi + 1)]) for gi in range(N_GROUPS)]
    gmax = functools.reduce(jnp.maximum, gsum)
    chosen, taken = [], None
    for gi in range(N_GROUPS):
        is_best = gsum[gi] == gmax if taken is None else (gsum[gi] == gmax) & jnp.logical_not(taken)
        taken = is_best if taken is None else taken | is_best
        members = range(EXP_PER_GROUP * gi, EXP_PER_GROUP * (gi + 1))
        for e in members:
            ahead = None
            for e2 in members:
                if e2 == e:
                    continue
                before = (sel[e2] >= sel[e]) if e2 < e else (sel[e2] > sel[e])
                cnt = jnp.where(before, 1.0, 0.0)
                ahead = cnt if ahead is None else ahead + cnt
            chosen.append(is_best & (ahead < TOP_K))
    picked = [jnp.where(c, a, 0.0) for c, a in zip(chosen, aff)]
    total = functools.reduce(lambda a, b: a + b, picked)
    return [p / total for p in picked]


def _moe_kernel(x_ref, wr_ref, rb_ref, wg_ref, wu_ref, wd_ref, lg_ref, lb_ref, o_ref, xb, comb, acc, *, alpha):
    step = pl.program_id(1)
    tm = x_ref.shape[0]
    lane = lax.broadcasted_iota(jnp.int32, (tm, LANES), 1)

    @pl.when(step == 0)
    def _():
        x16 = x_ref[...].astype(BF16)
        xb[...] = x16
        aff_t = jax.nn.sigmoid(_dot_nt(wr_ref[...], x16))
        sel_t = aff_t + rb_ref[...]
        weights = _route([aff_t[i:i + 1, :] for i in range(N_EXP)], [sel_t[i:i + 1, :] for i in range(N_EXP)])
        expert = lax.broadcasted_iota(jnp.int32, (N_EXP, tm), 0)
        comb_t = jnp.zeros((N_EXP, tm), F32)
        for i, w in enumerate(weights):
            comb_t = jnp.where(expert == i, w, comb_t)
        comb[...] = jnp.concatenate([comb_t, jnp.zeros((LANES - N_EXP, tm), F32)], axis=0).T
        acc[...] = jnp.zeros_like(acc)

    x16 = xb[...]
    hs = []
    for j in range(EXP_PER_STEP):
        w_e = jnp.sum(jnp.where(lane == EXP_PER_STEP * step + j, comb[...], 0.0), axis=1, keepdims=True)
        hg = jnp.dot(x16, wg_ref[0, j], preferred_element_type=F32)
        hu = jnp.dot(x16, wu_ref[0, j], preferred_element_type=F32)
        hs.append((hg * jax.nn.sigmoid(hg) * hu * w_e).astype(BF16))
    d = acc.shape[1]
    acc[...] += jnp.dot(jnp.concatenate(hs, axis=1), wd_ref[0].reshape(EXP_PER_STEP * D_EXP, d),
                        preferred_element_type=F32)

    @pl.when(step == pl.num_programs(1) - 1)
    def _():
        o_ref[...] = _layer_norm(alpha * x_ref[...] + acc[...], lg_ref[...], lb_ref[...])


def _moe(x, wr, rb, wg, wu, wd, layer, lg, lb, alpha):
    n_tok, d = x.shape
    tm = _pick_tile(n_tok, (1024, 512, 256, 128, 64, 32, 16, 8))
    row = lambda i, e: (i, 0)
    c2 = lambda i, e: (0, 0)
    wsel = lambda i, e: (layer, e, 0, 0)
    return pl.pallas_call(
        functools.partial(_moe_kernel, alpha=alpha),
        grid=(n_tok // tm, N_EXP // EXP_PER_STEP),
        in_specs=[pl.BlockSpec((tm, d), row), pl.BlockSpec(wr.shape, c2), pl.BlockSpec(rb.shape, c2),
                  pl.BlockSpec((1, EXP_PER_STEP, d, D_EXP), wsel), pl.BlockSpec((1, EXP_PER_STEP, d, D_EXP), wsel),
                  pl.BlockSpec((1, EXP_PER_STEP, D_EXP, d), wsel), pl.BlockSpec((1, d), c2), pl.BlockSpec((1, d), c2)],
        out_specs=pl.BlockSpec((tm, d), row),
        out_shape=jax.ShapeDtypeStruct((n_tok, d), F32),
        scratch_shapes=[pltpu.VMEM((tm, d), BF16), pltpu.VMEM((tm, LANES), F32), pltpu.VMEM((tm, d), F32)],
        compiler_params=_cparams("parallel", "arbitrary"),
    )(x, wr, rb, wg, wu, wd, lg.reshape(1, d), lb.reshape(1, d))


def _bucket_np(dist):
    n = np.maximum(dist, 0)
    exact = N_BUCKETS // 2
    nf = np.maximum(n, 1).astype(np.float32)
    large = exact + (np.log(nf / np.float32(exact)) / np.float32(math.log(REL_MAX_DIST / exact))
                     * np.float32(N_BUCKETS - exact)).astype(np.int32)
    return np.where(n < exact, n, np.minimum(large, N_BUCKETS - 1)).astype(np.int32)


def _bias_table(rel_bias, dist, valid):
    tab = rel_bias.astype(F32)[jnp.asarray(_bucket_np(dist))]
    tab = jnp.where(jnp.asarray(valid)[..., None], tab, NEG)
    return jnp.moveaxis(tab, -1, 0)


def _bias_tiles_kernel(rb_ref, o_ref, *, tile_step, key_stride, key_off, hi_valid, upper, lead):
    t = pl.program_id(0)
    shape = o_ref.shape[-2:]
    dist = (tile_step * t - key_off + lax.broadcasted_iota(jnp.int32, shape, 0)
            - key_stride * lax.broadcasted_iota(jnp.int32, shape, 1))
    valid = dist >= 0 if hi_valid is None else (dist >= 0) & (dist < hi_valid)
    acc = [jnp.full(shape, rb_ref[N_BUCKETS - 1, h], F32) for h in range(HQ)]
    for b in range(N_BUCKETS - 2, -1, -1):
        below = dist < upper[b]
        acc = [jnp.where(below, rb_ref[b, h], a) for h, a in enumerate(acc)]
    for h in range(HQ):
        if lead:
            o_ref[0, h] = jnp.where(valid, acc[h], NEG)
        else:
            o_ref[h] = jnp.where(valid, acc[h], NEG)


def _bias_tiles(rel_bias, n_tiles, n_keys, *, tile_step, key_stride, key_off, hi_valid, lead, scale=1.0):
    buckets = _bucket_np(np.arange(8 * REL_MAX_DIST))
    upper = tuple(int(np.searchsorted(buckets, b, side='right')) for b in range(N_BUCKETS - 1))
    if lead:
        out_spec = pl.BlockSpec((1, HQ, QT, n_keys), lambda t: (t, 0, 0, 0))
        out_shape = jax.ShapeDtypeStruct((n_tiles, HQ, QT, n_keys), F32)
    else:
        out_spec = pl.BlockSpec((HQ, QT, n_keys), lambda t: (0, t, 0))
        out_shape = jax.ShapeDtypeStruct((HQ, n_tiles * QT, n_keys), F32)
    return pl.pallas_call(
        functools.partial(_bias_tiles_kernel, tile_step=tile_step, key_stride=key_stride, key_off=key_off,
                          hi_valid=hi_valid, upper=upper, lead=lead),
        grid=(n_tiles,),
        in_specs=[pl.BlockSpec(memory_space=pltpu.SMEM)],
        out_specs=out_spec, out_shape=out_shape,
        compiler_params=_cparams("parallel"),
    )(rel_bias.astype(F32) * scale)


def _overlap_np(ncp, n_cmp, n_slc):
    c0 = np.arange(ncp)[:, None] * CMP_STRIDE
    j0 = np.arange(NSP)[None, :] * SLC_LEN
    ov = (c0 < j0 + SLC_LEN) & (c0 + CMP_LEN > j0)
    ov &= (np.arange(ncp)[:, None] < n_cmp) & (np.arange(NSP)[None, :] < n_slc)
    return ov.astype(np.float32)


def kernel(x_prompt, x_sample, cache_kv, state_win_kv, state_mlstm_C, state_mlstm_n, state_mlstm_m, page_table,
           w_in, b_in, w_cmp1, w_cmp2, mh_norm_g, w_branch, w_out, ln1_g, ln1_b, ln2_g, ln2_b,
           w_router, router_bias, w_gate_e, w_up_e, w_down_e, rel_bias):
    B, T, D = x_prompt.shape
    DB, TS, _ = x_sample.shape
    depth, n_pool, page = cache_kv.shape[:3]
    n_pages = page_table.shape[1]
    past = n_pages * page
    win_buf = state_win_kv.shape[2]
    alpha = (2 * depth) ** 0.25
    assert T % KT == 0 and T // SLC_LEN <= NSP and page % CMP_STRIDE == 0 and past % SLC_LEN == 0
    assert win_buf == min(WINDOW, past) and T >= win_buf

    off = np.cumsum((0, D_M, D_M, D_M, D_M, H_M, H_M, D_A, 6 * HKV * HD, 3 * HQ, 2 * D)).tolist()
    seg = lambda a, i, j: a[..., off[i]:off[j]]
    small = lambda a: jnp.concatenate(
        [seg(a, 4, 6), seg(a, 8, 9), jnp.zeros(a.shape[:-1] + (LANES - 2 * H_M - 3 * HQ,), a.dtype)], axis=-1)
    kv_cmp, kv_mid = off[7] + 2 * HKV * HD, off[7] + 4 * HKV * HD
    groups_s = lambda a: (seg(a, 0, 4), small(a), seg(a, 6, 7), a[..., off[7]:kv_mid], a[..., kv_mid:off[8]], seg(a, 9, 10))
    groups_p = lambda a: (seg(a, 0, 4), small(a), seg(a, 6, 7), a[..., off[7]:kv_cmp], seg(a, 9, 10))
    ws_s = [w.astype(BF16) for w in groups_s(w_in)]
    bs_s = [b[:, None, :] for b in groups_s(b_in)]
    ws_p = [w.astype(BF16) for w in groups_p(w_in)]
    bs_p = [b[:, None, :] for b in groups_p(b_in)]
    w_in_t = jnp.transpose(w_in, (0, 2, 1))
    wt_p = w_in_t[:, off[7]:off[8]].astype(BF16)
    bt_p = b_in[:, off[7]:off[8], None]
    wb16, wo16 = w_branch.astype(BF16), w_out.astype(BF16)
    wg16, wu16, wd16 = w_gate_e.astype(BF16), w_up_e.astype(BF16), w_down_e.astype(BF16)
    wr_pad = jnp.transpose(w_router).astype(BF16)
    rb_pad = router_bias.astype(F32).reshape(N_EXP, 1)
    eye = jnp.eye(HKV, dtype=F32)
    w1 = w_cmp1.reshape(depth, 2, 2, CMP_STRIDE, HD, CMP_HID)
    w1p = jnp.einsum('lshrdf,gG->lsrgdhGf', w1, eye).reshape(depth, 2, CMP_STRIDE, HKV * HD, 2 * HKV * CMP_HID)
    w1p = w1p.astype(BF16)
    w2p = jnp.einsum('lsfd,gG->lsgfGd', w_cmp2, eye).reshape(depth, 2, HKV * CMP_HID, HKV * HD).astype(BF16)
    gexp = np.zeros((LANES, 3 * D_A), np.float32)
    for br in range(3):
        for h in range(HQ):
            gexp[2 * H_M + br * HQ + h, br * D_A + h * HD:br * D_A + (h + 1) * HD] = 1.0
    gexp = jnp.asarray(np.tile(gexp, (3, 1)), dtype=BF16)

    ncp_p, n_cmp_p = T // CMP_STRIDE, (T - CMP_LEN) // CMP_STRIDE + 1
    assert n_cmp_p * CMP_STRIDE + CMP_LEN - 1 > T - 1
    bias_cmp_p = _bias_tiles(rel_bias, T // QT, ncp_p, tile_step=QT, key_stride=CMP_STRIDE, key_off=CMP_LEN - 1,
                             hi_valid=None, lead=False)
    ov_t_p = jnp.asarray(_overlap_np(ncp_p, n_cmp_p, -(-T // SLC_LEN)).T)
    n_far = -(-(KT - 1 + REL_MAX_DIST) // QT)
    bias_slc_p = _bias_tiles(rel_bias, n_far + 1, KT, tile_step=QT, key_stride=1, key_off=0, hi_valid=None, lead=True,
                             scale=LOG2E)
    bias_win_p = _bias_tiles(rel_bias, (WINDOW + KT_WIN) // QT, KT_WIN, tile_step=QT, key_stride=1, key_off=0,
                             hi_valid=WINDOW, lead=True, scale=LOG2E)

    ncp_s = past // CMP_STRIDE
    n_cmp_s = (past + TS - CMP_LEN) // CMP_STRIDE + 1
    assert n_cmp_s <= ncp_s and -(-(past + TS) // SLC_LEN) <= NSP and TS <= NQS
    qs = past + np.arange(NQS)[:, None]
    d_cmp_s = qs - (np.arange(ncp_s)[None, :] * CMP_STRIDE + CMP_LEN - 1)
    bias_cmp_s = _bias_table(rel_bias, d_cmp_s, (d_cmp_s >= 0) & (np.arange(ncp_s)[None, :] < n_cmp_s))
    ov_s = jnp.asarray(_overlap_np(ncp_s, n_cmp_s, -(-(past + TS) // SLC_LEN)))
    key_s = np.arange(past + LANES)[None, :]
    d_slc_s = qs - key_s
    bias_slc_s = _bias_table(rel_bias, d_slc_s, (d_slc_s >= 0) & (key_s < past + TS))
    expand_s = jnp.asarray((np.arange(NSP)[:, None] == key_s // SLC_LEN).astype(np.float32)).astype(BF16)
    idx_w = np.arange(win_buf + LANES)[None, :]
    d_win_s = qs - (past - win_buf + idx_w)
    bias_win_s = _bias_table(rel_bias, d_win_s, (d_win_s >= 0) & (d_win_s < WINDOW) & (idx_w < win_buf + TS))

    to_feature_major = lambda a: jnp.transpose(a, (0, 1, 3, 4, 5, 2))
    cache_t = to_feature_major(cache_kv).reshape(depth * n_pool, 4 * LANES, page)
    win_t = to_feature_major(state_win_kv).reshape(depth * DB, 2 * LANES, win_buf)
    from_feature_major = lambda a, slots: jnp.transpose(
        a.reshape(a.shape[:2] + (slots, HKV, HD, a.shape[-1])), (0, 1, 5, 2, 3, 4))
    half_per_page = page // CMP_STRIDE

    LP = _pick_tile(T, (256, 128, 64))
    LS = NQS
    zeros_state = (jnp.zeros((B, H_M, DH_M, DH_M), F32), jnp.zeros((B, H_M, DH_M), F32), jnp.zeros((B, H_M), F32))
    state_c_all = state_mlstm_C.reshape(depth * DB, H_M, DH_M, DH_M)

    xp = x_prompt.reshape(B * T, D)
    xs = x_sample.reshape(DB * TS, D)
    outs = [[] for _ in range(10)]
    kv_all = jnp.zeros((depth * B, 4 * LANES, T), F32)
    win_all = jnp.zeros(win_t.shape, F32)
    c_all = jnp.zeros(state_c_all.shape, F32)
    for l in range(depth):
        zm, zg, zq, zc, zgm, kv_all, zw_t, z16_t = _inproj(
            xp, [w[l] for w in ws_p], [b[l] for b in bs_p], wt_p[l], bt_p[l], t_split=4 * LANES, n_seq=B,
            layer=l, depth=depth, carried=kv_all)
        mout, c_p, n_p, m_p = _mlstm(zm, zg, mh_norm_g[l], *zeros_state, n_seq=B, L=LP, l_valid=LP)
        part = _cmp_partial(zc.reshape(B * T // CMP_STRIDE, CMP_STRIDE * 2 * LANES), w1p[l])
        ckv = _cmp_finish_prompt(part, w2p[l], B)
        ocmp, sel = _nsa_cmp_prompt(zq, ckv, bias_cmp_p, ov_t_p, B)
        oslc = _flash_prompt(zq, z16_t, 2, 3, bias_slc_p, sel, B, window=False)
        owin = _flash_prompt(zq, z16_t, 4, 5, bias_win_p, None, B, window=True)
        x1 = _merge(xp, mout, ocmp, oslc, owin, zg, zgm, gexp, wb16[l], wo16[l], ln1_g[l], ln1_b[l], alpha)
        xp = _moe(x1, wr_pad, rb_pad, wg16, wu16, wd16, l, ln2_g[l], ln2_b[l], alpha)
        outs[1].append(zw_t[:, :, T - win_buf:])
        outs[2].append(c_p)
        outs[3].append(n_p)
        outs[4].append(m_p)
        zm, zg, zq, zkv, zw, zgm = _inproj(xs, [w[l] for w in ws_s], [b[l] for b in bs_s])
        padt = lambda a: jnp.pad(a.reshape(DB, TS, -1), ((0, 0), (0, LS - TS), (0, 0))).reshape(DB * LS, -1)
        mout, c_all, n_s, m_s = _mlstm(padt(zm), padt(zg), mh_norm_g[l], state_c_all, state_mlstm_n[l],
                                       state_mlstm_m[l], n_seq=DB, L=LS, l_valid=TS, c_base=l * DB, c_layers=depth,
                                       carried=c_all)
        mout = mout.reshape(DB, LS, D_M)[:, :TS].reshape(DB * TS, D_M)
        part = _cmp_partial_paged(cache_t, w1p[l].reshape(2, CMP_STRIDE // 2, 2 * LANES, -1), l, n_pool)
        ckv = _cmp_finish_paged(part, w2p[l], page_table, 0, half_per_page)
        padq = lambda a: jnp.pad(a.reshape(DB, TS, -1), ((0, 0), (0, NQS - TS), (0, 0)))
        *o3, win_all = _nsa_sample(padq(zq), padq(zkv), padq(zw), TS, cache_t, l * n_pool, page_table, win_t, l * DB,
                                   ckv, bias_cmp_s, bias_slc_s, bias_win_s, ov_s, expand_s, win_all)
        ocmp, oslc, owin = [o[:, :TS].reshape(DB * TS, D_A) for o in o3]
        x1 = _merge(xs, mout, ocmp, oslc, owin, zg, zgm, gexp, wb16[l], wo16[l], ln1_g[l], ln1_b[l], alpha)
        xs = _moe(x1, wr_pad, rb_pad, wg16, wu16, wd16, l, ln2_g[l], ln2_b[l], alpha)
        outs[5].append(zkv.reshape(DB, TS, 4, HKV, HD))
        outs[8].append(n_s)
        outs[9].append(m_s)
    stacked = [jnp.stack(o) if o else None for o in outs]
    stacked[0] = from_feature_major(kv_all.reshape(depth, B, 4 * LANES, T), 4)
    stacked[1] = from_feature_major(stacked[1], 2)
    stacked[6] = from_feature_major(win_all.reshape(depth, DB, 2 * LANES, win_buf), 2)
    stacked[7] = c_all.reshape(depth, DB, H_M, DH_M, DH_M)
    return (xp.reshape(B, T, D), xs.reshape(DB, TS, D)) + tuple(stacked)
```

```python
import functools
import math

import numpy as np
import jax
import jax.numpy as jnp
from jax import lax
from jax.experimental import pallas as pl
from jax.experimental.pallas import tpu as pltpu

F32 = jnp.float32
BF16 = jnp.bfloat16
HIGHEST = lax.Precision.HIGHEST

H_M, DH_M = 4, 128
D_M = H_M * DH_M
HQ, HKV, HD = 8, 2, 64
REP = HQ // HKV
D_A = HQ * HD
CMP_LEN, CMP_STRIDE, CMP_HID = 32, 16, 256
SLC_LEN, N_SEL, WINDOW = 64, 16, 512
N_BUCKETS, REL_MAX_DIST = 32, 128
N_EXP, N_GROUPS, TOP_K, D_EXP = 16, 4, 2, 256
EXP_PER_GROUP = N_EXP // N_GROUPS
EXP_PER_STEP = 2
LN_EPS = 1e-5
NEG = -1e30

LANES = 128
QT = 128
KT = 512
KT_WIN = 512
LOG2E = math.log2(math.e)
ROWS = 32
NSP = 64
NQS = 8
VMEM_LIMIT = 56 * 1024 * 1024


def _cparams(*sem):
    return pltpu.CompilerParams(dimension_semantics=sem, vmem_limit_bytes=VMEM_LIMIT)


def _dot(a, b):
    return jnp.dot(a.astype(BF16), b.astype(BF16), preferred_element_type=F32)


def _dot_nt(a, b):
    return lax.dot_general(a.astype(BF16), b.astype(BF16), (((1,), (1,)), ((), ())), preferred_element_type=F32)


def _dot_tn(a, b):
    return lax.dot_general(a.astype(BF16), b.astype(BF16), (((0,), (0,)), ((), ())), preferred_element_type=F32)


def _dot_f32(a, b):
    return jnp.dot(a, b, precision=HIGHEST, preferred_element_type=F32)


def _dot_nt_f32(a, b):
    return lax.dot_general(a, b, (((1,), (1,)), ((), ())), precision=HIGHEST, preferred_element_type=F32)


def _pick_tile(n, cands):
    for c in cands:
        if n % c == 0:
            return c
    raise ValueError(f"no tile for {n}")


def _layer_norm(y, g, b):
    mu = jnp.mean(y, axis=-1, keepdims=True)
    yc = y - mu
    var = jnp.mean(yc * yc, axis=-1, keepdims=True)
    return yc * lax.rsqrt(var + LN_EPS) * g + b


def _inproj_kernel(x_ref, *refs, n_plain, t_split, n_carried):
    n_w = n_plain + (1 if t_split else 0)
    w_refs, b_refs, o_refs = refs[:n_plain], refs[n_w:n_w + n_plain], refs[2 * n_w + n_carried:]
    x = x_ref[...].astype(BF16)
    for w_ref, b_ref, o_ref in zip(w_refs, b_refs, o_refs):
        o_ref[...] = jnp.dot(x, w_ref[...], preferred_element_type=F32) + b_ref[...]
    if t_split:
        wt_ref, bt_ref = refs[n_plain], refs[n_w + n_plain]
        zt = lax.dot_general(wt_ref[...], x, (((1,), (1,)), ((), ())), preferred_element_type=F32) + bt_ref[...]
        lo_ref, hi_ref, all16_ref = o_refs[n_plain:]
        lo_ref[0] = zt[:t_split]
        hi_ref[0] = zt[t_split:]
        all16_ref[0] = zt.astype(BF16)


def _carry_args(carried):
    if carried is None:
        return [], [], 0
    return [carried], [pl.BlockSpec(memory_space=pl.ANY)], 1


def _inproj(x, ws, bs, wt=None, bt=None, t_split=0, n_seq=1, layer=0, depth=1, carried=None):
    n_tok, d = x.shape
    t = n_tok // n_seq
    tm = _pick_tile(t, (256, 128, 64, 32, 16, 8))
    nt = t // tm
    full = lambda i: (0, 0)
    row = lambda i: (i, 0)
    tspec = lambda n, base: pl.BlockSpec((1, n, tm), lambda i: (base + i // nt, 0, i % nt))
    extra_w = [] if wt is None else [wt]
    extra_b = [] if wt is None else [bt]
    n_t = 0 if wt is None else wt.shape[0]
    t_specs = [] if wt is None else [tspec(t_split, layer * n_seq), tspec(n_t - t_split, 0), tspec(n_t, 0)]
    t_shapes = [] if wt is None else [jax.ShapeDtypeStruct((depth * n_seq, t_split, t), F32),
                                      jax.ShapeDtypeStruct((n_seq, n_t - t_split, t), F32),
                                      jax.ShapeDtypeStruct((n_seq, n_t, t), BF16)]
    c_args, c_specs, n_carried = _carry_args(carried)
    n_in = 1 + 2 * (len(ws) + len(extra_w))
    return pl.pallas_call(
        functools.partial(_inproj_kernel, n_plain=len(ws), t_split=t_split if wt is not None else 0,
                          n_carried=n_carried),
        grid=(n_tok // tm,),
        in_specs=[pl.BlockSpec((tm, d), row)]
        + [pl.BlockSpec(w.shape, full) for w in (*ws, *extra_w)]
        + [pl.BlockSpec(b.shape, full) for b in (*bs, *extra_b)] + c_specs,
        out_specs=[pl.BlockSpec((tm, w.shape[1]), row) for w in ws] + t_specs,
        out_shape=[jax.ShapeDtypeStruct((n_tok, w.shape[1]), F32) for w in ws] + t_shapes,
        input_output_aliases={n_in: len(ws)} if n_carried else {},
        compiler_params=_cparams("parallel"),
    )(x, *ws, *extra_w, *bs, *extra_b, *c_args)


def _log_sigmoid(x):
    return jnp.minimum(x, 0.0) - jnp.log(1.0 + jnp.exp(-jnp.abs(x)))


def _mlstm_kernel(q_ref, k_ref, v_ref, o_ref, g_ref, ng_ref, c0_ref, n0_ref, m0_ref, *rest, L, l_valid):
    mout_ref, c_ref, n_ref, m_ref, cs, ns, ms = rest[-7:]
    c = pl.program_id(1)

    @pl.when(c == 0)
    def _():
        cs[...] = c0_ref[0]
        ns[...] = n0_ref[0]
        ms[...] = m0_ref[0]

    g = g_ref[...]
    row = lax.broadcasted_iota(jnp.int32, (L, L), 0)
    col = lax.broadcasted_iota(jnp.int32, (L, L), 1)
    causal = row >= col
    fcum = _dot_f32(causal.astype(F32), _log_sigmoid(g))
    lane = lax.broadcasted_iota(jnp.int32, (L, LANES), 1)
    y = jnp.where(lane < H_M, g - pltpu.roll(fcum, LANES - H_M, 1), fcum)
    yt = y.T
    rowv = lax.broadcasted_iota(jnp.int32, (L, LANES), 0)
    r = l_valid - 1
    wide = (lambda x: jnp.concatenate([x] * (L // LANES), axis=1)) if L >= LANES else (lambda x: x[:, :L])
    for h in range(H_M):
        hs = slice(h * DH_M, (h + 1) * DH_M)
        f_col = jnp.broadcast_to(fcum[:, H_M + h:H_M + h + 1], (L, LANES))
        a_row = yt[h:h + 1, :]
        i_col = jnp.broadcast_to(g[:, h:h + 1], (L, LANES))
        m0 = ms[h]
        n0 = ns[h]
        c0 = cs[h]
        dm = jnp.where(causal, wide(f_col) + a_row, NEG)
        b = f_col + m0
        mrow = jnp.maximum(b, jnp.max(dm, axis=1, keepdims=True))
        w = jnp.exp(dm - wide(mrow))
        dec = jnp.exp(b - mrow)
        q = q_ref[:, hs]
        k = k_ref[:, hs] * (DH_M ** -0.5)
        v = v_ref[:, hs]
        s = _dot_nt(q, k) * w
        num = dec * _dot_nt(q, c0) + _dot(s, v)
        den = dec * jnp.sum(q * n0, axis=1, keepdims=True) + jnp.sum(s, axis=1, keepdims=True)
        hh = num / jnp.maximum(jnp.abs(den), jnp.exp(-mrow))
        mu = jnp.mean(hh, axis=1, keepdims=True)
        hc = hh - mu
        var = jnp.mean(hc * hc, axis=1, keepdims=True)
        hn = hc * lax.rsqrt(var + LN_EPS) * ng_ref[:, hs]
        mout_ref[:, hs] = (hn * jax.nn.sigmoid(o_ref[:, hs])).astype(mout_ref.dtype)
        f_r = f_col[r:r + 1, :]
        m_r = mrow[r:r + 1, :]
        w_last = jnp.where(rowv <= r, jnp.exp(f_r - f_col + i_col - m_r), 0.0)
        d_last = dec[r:r + 1, :]
        cs[h] = d_last * c0 + _dot_tn(v * w_last, k)
        ns[h] = d_last * n0 + jnp.sum(k * w_last, axis=0, keepdims=True)
        ms[h] = m_r

    @pl.when(c == pl.num_programs(1) - 1)
    def _():
        c_ref[0] = cs[...]
        n_ref[0] = ns[...]
        m_ref[0] = ms[...]


def _mlstm(zm, zg, norm_g, c0, n0, m0, *, n_seq, L, l_valid, c_base=0, c_layers=1, carried=None):
    n_tok = zm.shape[0]
    t = n_tok // n_seq
    nc = t // L
    n0 = n0.reshape(n_seq, H_M, 1, DH_M)
    m0 = jnp.broadcast_to(m0.reshape(n_seq, H_M, 1, 1), (n_seq, H_M, 1, LANES))
    colblk = lambda j: (lambda b, c: (b * nc + c, j))
    st4 = lambda b, c: (b, 0, 0, 0)
    c_spec = pl.BlockSpec((1, H_M, DH_M, DH_M), lambda b, c: (c_base + b, 0, 0, 0))
    c_args, c_specs, n_carried = _carry_args(carried)
    mout, c_new, n_new, m_new = pl.pallas_call(
        functools.partial(_mlstm_kernel, L=L, l_valid=l_valid),
        grid=(n_seq, nc),
        in_specs=[pl.BlockSpec((L, D_M), colblk(0)), pl.BlockSpec((L, D_M), colblk(1)),
                  pl.BlockSpec((L, D_M), colblk(2)), pl.BlockSpec((L, D_M), colblk(3)),
                  pl.BlockSpec((L, LANES), colblk(0)),
                  pl.BlockSpec((1, D_M), lambda b, c: (0, 0)),
                  c_spec,
                  pl.BlockSpec((1, H_M, 1, DH_M), st4),
                  pl.BlockSpec((1, H_M, 1, LANES), st4)] + c_specs,
        out_specs=[pl.BlockSpec((L, D_M), colblk(0)),
                   c_spec,
                   pl.BlockSpec((1, H_M, 1, DH_M), st4),
                   pl.BlockSpec((1, H_M, 1, LANES), st4)],
        out_shape=[jax.ShapeDtypeStruct((n_tok, D_M), BF16),
                   jax.ShapeDtypeStruct((c_layers * n_seq, H_M, DH_M, DH_M), F32),
                   jax.ShapeDtypeStruct((n_seq, H_M, 1, DH_M), F32),
                   jax.ShapeDtypeStruct((n_seq, H_M, 1, LANES), F32)],
        scratch_shapes=[pltpu.VMEM((H_M, DH_M, DH_M), F32), pltpu.VMEM((H_M, 1, DH_M), F32),
                        pltpu.VMEM((H_M, 1, LANES), F32)],
        input_output_aliases={9: 1} if n_carried else {},
        compiler_params=_cparams("parallel", "arbitrary"),
    )(zm, zm, zm, zm, zg, norm_g.reshape(1, D_M), c0, n0, m0, *c_args)
    return mout, c_new, n_new.reshape(n_seq, H_M, DH_M), m_new[:, :, 0, 0]


def _cmp_partial_kernel(x_ref, w_ref, o_ref):
    @pl.when(pl.program_id(2) == 0)
    def _():
        o_ref[...] = jnp.zeros_like(o_ref)

    o_ref[0] += _dot(x_ref[...], w_ref[0, 0])


def _cmp_partial(rows16, w1p):
    n_half = rows16.shape[0]
    tm = _pick_tile(n_half, (2048, 1024, 512, 256, 128, 64, 32, 16, 8))
    hid2 = 2 * HKV * CMP_HID
    return pl.pallas_call(
        _cmp_partial_kernel,
        grid=(2, n_half // tm, CMP_STRIDE),
        in_specs=[pl.BlockSpec((tm, LANES), lambda s, i, r: (i, 2 * r + s)),
                  pl.BlockSpec((1, 1, LANES, hid2), lambda s, i, r: (s, r, 0, 0))],
        out_specs=pl.BlockSpec((1, tm, hid2), lambda s, i, r: (s, i, 0)),
        out_shape=jax.ShapeDtypeStruct((2, n_half, hid2), F32),
        compiler_params=_cparams("parallel", "parallel", "arbitrary"),
    )(rows16, w1p)


def _gelu_tanh(x):
    return 0.5 * x * (1.0 + jnp.tanh(math.sqrt(2.0 / math.pi) * (x + 0.044715 * (x * x * x))))


def _cmp_finish_kernel(p_ref, w2_ref, o_ref):
    half = HKV * CMP_HID
    for s in range(2):
        p = p_ref[s]
        n = p.shape[0]
        hid = _gelu_tanh(p[:, :half] + pltpu.roll(p[:, half:], n - 1, 0))
        o_ref[0, s] = _dot(hid, w2_ref[s])


def _cmp_finish_prompt(part, w2p, n_seq):
    n_half = part.shape[1] // n_seq
    hid2 = part.shape[2]
    return pl.pallas_call(
        _cmp_finish_kernel,
        grid=(n_seq,),
        in_specs=[pl.BlockSpec((2, n_half, hid2), lambda b: (0, b, 0)),
                  pl.BlockSpec(w2p.shape, lambda b: (0, 0, 0))],
        out_specs=pl.BlockSpec((1, 2, n_half, LANES), lambda b: (b, 0, 0, 0)),
        out_shape=jax.ShapeDtypeStruct((n_seq, 2, n_half, LANES), F32),
        compiler_params=_cparams("parallel"),
    )(part, w2p)


def _cmp_paged_kernel(*refs, n_pages, n_item, page):
    page_refs = refs[1:1 + n_item * n_pages]
    perm_ref, w1_ref, w2_ref, o_ref, tok = refs[1 + n_item * n_pages:]
    half_per_page = page // CMP_STRIDE
    n = n_pages * half_per_page
    half = HKV * CMP_HID
    for s in range(2):
        for i, pr in enumerate(page_refs):
            xt = _dot_nt(perm_ref[...], pr[0, s * LANES:(s + 1) * LANES, :])
            tok[:, i * half_per_page:(i + 1) * half_per_page, :] = xt.reshape(CMP_STRIDE, half_per_page, LANES)
        acc = None
        for r in range(0, CMP_STRIDE, 2):
            part = _dot(jnp.concatenate([tok[r], tok[r + 1]], axis=1), w1_ref[s, r // 2])
            acc = part if acc is None else acc + part
        hids = []
        for it in range(n_item):
            p = acc[it * n:(it + 1) * n]
            hids.append(_gelu_tanh(p[:, :half] + pltpu.roll(p[:, half:], n - 1, 0)))
        out = _dot(jnp.concatenate(hids, axis=0) if n_item > 1 else hids[0], w2_ref[s])
        for it in range(n_item):
            o_ref[it, s] = out[it * n:(it + 1) * n]


def _cmp_paged(cache_t, w1p, w2p, page_table, page_base):
    n_seq, n_pages = page_table.shape
    page = cache_t.shape[2]
    half_per_page = page // CMP_STRIDE
    n_half = n_pages * half_per_page
    n_item = _pick_tile(n_seq, (4, 2, 1))
    tok = np.arange(page)
    perm = np.zeros((page, page), np.float32)
    perm[(tok % CMP_STRIDE) * half_per_page + tok // CMP_STRIDE, tok] = 1.0
    page_spec = lambda it, p: pl.BlockSpec((1, 2 * LANES, page),
                                           lambda b, pt: (page_base + pt[b * n_item + it, p], 0, 0))
    return pl.pallas_call(
        functools.partial(_cmp_paged_kernel, n_pages=n_pages, n_item=n_item, page=page),
        grid_spec=pltpu.PrefetchScalarGridSpec(
            num_scalar_prefetch=1, grid=(n_seq // n_item,),
            in_specs=[page_spec(it, p) for it in range(n_item) for p in range(n_pages)]
            + [pl.BlockSpec((page, page), lambda b, pt: (0, 0)),
               pl.BlockSpec(w1p.shape, lambda b, pt: (0, 0, 0, 0)),
               pl.BlockSpec(w2p.shape, lambda b, pt: (0, 0, 0))],
            out_specs=pl.BlockSpec((n_item, 2, n_half, LANES), lambda b, pt: (b, 0, 0, 0)),
            scratch_shapes=[pltpu.VMEM((CMP_STRIDE, n_item * n_half, LANES), F32)]),
        out_shape=jax.ShapeDtypeStruct((n_seq, 2, n_half, LANES), F32),
        compiler_params=_cparams("parallel"),
    )(page_table, *([cache_t] * (n_item * n_pages)), jnp.asarray(perm, dtype=BF16), w1p, w2p)


def _cmp_branch(q, ck, cv, bias_ref, ocmp_ref):
    nq = q.shape[0]
    qpad = jnp.concatenate([_padded_queries(q, g) for g in range(HKV)], axis=0)
    bias = bias_ref[...].reshape(HQ * nq, -1)
    s = _dot_nt(qpad, ck) + bias
    e = jnp.exp(s - jnp.max(s, axis=1, keepdims=True))
    p = jnp.where(bias > 0.5 * NEG, e / jnp.sum(e, axis=1, keepdims=True), 0.0)
    _store_all_heads(ocmp_ref, _dot(p, cv), nq)
    psums = []
    for g in range(HKV):
        heads = [p[(g * REP + r) * nq:(g * REP + r + 1) * nq] for r in range(REP)]
        psums.append(functools.reduce(lambda a, b: a + b, heads))
    return psums


def _select_blocks(imp, qpos, blk_axis, n_blocks):
    blk = lax.broadcasted_iota(jnp.int32, imp.shape, blk_axis)
    cur = (qpos // SLC_LEN) == blk
    avail = blk * SLC_LEN <= qpos
    imp = jnp.where(cur, -NEG, jnp.where(avail, imp, NEG))
    cnt = jnp.zeros(imp.shape, F32)
    for j in range(n_blocks):
        other = lax.slice_in_dim(imp, j, j + 1, axis=blk_axis)
        cnt = cnt + jnp.where(blk > j, jnp.where(other >= imp, 1.0, 0.0), jnp.where(other > imp, 1.0, 0.0))
    return jnp.where((cnt < N_SEL) & (imp > 0.5 * NEG), 1.0, 0.0)


def _padded_queries(q, g, dtype=BF16):
    nq = q.shape[0]
    zero = jnp.zeros((nq, HD), F32)
    parts = []
    for r in range(REP):
        h = g * REP + r
        piece = q[:, h * HD:(h + 1) * HD]
        parts.append(jnp.concatenate([piece, zero] if g == 0 else [zero, piece], axis=1))
    return jnp.concatenate(parts, axis=0).astype(dtype)


def _store_all_heads(o_ref, o, nq):
    for h in range(HQ):
        g = h // REP
        o_ref[:, h * HD:(h + 1) * HD] = o[h * nq:(h + 1) * nq, g * HD:(g + 1) * HD]


def _nsa_cmp_prompt_kernel(q_ref, ckv_ref, bias_ref, ovt_ref, ocmp_ref, sel_ref):
    qb = pl.program_id(1)
    q = q_ref[...] * (HD ** -0.5)
    ck, cv = ckv_ref[0, 0], ckv_ref[0, 1]
    qpos = qb * QT + lax.broadcasted_iota(jnp.int32, (1, QT), 1)
    for g, psum in enumerate(_cmp_branch(q, ck, cv, bias_ref, ocmp_ref)):
        imp_t = _dot_nt_f32(ovt_ref[...], psum)
        sel_ref[:, g * NSP:(g + 1) * NSP] = _select_blocks(imp_t, qpos, 0, NSP).T


def _nsa_cmp_prompt(zq, ckv, bias_cmp, ov_t, n_seq):
    n_tok = zq.shape[0]
    nqb = n_tok // n_seq // QT
    ncp = ckv.shape[2]
    tok = lambda b, i: (b * nqb + i, 0)
    return pl.pallas_call(
        _nsa_cmp_prompt_kernel,
        grid=(n_seq, nqb),
        in_specs=[pl.BlockSpec((QT, D_A), tok),
                  pl.BlockSpec((1, 2, ncp, LANES), lambda b, i: (b, 0, 0, 0)),
                  pl.BlockSpec((HQ, QT, ncp), lambda b, i: (0, i, 0)),
                  pl.BlockSpec(ov_t.shape, lambda b, i: (0, 0))],
        out_specs=[pl.BlockSpec((QT, D_A), tok), pl.BlockSpec((QT, HKV * NSP), tok)],
        out_shape=[jax.ShapeDtypeStruct((n_tok, D_A), F32), jax.ShapeDtypeStruct((n_tok, HKV * NSP), F32)],
        compiler_params=_cparams("parallel", "parallel"),
    )(zq, ckv, bias_cmp, ov_t)


def _flash_prompt_kernel(*refs, window, n_delta, kt_len):
    if window:
        q_ref, kt_ref, vt_ref, bias_ref, o_ref, s_scr, p_scr, m_scr, l_scr, a_scr, acc_scr = refs
    else:
        q_ref, kt_ref, vt_ref, bias_ref, sel_ref, o_ref, s_scr, p_scr, m_scr, l_scr, a_scr, acc_scr = refs
    g = pl.program_id(0)
    qb = pl.program_id(2)
    q = q_ref[...] * (HD ** -0.5 * LOG2E)
    if window:
        parts = [q[:, r * HD:(r + 1) * HD] for r in range(REP)]
    else:
        sel = sel_ref[...]
        unpicked = (1.0 - jnp.where(g == 0, sel[:, :NSP], sel[:, NSP:])) * NEG
        parts = [jnp.concatenate([q[:, r * HD:(r + 1) * HD], unpicked], axis=1) for r in range(REP)]
    qrows = jnp.concatenate(parts, axis=0).astype(BF16)
    hi = (qb * QT) // kt_len + 1
    lo = jnp.maximum(qb * QT - (WINDOW - 1), 0) // kt_len if window else 0

    m_scr[...] = jnp.full(m_scr.shape, NEG, F32)
    l_scr[...] = jnp.zeros(l_scr.shape, F32)
    acc_scr[...] = jnp.zeros(acc_scr.shape, F32)

    def body(kt, carry):
        k0 = pl.multiple_of(kt * kt_len, kt_len)
        delta = jnp.minimum(qb - (kt_len // QT) * kt, n_delta - 1)
        k_t = kt_ref[0, :, pl.ds(k0, kt_len)]
        if not window:
            blk = lax.broadcasted_iota(jnp.int32, (NSP, kt_len), 0)
            key = lax.broadcasted_iota(jnp.int32, (NSP, kt_len), 1)
            one_hot = jnp.where(blk == (k0 + key) // SLC_LEN, 1.0, 0.0).astype(BF16)
            k_t = jnp.concatenate([k_t, one_hot], axis=0)
        s_scr[...] = jnp.dot(qrows, k_t, preferred_element_type=F32)
        for c in range(REP * QT // ROWS):
            r, qc = divmod(c, QT // ROWS)
            rows = pl.ds(c * ROWS, ROWS)
            s = s_scr[rows, :] + bias_ref[delta, r, pl.ds(qc * ROWS, ROWS), :]
            m_old = m_scr[rows, :]
            m_new = jnp.maximum(m_old, jnp.max(s, axis=1, keepdims=True))
            p = jnp.exp2(s - jnp.concatenate([m_new] * (kt_len // LANES), axis=1))
            alpha = jnp.exp2(m_old - m_new)
            l_scr[rows, :] = alpha * l_scr[rows, :] + jnp.sum(p, axis=1, keepdims=True)
            m_scr[rows, :] = m_new
            a_scr[rows, :] = alpha
            p_scr[rows, :] = p.astype(BF16)
        acc_scr[...] = a_scr[...] * acc_scr[...] + lax.dot_general(
            p_scr[...], vt_ref[0, :, pl.ds(k0, kt_len)], (((1,), (1,)), ((), ())), preferred_element_type=F32)
        return carry

    lax.fori_loop(lo, hi, body, 0)
    o = acc_scr[...] / l_scr[...]
    for r in range(REP):
        o_r = o[r * QT:(r + 1) * QT]
        o_ref[:, r * HD:(r + 1) * HD] = jnp.where(g == 0, o_r[:, :HD], o_r[:, HD:])


def _flash_prompt(zq, kv_t, kslot, vslot, bias, sel, n_seq, *, window):
    n_tok = zq.shape[0]
    t = kv_t.shape[2]
    nqb = t // QT
    kt_len = bias.shape[3]
    in_specs = [pl.BlockSpec((QT, REP * HD), lambda g, b, i: (b * nqb + i, g)),
                pl.BlockSpec((1, HD, t), lambda g, b, i: (b, HKV * kslot + g, 0)),
                pl.BlockSpec((1, LANES, t), lambda g, b, i: (b, vslot, 0)),
                pl.BlockSpec((bias.shape[0], REP, QT, kt_len), lambda g, b, i: (0, g, 0, 0))]
    args = [zq, kv_t, kv_t, bias]
    rows = REP * QT
    scratch = [pltpu.VMEM((rows, kt_len), F32), pltpu.VMEM((rows, kt_len), BF16), pltpu.VMEM((rows, LANES), F32),
               pltpu.VMEM((rows, LANES), F32), pltpu.VMEM((rows, LANES), F32), pltpu.VMEM((rows, LANES), F32)]
    if not window:
        in_specs.append(pl.BlockSpec((QT, HKV * NSP), lambda g, b, i: (b * nqb + i, 0)))
        args.append(sel)
    return pl.pallas_call(
        functools.partial(_flash_prompt_kernel, window=window, n_delta=bias.shape[0], kt_len=kt_len),
        grid=(HKV, n_seq, nqb),
        in_specs=in_specs,
        out_specs=pl.BlockSpec((QT, REP * HD), lambda g, b, i: (b * nqb + i, g)),
        out_shape=jax.ShapeDtypeStruct((n_tok, D_A), F32),
        scratch_shapes=scratch,
        compiler_params=_cparams("parallel", "parallel", "parallel"),
    )(*args)


def _nsa_sample_kernel(*refs, n_pages, nq, n_new, past, n_item, n_carried):
    n_in = 1 + n_item * n_pages
    all_pages = refs[1:n_in]
    q_ref, kvn_ref, wn_ref, wbuf_ref, ckv_ref, bcmp_ref, bslc_ref, bwin_ref, ov_ref, exp_ref = refs[n_in:n_in + 10]
    ocmp_ref, oslc_ref, owin_ref, wout_ref = refs[n_in + 10 + n_carried:n_in + 14 + n_carried]
    scratch = refs[n_in + 14 + n_carried:]
    for it in range(n_item):
        _nsa_sample_item(all_pages[it * n_pages:(it + 1) * n_pages], q_ref.at[it], kvn_ref.at[it], wn_ref.at[it],
                         wbuf_ref.at[it], ckv_ref.at[it], bcmp_ref, bslc_ref, bwin_ref, ov_ref, exp_ref,
                         ocmp_ref.at[it], oslc_ref.at[it], owin_ref.at[it], wout_ref.at[it],
                         scratch[2 * it], scratch[2 * it + 1], nq=nq, n_new=n_new, past=past)


def _nsa_sample_item(page_refs, q_ref, kvn_ref, wn_ref, wbuf_ref, ckv_ref, bcmp_ref, bslc_ref, bwin_ref, ov_ref,
                     exp_ref, ocmp_ref, oslc_ref, owin_ref, wout_ref, new_kv, new_w, *, nq, n_new, past):
    new_kv[nq:, :] = jnp.zeros((LANES - nq, 4 * LANES), F32)
    new_kv[:nq, :] = kvn_ref[...]
    new_w[nq:, :] = jnp.zeros((LANES - nq, 2 * LANES), F32)
    new_w[:nq, :] = wn_ref[...]
    win_buf = wbuf_ref.shape[1]
    shifted = pltpu.roll(wbuf_ref[...], win_buf - n_new, 1)
    new_t = jnp.concatenate([new_w[:, :LANES].T, new_w[:, LANES:].T], axis=0)
    new_t = pltpu.roll(new_t, LANES - n_new, 1)
    tail = lax.broadcasted_iota(jnp.int32, (2 * LANES, LANES), 1) >= LANES - n_new
    wout_ref[:, :win_buf - LANES] = shifted[:, :win_buf - LANES]
    wout_ref[:, win_buf - LANES:] = jnp.where(tail, new_t, shifted[:, win_buf - LANES:])

    q = q_ref[...] * (HD ** -0.5)
    ck, cv = ckv_ref[0], ckv_ref[1]
    qpos = past + lax.broadcasted_iota(jnp.int32, (nq, 1), 0)

    def attend(qpad, old_k, old_v, new_k, new_v, bias):
        s = jnp.concatenate([jnp.dot(qpad, k_t.astype(BF16), preferred_element_type=F32) for k_t in old_k]
                            + [_dot_nt(qpad, new_k)], axis=1) + bias
        e = jnp.exp(s - jnp.max(s, axis=1, keepdims=True))
        o, c0 = _dot(e[:, s.shape[1] - LANES:], new_v), 0
        for v_t in old_v:
            o = o + _dot_nt(e[:, c0:c0 + v_t.shape[1]], v_t)
            c0 += v_t.shape[1]
        return o / jnp.sum(e, axis=1, keepdims=True)

    picked = []
    n_blocks = -(-(past + nq) // SLC_LEN)
    for psum in _cmp_branch(q, ck, cv, bcmp_ref, ocmp_ref):
        sel = _select_blocks(_dot_f32(psum, ov_ref[...]), qpos, 1, n_blocks)
        picked += [jnp.dot(sel.astype(BF16), exp_ref[...], preferred_element_type=F32)] * REP
    qpad = jnp.concatenate([_padded_queries(q, g) for g in range(HKV)], axis=0)
    bslc = jnp.where(jnp.concatenate(picked, axis=0) > 0.5, bslc_ref[...].reshape(HQ * nq, -1), NEG)
    o = attend(qpad, [pr[0, 2 * LANES:3 * LANES, :] for pr in page_refs],
               [pr[0, 3 * LANES:4 * LANES, :] for pr in page_refs],
               new_kv[:, 2 * LANES:3 * LANES], new_kv[:, 3 * LANES:4 * LANES], bslc)
    _store_all_heads(oslc_ref, o, nq)
    o = attend(qpad, [wbuf_ref[:LANES, :]], [wbuf_ref[LANES:, :]], new_w[:, :LANES], new_w[:, LANES:],
               bwin_ref[...].reshape(HQ * nq, -1))
    _store_all_heads(owin_ref, o, nq)


def _nsa_sample(zq, kvn, wn, n_new, cache_t, page_base, page_table, win_t, win_base, ckv,
                bias_cmp, bias_slc, bias_win, ov, expand, carried):
    n_seq, nq, _ = zq.shape
    n_pages = page_table.shape[1]
    page = cache_t.shape[2]
    past = n_pages * page
    win_buf = win_t.shape[2]
    ncp = ckv.shape[2]
    n_item = _pick_tile(n_seq, (2, 1))
    seq3 = lambda b, pt: (b, 0, 0)
    const = lambda nd: (lambda b, pt: (0,) * nd)
    page_spec = lambda it, p: pl.BlockSpec((1, 4 * LANES, page),
                                           lambda b, pt: (page_base + pt[b * n_item + it, p], 0, 0))
    in_specs = [page_spec(it, p) for it in range(n_item) for p in range(n_pages)] + [
        pl.BlockSpec((n_item, nq, D_A), seq3),
        pl.BlockSpec((n_item, nq, 4 * LANES), seq3),
        pl.BlockSpec((n_item, nq, 2 * LANES), seq3),
        pl.BlockSpec((n_item, 2 * LANES, win_buf), lambda b, pt: (win_base // n_item + b, 0, 0)),
        pl.BlockSpec((n_item, 2, ncp, LANES), lambda b, pt: (b, 0, 0, 0)),
        pl.BlockSpec(bias_cmp.shape, const(3)),
        pl.BlockSpec(bias_slc.shape, const(3)),
        pl.BlockSpec(bias_win.shape, const(3)),
        pl.BlockSpec(ov.shape, const(2)),
        pl.BlockSpec(expand.shape, const(2))]
    out = pl.BlockSpec((n_item, nq, D_A), seq3)
    wout = pl.BlockSpec((n_item, 2 * LANES, win_buf), lambda b, pt: (win_base // n_item + b, 0, 0))
    c_args, c_specs, n_carried = _carry_args(carried)
    n_in = 1 + n_item * n_pages + 10
    return pl.pallas_call(
        functools.partial(_nsa_sample_kernel, n_pages=n_pages, nq=nq, n_new=n_new, past=past, n_item=n_item,
                          n_carried=n_carried),
        grid_spec=pltpu.PrefetchScalarGridSpec(
            num_scalar_prefetch=1, grid=(n_seq // n_item,), in_specs=in_specs + c_specs,
            out_specs=[out, out, out, wout],
            scratch_shapes=[pltpu.VMEM((LANES, 4 * LANES), F32), pltpu.VMEM((LANES, 2 * LANES), F32)] * n_item),
        out_shape=[jax.ShapeDtypeStruct((n_seq, nq, D_A), F32)] * 3 + [jax.ShapeDtypeStruct(win_t.shape, F32)],
        input_output_aliases={n_in: 3} if n_carried else {},
        compiler_params=_cparams("parallel"),
    )(page_table, *([cache_t] * (n_item * n_pages)), zq, kvn, wn, win_t, ckv, bias_cmp, bias_slc, bias_win, ov, expand,
      *c_args)


def _merge_kernel(x_ref, mout_ref, ocmp_ref, oslc_ref, owin_ref, zg_ref, zgm_ref, gexp_ref, wb_ref, wo_ref,
                  lg_ref, lb_ref, o_ref, *, alpha):
    d = x_ref.shape[1]
    gate = jax.nn.sigmoid(zg_ref[...])
    terms, rest = [], gate
    for _ in range(3):
        terms.append(rest.astype(BF16))
        rest = rest - terms[-1].astype(F32)
    spread = jnp.dot(jnp.concatenate(terms, axis=1), gexp_ref[...], preferred_element_type=F32)
    a = None
    for br, src in enumerate((ocmp_ref, oslc_ref, owin_ref)):
        term = spread[:, br * D_A:(br + 1) * D_A] * src[...]
        a = term if a is None else a + term
    u = (jax.nn.sigmoid(zgm_ref[:, :d]) * _dot(mout_ref[...], wb_ref[0])
         + jax.nn.sigmoid(zgm_ref[:, d:]) * _dot(a, wb_ref[1]))
    y = _dot(u, wo_ref[...])
    o_ref[...] = _layer_norm(alpha * x_ref[...] + y, lg_ref[...], lb_ref[...])


def _merge(x, mout, ocmp, oslc, owin, zg, zgm, gexp, wb, wo, lg, lb, alpha):
    n_tok, d = x.shape
    tm = _pick_tile(n_tok, (256, 128, 64, 32, 16, 8))
    row = lambda i: (i, 0)
    c2 = lambda i: (0, 0)
    c3 = lambda i: (0, 0, 0)
    return pl.pallas_call(
        functools.partial(_merge_kernel, alpha=alpha),
        grid=(n_tok // tm,),
        in_specs=[pl.BlockSpec((tm, d), row), pl.BlockSpec((tm, D_M), row), pl.BlockSpec((tm, D_A), row),
                  pl.BlockSpec((tm, D_A), row), pl.BlockSpec((tm, D_A), row), pl.BlockSpec((tm, LANES), row),
                  pl.BlockSpec((tm, 2 * d), row), pl.BlockSpec(gexp.shape, c2), pl.BlockSpec(wb.shape, c3),
                  pl.BlockSpec(wo.shape, c2), pl.BlockSpec((1, d), c2), pl.BlockSpec((1, d), c2)],
        out_specs=pl.BlockSpec((tm, d), row),
        out_shape=jax.ShapeDtypeStruct((n_tok, d), F32),
        compiler_params=_cparams("parallel"),
    )(x, mout, ocmp, oslc, owin, zg, zgm, gexp, wb, wo, lg.reshape(1, d), lb.reshape(1, d))


def _route(aff, sel):
    def top2_sum(a, b, c, d):
        x, x2, y, y2 = jnp.maximum(a, b), jnp.minimum(a, b), jnp.maximum(c, d), jnp.minimum(c, d)
        return jnp.maximum(x, y) + jnp.maximum(jnp.minimum(x, y), jnp.maximum(x2, y2))

    gsum = [top2_sum(*sel[EXP_PER_GROUP * gi:EXP_PER_GROUP * (gi + 1)]) for gi in range(N_GROUPS)]
    gmax = functools.reduce(jnp.maximum, gsum)
    chosen, taken = [], None
    for gi in range(N_GROUPS):
        is_best = gsum[gi] == gmax if taken is None else (gsum[gi] == gmax) & jnp.logical_not(taken)
        taken = is_best if taken is None else taken | is_best
        members = range(EXP_PER_GROUP * gi, EXP_PER_GROUP * (gi + 1))
        for e in members:
            ahead = None
            for e2 in members:
                if e2 == e:
                    continue
                before = (sel[e2] >= sel[e]) if e2 < e else (sel[e2] > sel[e])
                cnt = jnp.where(before, 1.0, 0.0)
                ahead = cnt if ahead is None else ahead + cnt
            chosen.append(is_best & (ahead < TOP_K))
    picked = [jnp.where(c, a, 0.0) for c, a in zip(chosen, aff)]
    total = functools.reduce(lambda a, b: a + b, picked)
    return [p / total for p in picked]


def _moe_kernel(x_ref, wr_ref, rb_ref, wg_ref, wu_ref, wd_ref, lg_ref, lb_ref, o_ref, xb, comb, acc, *, alpha):
    step = pl.program_id(1)
    tm = x_ref.shape[0]
    lane = lax.broadcasted_iota(jnp.int32, (tm, LANES), 1)

    @pl.when(step == 0)
    def _():
        x16 = x_ref[...].astype(BF16)
        xb[...] = x16
        aff_t = jax.nn.sigmoid(_dot_nt(wr_ref[...], x16))
        sel_t = aff_t + rb_ref[...]
        weights = _route([aff_t[i:i + 1, :] for i in range(N_EXP)], [sel_t[i:i + 1, :] for i in range(N_EXP)])
        expert = lax.broadcasted_iota(jnp.int32, (N_EXP, tm), 0)
        comb_t = jnp.zeros((N_EXP, tm), F32)
        for i, w in enumerate(weights):
            comb_t = jnp.where(expert == i, w, comb_t)
        comb[...] = jnp.concatenate([comb_t, jnp.zeros((LANES - N_EXP, tm), F32)], axis=0).T
        acc[...] = jnp.zeros_like(acc)

    x16 = xb[...]
    hs = []
    for j in range(EXP_PER_STEP):
        w_e = jnp.sum(jnp.where(lane == EXP_PER_STEP * step + j, comb[...], 0.0), axis=1, keepdims=True)
        hg = jnp.dot(x16, wg_ref[0, j], preferred_element_type=F32)
        hu = jnp.dot(x16, wu_ref[0, j], preferred_element_type=F32)
        hs.append((hg * jax.nn.sigmoid(hg) * hu * w_e).astype(BF16))
    d = acc.shape[1]
    acc[...] += jnp.dot(jnp.concatenate(hs, axis=1), wd_ref[0].reshape(EXP_PER_STEP * D_EXP, d),
                        preferred_element_type=F32)

    @pl.when(step == pl.num_programs(1) - 1)
    def _():
        o_ref[...] = _layer_norm(alpha * x_ref[...] + acc[...], lg_ref[...], lb_ref[...])


def _moe(x, wr, rb, wg, wu, wd, layer, lg, lb, alpha):
    n_tok, d = x.shape
    tm = _pick_tile(n_tok, (1024, 512, 256, 128, 64, 32, 16, 8))
    row = lambda i, e: (i, 0)
    c2 = lambda i, e: (0, 0)
    wsel = lambda i, e: (layer, e, 0, 0)
    return pl.pallas_call(
        functools.partial(_moe_kernel, alpha=alpha),
        grid=(n_tok // tm, N_EXP // EXP_PER_STEP),
        in_specs=[pl.BlockSpec((tm, d), row), pl.BlockSpec(wr.shape, c2), pl.BlockSpec(rb.shape, c2),
                  pl.BlockSpec((1, EXP_PER_STEP, d, D_EXP), wsel), pl.BlockSpec((1, EXP_PER_STEP, d, D_EXP), wsel),
                  pl.BlockSpec((1, EXP_PER_STEP, D_EXP, d), wsel), pl.BlockSpec((1, d), c2), pl.BlockSpec((1, d), c2)],
        out_specs=pl.BlockSpec((tm, d), row),
        out_shape=jax.ShapeDtypeStruct((n_tok, d), F32),
        scratch_shapes=[pltpu.VMEM((tm, d), BF16), pltpu.VMEM((tm, LANES), F32), pltpu.VMEM((tm, d), F32)],
        compiler_params=_cparams("parallel", "arbitrary"),
    )(x, wr, rb, wg, wu, wd, lg.reshape(1, d), lb.reshape(1, d))


def _bucket_np(dist):
    n = np.maximum(dist, 0)
    exact = N_BUCKETS // 2
    nf = np.maximum(n, 1).astype(np.float32)
    large = exact + (np.log(nf / np.float32(exact)) / np.float32(math.log(REL_MAX_DIST / exact))
                     * np.float32(N_BUCKETS - exact)).astype(np.int32)
    return np.where(n < exact, n, np.minimum(large, N_BUCKETS - 1)).astype(np.int32)


def _bias_table(rel_bias, dist, valid):
    tab = rel_bias.astype(F32)[jnp.asarray(_bucket_np(dist))]
    tab = jnp.where(jnp.asarray(valid)[..., None], tab, NEG)
    return jnp.moveaxis(tab, -1, 0)


def _bias_tiles_kernel(rb_ref, o_ref, *, tile_step, key_stride, key_off, hi_valid, upper, lead):
    t = pl.program_id(0)
    shape = o_ref.shape[-2:]
    dist = (tile_step * t - key_off + lax.broadcasted_iota(jnp.int32, shape, 0)
            - key_stride * lax.broadcasted_iota(jnp.int32, shape, 1))
    valid = dist >= 0 if hi_valid is None else (dist >= 0) & (dist < hi_valid)
    acc = [jnp.full(shape, rb_ref[N_BUCKETS - 1, h], F32) for h in range(HQ)]
    for b in range(N_BUCKETS - 2, -1, -1):
        below = dist < upper[b]
        acc = [jnp.where(below, rb_ref[b, h], a) for h, a in enumerate(acc)]
    for h in range(HQ):
        if lead:
            o_ref[0, h] = jnp.where(valid, acc[h], NEG)
        else:
            o_ref[h] = jnp.where(valid, acc[h], NEG)


def _bias_tiles(rel_bias, n_tiles, n_keys, *, tile_step, key_stride, key_off, hi_valid, lead, scale=1.0):
    buckets = _bucket_np(np.arange(8 * REL_MAX_DIST))
    upper = tuple(int(np.searchsorted(buckets, b, side='right')) for b in range(N_BUCKETS - 1))
    if lead:
        out_spec = pl.BlockSpec((1, HQ, QT, n_keys), lambda t: (t, 0, 0, 0))
        out_shape = jax.ShapeDtypeStruct((n_tiles, HQ, QT, n_keys), F32)
    else:
        out_spec = pl.BlockSpec((HQ, QT, n_keys), lambda t: (0, t, 0))
        out_shape = jax.ShapeDtypeStruct((HQ, n_tiles * QT, n_keys), F32)
    return pl.pallas_call(
        functools.partial(_bias_tiles_kernel, tile_step=tile_step, key_stride=key_stride, key_off=key_off,
                          hi_valid=hi_valid, upper=upper, lead=lead),
        grid=(n_tiles,),
        in_specs=[pl.BlockSpec(memory_space=pltpu.SMEM)],
        out_specs=out_spec, out_shape=out_shape,
        compiler_params=_cparams("parallel"),
    )(rel_bias.astype(F32) * scale)


def _overlap_np(ncp, n_cmp, n_slc):
    c0 = np.arange(ncp)[:, None] * CMP_STRIDE
    j0 = np.arange(NSP)[None, :] * SLC_LEN
    ov = (c0 < j0 + SLC_LEN) & (c0 + CMP_LEN > j0)
    ov &= (np.arange(ncp)[:, None] < n_cmp) & (np.arange(NSP)[None, :] < n_slc)
    return ov.astype(np.float32)


def kernel(x_prompt, x_sample, cache_kv, state_win_kv, state_mlstm_C, state_mlstm_n, state_mlstm_m, page_table,
           w_in, b_in, w_cmp1, w_cmp2, mh_norm_g, w_branch, w_out, ln1_g, ln1_b, ln2_g, ln2_b,
           w_router, router_bias, w_gate_e, w_up_e, w_down_e, rel_bias):
    B, T, D = x_prompt.shape
    DB, TS, _ = x_sample.shape
    depth, n_pool, page = cache_kv.shape[:3]
    n_pages = page_table.shape[1]
    past = n_pages * page
    win_buf = state_win_kv.shape[2]
    alpha = (2 * depth) ** 0.25
    assert T % KT == 0 and T // SLC_LEN <= NSP and page % CMP_STRIDE == 0 and past % SLC_LEN == 0
    assert win_buf == min(WINDOW, past) and T >= win_buf

    off = np.cumsum((0, D_M, D_M, D_M, D_M, H_M, H_M, D_A, 6 * HKV * HD, 3 * HQ, 2 * D)).tolist()
    seg = lambda a, i, j: a[..., off[i]:off[j]]
    small = lambda a: jnp.concatenate(
        [seg(a, 4, 6), seg(a, 8, 9), jnp.zeros(a.shape[:-1] + (LANES - 2 * H_M - 3 * HQ,), a.dtype)], axis=-1)
    kv_cmp, kv_mid = off[7] + 2 * HKV * HD, off[7] + 4 * HKV * HD
    groups_s = lambda a: (seg(a, 0, 4), small(a), seg(a, 6, 7), a[..., off[7]:kv_mid], a[..., kv_mid:off[8]], seg(a, 9, 10))
    groups_p = lambda a: (seg(a, 0, 4), small(a), seg(a, 6, 7), a[..., off[7]:kv_cmp], seg(a, 9, 10))
    ws_s = [w.astype(BF16) for w in groups_s(w_in)]
    bs_s = [b[:, None, :] for b in groups_s(b_in)]
    ws_p = [w.astype(BF16) for w in groups_p(w_in)]
    bs_p = [b[:, None, :] for b in groups_p(b_in)]
    w_in_t = jnp.transpose(w_in, (0, 2, 1))
    wt_p = w_in_t[:, off[7]:off[8]].astype(BF16)
    bt_p = b_in[:, off[7]:off[8], None]
    wb16, wo16 = w_branch.astype(BF16), w_out.astype(BF16)
    wg16, wu16, wd16 = w_gate_e.astype(BF16), w_up_e.astype(BF16), w_down_e.astype(BF16)
    wr_pad = jnp.transpose(w_router).astype(BF16)
    rb_pad = router_bias.astype(F32).reshape(N_EXP, 1)
    eye = jnp.eye(HKV, dtype=F32)
    w1 = w_cmp1.reshape(depth, 2, 2, CMP_STRIDE, HD, CMP_HID)
    w1p = jnp.einsum('lshrdf,gG->lsrgdhGf', w1, eye).reshape(depth, 2, CMP_STRIDE, HKV * HD, 2 * HKV * CMP_HID)
    w1p = w1p.astype(BF16)
    w2p = jnp.einsum('lsfd,gG->lsgfGd', w_cmp2, eye).reshape(depth, 2, HKV * CMP_HID, HKV * HD).astype(BF16)
    gexp = np.zeros((LANES, 3 * D_A), np.float32)
    for br in range(3):
        for h in range(HQ):
            gexp[2 * H_M + br * HQ + h, br * D_A + h * HD:br * D_A + (h + 1) * HD] = 1.0
    gexp = jnp.asarray(np.tile(gexp, (3, 1)), dtype=BF16)

    ncp_p, n_cmp_p = T // CMP_STRIDE, (T - CMP_LEN) // CMP_STRIDE + 1
    assert n_cmp_p * CMP_STRIDE + CMP_LEN - 1 > T - 1
    bias_cmp_p = _bias_tiles(rel_bias, T // QT, ncp_p, tile_step=QT, key_stride=CMP_STRIDE, key_off=CMP_LEN - 1,
                             hi_valid=None, lead=False)
    ov_t_p = jnp.asarray(_overlap_np(ncp_p, n_cmp_p, -(-T // SLC_LEN)).T)
    n_far = -(-(KT - 1 + REL_MAX_DIST) // QT)
    bias_slc_p = _bias_tiles(rel_bias, n_far + 1, KT, tile_step=QT, key_stride=1, key_off=0, hi_valid=None, lead=True,
                             scale=LOG2E)
    bias_win_p = _bias_tiles(rel_bias, (WINDOW + KT_WIN) // QT, KT_WIN, tile_step=QT, key_stride=1, key_off=0,
                             hi_valid=WINDOW, lead=True, scale=LOG2E)

    ncp_s = past // CMP_STRIDE
    n_cmp_s = (past + TS - CMP_LEN) // CMP_STRIDE + 1
    assert n_cmp_s <= ncp_s and -(-(past + TS) // SLC_LEN) <= NSP and TS <= NQS
    qs = past + np.arange(NQS)[:, None]
    d_cmp_s = qs - (np.arange(ncp_s)[None, :] * CMP_STRIDE + CMP_LEN - 1)
    bias_cmp_s = _bias_table(rel_bias, d_cmp_s, (d_cmp_s >= 0) & (np.arange(ncp_s)[None, :] < n_cmp_s))
    ov_s = jnp.asarray(_overlap_np(ncp_s, n_cmp_s, -(-(past + TS) // SLC_LEN)))
    key_s = np.arange(past + LANES)[None, :]
    d_slc_s = qs - key_s
    bias_slc_s = _bias_table(rel_bias, d_slc_s, (d_slc_s >= 0) & (key_s < past + TS))
    expand_s = jnp.asarray((np.arange(NSP)[:, None] == key_s // SLC_LEN).astype(np.float32)).astype(BF16)
    idx_w = np.arange(win_buf + LANES)[None, :]
    d_win_s = qs - (past - win_buf + idx_w)
    bias_win_s = _bias_table(rel_bias, d_win_s, (d_win_s >= 0) & (d_win_s < WINDOW) & (idx_w < win_buf + TS))

    to_feature_major = lambda a: jnp.transpose(a, (0, 1, 3, 4, 5, 2))
    cache_t = to_feature_major(cache_kv).reshape(depth * n_pool, 4 * LANES, page)
    win_t = to_feature_major(state_win_kv).reshape(depth * DB, 2 * LANES, win_buf)
    from_feature_major = lambda a, slots: jnp.transpose(
        a.reshape(a.shape[:2] + (slots, HKV, HD, a.shape[-1])), (0, 1, 5, 2, 3, 4))

    LP = _pick_tile(T, (256, 128, 64))
    LS = NQS
    zeros_state = (jnp.zeros((B, H_M, DH_M, DH_M), F32), jnp.zeros((B, H_M, DH_M), F32), jnp.zeros((B, H_M), F32))
    state_c_all = state_mlstm_C.reshape(depth * DB, H_M, DH_M, DH_M)

    xp = x_prompt.reshape(B * T, D)
    xs = x_sample.reshape(DB * TS, D)
    outs = [[] for _ in range(10)]
    kv_all = jnp.zeros((depth * B, 4 * LANES, T), F32)
    win_all = jnp.zeros(win_t.shape, F32)
    c_all = jnp.zeros(state_c_all.shape, F32)
    for l in range(depth):
        zm, zg, zq, zc, zgm, kv_all, zw_t, z16_t = _inproj(
            xp, [w[l] for w in ws_p], [b[l] for b in bs_p], wt_p[l], bt_p[l], t_split=4 * LANES, n_seq=B,
            layer=l, depth=depth, carried=kv_all)
        mout, c_p, n_p, m_p = _mlstm(zm, zg, mh_norm_g[l], *zeros_state, n_seq=B, L=LP, l_valid=LP)
        part = _cmp_partial(zc.reshape(B * T // CMP_STRIDE, CMP_STRIDE * 2 * LANES), w1p[l])
        ckv = _cmp_finish_prompt(part, w2p[l], B)
        ocmp, sel = _nsa_cmp_prompt(zq, ckv, bias_cmp_p, ov_t_p, B)
        oslc = _flash_prompt(zq, z16_t, 2, 3, bias_slc_p, sel, B, window=False)
        owin = _flash_prompt(zq, z16_t, 4, 5, bias_win_p, None, B, window=True)
        x1 = _merge(xp, mout, ocmp, oslc, owin, zg, zgm, gexp, wb16[l], wo16[l], ln1_g[l], ln1_b[l], alpha)
        xp = _moe(x1, wr_pad, rb_pad, wg16, wu16, wd16, l, ln2_g[l], ln2_b[l], alpha)
        outs[1].append(zw_t[:, :, T - win_buf:])
        outs[2].append(c_p)
        outs[3].append(n_p)
        outs[4].append(m_p)
        zm, zg, zq, zkv, zw, zgm = _inproj(xs, [w[l] for w in ws_s], [b[l] for b in bs_s])
        padt = lambda a: jnp.pad(a.reshape(DB, TS, -1), ((0, 0), (0, LS - TS), (0, 0))).reshape(DB * LS, -1)
        mout, c_all, n_s, m_s = _mlstm(padt(zm), padt(zg), mh_norm_g[l], state_c_all, state_mlstm_n[l],
                                       state_mlstm_m[l], n_seq=DB, L=LS, l_valid=TS, c_base=l * DB, c_layers=depth,
                                       carried=c_all)
        mout = mout.reshape(DB, LS, D_M)[:, :TS].reshape(DB * TS, D_M)
        ckv = _cmp_paged(cache_t, w1p[l].reshape(2, CMP_STRIDE // 2, 2 * LANES, -1), w2p[l], page_table, l * n_pool)
        padq = lambda a: jnp.pad(a.reshape(DB, TS, -1), ((0, 0), (0, NQS - TS), (0, 0)))
        *o3, win_all = _nsa_sample(padq(zq), padq(zkv), padq(zw), TS, cache_t, l * n_pool, page_table, win_t, l * DB,
                                   ckv, bias_cmp_s, bias_slc_s, bias_win_s, ov_s, expand_s, win_all)
        ocmp, oslc, owin = [o[:, :TS].reshape(DB * TS, D_A) for o in o3]
        x1 = _merge(xs, mout, ocmp, oslc, owin, zg, zgm, gexp, wb16[l], wo16[l], ln1_g[l], ln1_b[l], alpha)
        xs = _moe(x1, wr_pad, rb_pad, wg16, wu16, wd16, l, ln2_g[l], ln2_b[l], alpha)
        outs[5].append(zkv.reshape(DB, TS, 4, HKV, HD))
        outs[8].append(n_s)
        outs[9].append(m_s)
    stacked = [jnp.stack(o) if o else None for o in outs]
    stacked[0] = from_feature_major(kv_all.reshape(depth, B, 4 * LANES, T), 4)
    stacked[1] = from_feature_major(stacked[1], 2)
    stacked[6] = from_feature_major(win_all.reshape(depth, DB, 2 * LANES, win_buf), 2)
    stacked[7] = c_all.reshape(depth, DB, H_M, DH_M, DH_M)
    return (xp.reshape(B, T, D), xs.reshape(DB, TS, D)) + tuple(stacked)
```

```python
import functools
import math

import numpy as np
import jax
import jax.numpy as jnp
from jax import lax
from jax.experimental import pallas as pl
from jax.experimental.pallas import tpu as pltpu

F32 = jnp.float32
BF16 = jnp.bfloat16
HIGHEST = lax.Precision.HIGHEST

H_M, DH_M = 4, 128
D_M = H_M * DH_M
HQ, HKV, HD = 8, 2, 64
REP = HQ // HKV
D_A = HQ * HD
CMP_LEN, CMP_STRIDE, CMP_HID = 32, 16, 256
SLC_LEN, N_SEL, WINDOW = 64, 16, 512
N_BUCKETS, REL_MAX_DIST = 32, 128
N_EXP, N_GROUPS, TOP_K, D_EXP = 16, 4, 2, 256
EXP_PER_GROUP = N_EXP // N_GROUPS
EXP_PER_STEP = 2
LN_EPS = 1e-5
NEG = -1e30

LANES = 128
QT = 128
KT = 512
LOG2E = math.log2(math.e)
ROWS = 32
NSP = 64
NQS = 8
VMEM_LIMIT = 56 * 1024 * 1024


def _cparams(*sem):
    return pltpu.CompilerParams(dimension_semantics=sem, vmem_limit_bytes=VMEM_LIMIT)


def _dot(a, b):
    return jnp.dot(a.astype(BF16), b.astype(BF16), preferred_element_type=F32)


def _dot_nt(a, b):
    return lax.dot_general(a.astype(BF16), b.astype(BF16), (((1,), (1,)), ((), ())), preferred_element_type=F32)


def _dot_tn(a, b):
    return lax.dot_general(a.astype(BF16), b.astype(BF16), (((0,), (0,)), ((), ())), preferred_element_type=F32)


def _dot_f32(a, b):
    return jnp.dot(a, b, precision=HIGHEST, preferred_element_type=F32)


def _dot_nt_f32(a, b):
    return lax.dot_general(a, b, (((1,), (1,)), ((), ())), precision=HIGHEST, preferred_element_type=F32)


def _pick_tile(n, cands):
    for c in cands:
        if n % c == 0:
            return c
    raise ValueError(f"no tile for {n}")


def _layer_norm(y, g, b):
    mu = jnp.mean(y, axis=-1, keepdims=True)
    yc = y - mu
    var = jnp.mean(yc * yc, axis=-1, keepdims=True)
    return yc * lax.rsqrt(var + LN_EPS) * g + b


def _inproj_kernel(x_ref, *refs, n_plain, t_split, n_carried):
    n_w = n_plain + (1 if t_split else 0)
    w_refs, b_refs, o_refs = refs[:n_plain], refs[n_w:n_w + n_plain], refs[2 * n_w + n_carried:]
    x = x_ref[...].astype(BF16)
    for w_ref, b_ref, o_ref in zip(w_refs, b_refs, o_refs):
        o_ref[...] = jnp.dot(x, w_ref[...], preferred_element_type=F32) + b_ref[...]
    if t_split:
        wt_ref, bt_ref = refs[n_plain], refs[n_w + n_plain]
        zt = lax.dot_general(wt_ref[...], x, (((1,), (1,)), ((), ())), preferred_element_type=F32) + bt_ref[...]
        lo_ref, hi_ref, all16_ref = o_refs[n_plain:]
        lo_ref[0] = zt[:t_split]
        hi_ref[0] = zt[t_split:]
        all16_ref[0] = zt.astype(BF16)


def _carry_args(carried):
    if carried is None:
        return [], [], 0
    return [carried], [pl.BlockSpec(memory_space=pl.ANY)], 1


def _inproj(x, ws, bs, wt=None, bt=None, t_split=0, n_seq=1, layer=0, depth=1, carried=None):
    n_tok, d = x.shape
    t = n_tok // n_seq
    tm = _pick_tile(t, (256, 128, 64, 32, 16, 8))
    nt = t // tm
    full = lambda i: (0, 0)
    row = lambda i: (i, 0)
    tspec = lambda n, base: pl.BlockSpec((1, n, tm), lambda i: (base + i // nt, 0, i % nt))
    extra_w = [] if wt is None else [wt]
    extra_b = [] if wt is None else [bt]
    n_t = 0 if wt is None else wt.shape[0]
    t_specs = [] if wt is None else [tspec(t_split, layer * n_seq), tspec(n_t - t_split, 0), tspec(n_t, 0)]
    t_shapes = [] if wt is None else [jax.ShapeDtypeStruct((depth * n_seq, t_split, t), F32),
                                      jax.ShapeDtypeStruct((n_seq, n_t - t_split, t), F32),
                                      jax.ShapeDtypeStruct((n_seq, n_t, t), BF16)]
    c_args, c_specs, n_carried = _carry_args(carried)
    n_in = 1 + 2 * (len(ws) + len(extra_w))
    return pl.pallas_call(
        functools.partial(_inproj_kernel, n_plain=len(ws), t_split=t_split if wt is not None else 0,
                          n_carried=n_carried),
        grid=(n_tok // tm,),
        in_specs=[pl.BlockSpec((tm, d), row)]
        + [pl.BlockSpec(w.shape, full) for w in (*ws, *extra_w)]
        + [pl.BlockSpec(b.shape, full) for b in (*bs, *extra_b)] + c_specs,
        out_specs=[pl.BlockSpec((tm, w.shape[1]), row) for w in ws] + t_specs,
        out_shape=[jax.ShapeDtypeStruct((n_tok, w.shape[1]), F32) for w in ws] + t_shapes,
        input_output_aliases={n_in: len(ws)} if n_carried else {},
        compiler_params=_cparams("parallel"),
    )(x, *ws, *extra_w, *bs, *extra_b, *c_args)


def _log_sigmoid(x):
    return jnp.minimum(x, 0.0) - jnp.log(1.0 + jnp.exp(-jnp.abs(x)))


def _mlstm_kernel(q_ref, k_ref, v_ref, o_ref, g_ref, ng_ref, c0_ref, n0_ref, m0_ref, *rest, L, l_valid):
    mout_ref, c_ref, n_ref, m_ref, cs, ns, ms = rest[-7:]
    c = pl.program_id(1)

    @pl.when(c == 0)
    def _():
        cs[...] = c0_ref[0]
        ns[...] = n0_ref[0]
        ms[...] = m0_ref[0]

    g = g_ref[...]
    row = lax.broadcasted_iota(jnp.int32, (L, L), 0)
    col = lax.broadcasted_iota(jnp.int32, (L, L), 1)
    causal = row >= col
    fcum = _dot_f32(causal.astype(F32), _log_sigmoid(g))
    lane = lax.broadcasted_iota(jnp.int32, (L, LANES), 1)
    y = jnp.where(lane < H_M, g - pltpu.roll(fcum, LANES - H_M, 1), fcum)
    yt = y.T
    rowv = lax.broadcasted_iota(jnp.int32, (L, LANES), 0)
    r = l_valid - 1
    wide = (lambda x: jnp.concatenate([x] * (L // LANES), axis=1)) if L >= LANES else (lambda x: x[:, :L])
    for h in range(H_M):
        hs = slice(h * DH_M, (h + 1) * DH_M)
        f_col = jnp.broadcast_to(fcum[:, H_M + h:H_M + h + 1], (L, LANES))
        a_row = yt[h:h + 1, :]
        i_col = jnp.broadcast_to(g[:, h:h + 1], (L, LANES))
        m0 = ms[h]
        n0 = ns[h]
        c0 = cs[h]
        dm = jnp.where(causal, wide(f_col) + a_row, NEG)
        b = f_col + m0
        mrow = jnp.maximum(b, jnp.max(dm, axis=1, keepdims=True))
        w = jnp.exp(dm - wide(mrow))
        dec = jnp.exp(b - mrow)
        q = q_ref[:, hs]
        k = k_ref[:, hs] * (DH_M ** -0.5)
        v = v_ref[:, hs]
        s = _dot_nt(q, k) * w
        num = dec * _dot_nt(q, c0) + _dot(s, v)
        den = dec * jnp.sum(q * n0, axis=1, keepdims=True) + jnp.sum(s, axis=1, keepdims=True)
        hh = num / jnp.maximum(jnp.abs(den), jnp.exp(-mrow))
        mu = jnp.mean(hh, axis=1, keepdims=True)
        hc = hh - mu
        var = jnp.mean(hc * hc, axis=1, keepdims=True)
        hn = hc * lax.rsqrt(var + LN_EPS) * ng_ref[:, hs]
        mout_ref[:, hs] = (hn * jax.nn.sigmoid(o_ref[:, hs])).astype(mout_ref.dtype)
        f_r = f_col[r:r + 1, :]
        m_r = mrow[r:r + 1, :]
        w_last = jnp.where(rowv <= r, jnp.exp(f_r - f_col + i_col - m_r), 0.0)
        d_last = dec[r:r + 1, :]
        cs[h] = d_last * c0 + _dot_tn(v * w_last, k)
        ns[h] = d_last * n0 + jnp.sum(k * w_last, axis=0, keepdims=True)
        ms[h] = m_r

    @pl.when(c == pl.num_programs(1) - 1)
    def _():
        c_ref[0] = cs[...]
        n_ref[0] = ns[...]
        m_ref[0] = ms[...]


def _mlstm(zm, zg, norm_g, c0, n0, m0, *, n_seq, L, l_valid, c_base=0, c_layers=1, carried=None):
    n_tok = zm.shape[0]
    t = n_tok // n_seq
    nc = t // L
    n0 = n0.reshape(n_seq, H_M, 1, DH_M)
    m0 = jnp.broadcast_to(m0.reshape(n_seq, H_M, 1, 1), (n_seq, H_M, 1, LANES))
    colblk = lambda j: (lambda b, c: (b * nc + c, j))
    st4 = lambda b, c: (b, 0, 0, 0)
    c_spec = pl.BlockSpec((1, H_M, DH_M, DH_M), lambda b, c: (c_base + b, 0, 0, 0))
    c_args, c_specs, n_carried = _carry_args(carried)
    mout, c_new, n_new, m_new = pl.pallas_call(
        functools.partial(_mlstm_kernel, L=L, l_valid=l_valid),
        grid=(n_seq, nc),
        in_specs=[pl.BlockSpec((L, D_M), colblk(0)), pl.BlockSpec((L, D_M), colblk(1)),
                  pl.BlockSpec((L, D_M), colblk(2)), pl.BlockSpec((L, D_M), colblk(3)),
                  pl.BlockSpec((L, LANES), colblk(0)),
                  pl.BlockSpec((1, D_M), lambda b, c: (0, 0)),
                  c_spec,
                  pl.BlockSpec((1, H_M, 1, DH_M), st4),
                  pl.BlockSpec((1, H_M, 1, LANES), st4)] + c_specs,
        out_specs=[pl.BlockSpec((L, D_M), colblk(0)),
                   c_spec,
                   pl.BlockSpec((1, H_M, 1, DH_M), st4),
                   pl.BlockSpec((1, H_M, 1, LANES), st4)],
        out_shape=[jax.ShapeDtypeStruct((n_tok, D_M), BF16),
                   jax.ShapeDtypeStruct((c_layers * n_seq, H_M, DH_M, DH_M), F32),
                   jax.ShapeDtypeStruct((n_seq, H_M, 1, DH_M), F32),
                   jax.ShapeDtypeStruct((n_seq, H_M, 1, LANES), F32)],
        scratch_shapes=[pltpu.VMEM((H_M, DH_M, DH_M), F32), pltpu.VMEM((H_M, 1, DH_M), F32),
                        pltpu.VMEM((H_M, 1, LANES), F32)],
        input_output_aliases={9: 1} if n_carried else {},
        compiler_params=_cparams("parallel", "arbitrary"),
    )(zm, zm, zm, zm, zg, norm_g.reshape(1, D_M), c0, n0, m0, *c_args)
    return mout, c_new, n_new.reshape(n_seq, H_M, DH_M), m_new[:, :, 0, 0]


def _cmp_partial_kernel(x_ref, w_ref, o_ref):
    @pl.when(pl.program_id(2) == 0)
    def _():
        o_ref[...] = jnp.zeros_like(o_ref)

    o_ref[0] += _dot(x_ref[...], w_ref[0, 0])


def _cmp_partial(rows16, w1p):
    n_half = rows16.shape[0]
    tm = _pick_tile(n_half, (2048, 1024, 512, 256, 128, 64, 32, 16, 8))
    hid2 = 2 * HKV * CMP_HID
    return pl.pallas_call(
        _cmp_partial_kernel,
        grid=(2, n_half // tm, CMP_STRIDE),
        in_specs=[pl.BlockSpec((tm, LANES), lambda s, i, r: (i, 2 * r + s)),
                  pl.BlockSpec((1, 1, LANES, hid2), lambda s, i, r: (s, r, 0, 0))],
        out_specs=pl.BlockSpec((1, tm, hid2), lambda s, i, r: (s, i, 0)),
        out_shape=jax.ShapeDtypeStruct((2, n_half, hid2), F32),
        compiler_params=_cparams("parallel", "parallel", "arbitrary"),
    )(rows16, w1p)


def _gelu_tanh(x):
    return 0.5 * x * (1.0 + jnp.tanh(math.sqrt(2.0 / math.pi) * (x + 0.044715 * (x * x * x))))


def _cmp_finish_kernel(p_ref, w2_ref, o_ref):
    half = HKV * CMP_HID
    for s in range(2):
        p = p_ref[s]
        n = p.shape[0]
        hid = _gelu_tanh(p[:, :half] + pltpu.roll(p[:, half:], n - 1, 0))
        o_ref[0, s] = _dot(hid, w2_ref[s])


def _cmp_finish_prompt(part, w2p, n_seq):
    n_half = part.shape[1] // n_seq
    hid2 = part.shape[2]
    return pl.pallas_call(
        _cmp_finish_kernel,
        grid=(n_seq,),
        in_specs=[pl.BlockSpec((2, n_half, hid2), lambda b: (0, b, 0)),
                  pl.BlockSpec(w2p.shape, lambda b: (0, 0, 0))],
        out_specs=pl.BlockSpec((1, 2, n_half, LANES), lambda b: (b, 0, 0, 0)),
        out_shape=jax.ShapeDtypeStruct((n_seq, 2, n_half, LANES), F32),
        compiler_params=_cparams("parallel"),
    )(part, w2p)


def _cmp_paged_kernel(*refs, n_pages, n_item, page):
    page_refs = refs[1:1 + n_item * n_pages]
    perm_ref, w1_ref, w2_ref, o_ref, tok = refs[1 + n_item * n_pages:]
    half_per_page = page // CMP_STRIDE
    n = n_pages * half_per_page
    half = HKV * CMP_HID
    for s in range(2):
        for i, pr in enumerate(page_refs):
            xt = _dot_nt(perm_ref[...], pr[0, s * LANES:(s + 1) * LANES, :])
            tok[:, i * half_per_page:(i + 1) * half_per_page, :] = xt.reshape(CMP_STRIDE, half_per_page, LANES)
        acc = None
        for r in range(0, CMP_STRIDE, 2):
            part = _dot(jnp.concatenate([tok[r], tok[r + 1]], axis=1), w1_ref[s, r // 2])
            acc = part if acc is None else acc + part
        hids = []
        for it in range(n_item):
            p = acc[it * n:(it + 1) * n]
            hids.append(_gelu_tanh(p[:, :half] + pltpu.roll(p[:, half:], n - 1, 0)))
        out = _dot(jnp.concatenate(hids, axis=0) if n_item > 1 else hids[0], w2_ref[s])
        for it in range(n_item):
            o_ref[it, s] = out[it * n:(it + 1) * n]


def _cmp_paged(cache_t, w1p, w2p, page_table, page_base):
    n_seq, n_pages = page_table.shape
    page = cache_t.shape[2]
    half_per_page = page // CMP_STRIDE
    n_half = n_pages * half_per_page
    n_item = _pick_tile(n_seq, (4, 2, 1))
    tok = np.arange(page)
    perm = np.zeros((page, page), np.float32)
    perm[(tok % CMP_STRIDE) * half_per_page + tok // CMP_STRIDE, tok] = 1.0
    page_spec = lambda it, p: pl.BlockSpec((1, 2 * LANES, page),
                                           lambda b, pt: (page_base + pt[b * n_item + it, p], 0, 0))
    return pl.pallas_call(
        functools.partial(_cmp_paged_kernel, n_pages=n_pages, n_item=n_item, page=page),
        grid_spec=pltpu.PrefetchScalarGridSpec(
            num_scalar_prefetch=1, grid=(n_seq // n_item,),
            in_specs=[page_spec(it, p) for it in range(n_item) for p in range(n_pages)]
            + [pl.BlockSpec((page, page), lambda b, pt: (0, 0)),
               pl.BlockSpec(w1p.shape, lambda b, pt: (0, 0, 0, 0)),
               pl.BlockSpec(w2p.shape, lambda b, pt: (0, 0, 0))],
            out_specs=pl.BlockSpec((n_item, 2, n_half, LANES), lambda b, pt: (b, 0, 0, 0)),
            scratch_shapes=[pltpu.VMEM((CMP_STRIDE, n_item * n_half, LANES), F32)]),
        out_shape=jax.ShapeDtypeStruct((n_seq, 2, n_half, LANES), F32),
        compiler_params=_cparams("parallel"),
    )(page_table, *([cache_t] * (n_item * n_pages)), jnp.asarray(perm, dtype=BF16), w1p, w2p)


def _cmp_branch(q, ck, cv, bias_ref, ocmp_ref):
    nq = q.shape[0]
    qpad = jnp.concatenate([_padded_queries(q, g) for g in range(HKV)], axis=0)
    bias = bias_ref[...].reshape(HQ * nq, -1)
    s = _dot_nt(qpad, ck) + bias
    e = jnp.exp(s - jnp.max(s, axis=1, keepdims=True))
    p = jnp.where(bias > 0.5 * NEG, e / jnp.sum(e, axis=1, keepdims=True), 0.0)
    _store_all_heads(ocmp_ref, _dot(p, cv), nq)
    psums = []
    for g in range(HKV):
        heads = [p[(g * REP + r) * nq:(g * REP + r + 1) * nq] for r in range(REP)]
        psums.append(functools.reduce(lambda a, b: a + b, heads))
    return psums


def _select_blocks(imp, qpos, blk_axis, n_blocks):
    blk = lax.broadcasted_iota(jnp.int32, imp.shape, blk_axis)
    cur = (qpos // SLC_LEN) == blk
    avail = blk * SLC_LEN <= qpos
    imp = jnp.where(cur, -NEG, jnp.where(avail, imp, NEG))
    cnt = jnp.zeros(imp.shape, F32)
    for j in range(n_blocks):
        other = lax.slice_in_dim(imp, j, j + 1, axis=blk_axis)
        cnt = cnt + jnp.where(blk > j, jnp.where(other >= imp, 1.0, 0.0), jnp.where(other > imp, 1.0, 0.0))
    return jnp.where((cnt < N_SEL) & (imp > 0.5 * NEG), 1.0, 0.0)


def _padded_queries(q, g, dtype=BF16):
    nq = q.shape[0]
    zero = jnp.zeros((nq, HD), F32)
    parts = []
    for r in range(REP):
        h = g * REP + r
        piece = q[:, h * HD:(h + 1) * HD]
        parts.append(jnp.concatenate([piece, zero] if g == 0 else [zero, piece], axis=1))
    return jnp.concatenate(parts, axis=0).astype(dtype)


def _store_all_heads(o_ref, o, nq):
    for h in range(HQ):
        g = h // REP
        o_ref[:, h * HD:(h + 1) * HD] = o[h * nq:(h + 1) * nq, g * HD:(g + 1) * HD]


def _nsa_cmp_prompt_kernel(q_ref, ckv_ref, bias_ref, ovt_ref, ocmp_ref, sel_ref):
    qb = pl.program_id(1)
    q = q_ref[...] * (HD ** -0.5)
    ck, cv = ckv_ref[0, 0], ckv_ref[0, 1]
    qpos = qb * QT + lax.broadcasted_iota(jnp.int32, (1, QT), 1)
    for g, psum in enumerate(_cmp_branch(q, ck, cv, bias_ref, ocmp_ref)):
        imp_t = _dot_nt_f32(ovt_ref[...], psum)
        sel_ref[:, g * NSP:(g + 1) * NSP] = _select_blocks(imp_t, qpos, 0, NSP).T


def _nsa_cmp_prompt(zq, ckv, bias_cmp, ov_t, n_seq):
    n_tok = zq.shape[0]
    nqb = n_tok // n_seq // QT
    ncp = ckv.shape[2]
    tok = lambda b, i: (b * nqb + i, 0)
    return pl.pallas_call(
        _nsa_cmp_prompt_kernel,
        grid=(n_seq, nqb),
        in_specs=[pl.BlockSpec((QT, D_A), tok),
                  pl.BlockSpec((1, 2, ncp, LANES), lambda b, i: (b, 0, 0, 0)),
                  pl.BlockSpec((HQ, QT, ncp), lambda b, i: (0, i, 0)),
                  pl.BlockSpec(ov_t.shape, lambda b, i: (0, 0))],
        out_specs=[pl.BlockSpec((QT, D_A), tok), pl.BlockSpec((QT, HKV * NSP), tok)],
        out_shape=[jax.ShapeDtypeStruct((n_tok, D_A), F32), jax.ShapeDtypeStruct((n_tok, HKV * NSP), F32)],
        compiler_params=_cparams("parallel", "parallel"),
    )(zq, ckv, bias_cmp, ov_t)


def _flash_prompt_kernel(q_ref, kt_ref, vt_ref, bias_ref, sel_ref, o_ref, s_scr, p_scr, m_scr, l_scr, a_scr, acc_scr,
                         *, n_delta, kt_len):
    g = pl.program_id(0)
    qb = pl.program_id(2)
    q = q_ref[...] * (HD ** -0.5 * LOG2E)
    sel = sel_ref[...]
    unpicked = (1.0 - jnp.where(g == 0, sel[:, :NSP], sel[:, NSP:])) * NEG
    parts = [jnp.concatenate([q[:, r * HD:(r + 1) * HD], unpicked], axis=1) for r in range(REP)]
    qrows = jnp.concatenate(parts, axis=0).astype(BF16)
    hi = (qb * QT) // kt_len + 1

    m_scr[...] = jnp.full(m_scr.shape, NEG, F32)
    l_scr[...] = jnp.zeros(l_scr.shape, F32)
    acc_scr[...] = jnp.zeros(acc_scr.shape, F32)

    def body(kt, carry):
        k0 = pl.multiple_of(kt * kt_len, kt_len)
        delta = jnp.minimum(qb - (kt_len // QT) * kt, n_delta - 1)
        blk = lax.broadcasted_iota(jnp.int32, (NSP, kt_len), 0)
        key = lax.broadcasted_iota(jnp.int32, (NSP, kt_len), 1)
        one_hot = jnp.where(blk == (k0 + key) // SLC_LEN, 1.0, 0.0).astype(BF16)
        k_t = jnp.concatenate([kt_ref[0, :, pl.ds(k0, kt_len)], one_hot], axis=0)
        s_scr[...] = jnp.dot(qrows, k_t, preferred_element_type=F32)
        for c in range(REP * QT // ROWS):
            r, qc = divmod(c, QT // ROWS)
            rows = pl.ds(c * ROWS, ROWS)
            s = s_scr[rows, :] + bias_ref[delta, r, pl.ds(qc * ROWS, ROWS), :]
            m_old = m_scr[rows, :]
            m_new = jnp.maximum(m_old, jnp.max(s, axis=1, keepdims=True))
            p = jnp.exp2(s - jnp.concatenate([m_new] * (kt_len // LANES), axis=1))
            alpha = jnp.exp2(m_old - m_new)
            l_scr[rows, :] = alpha * l_scr[rows, :] + jnp.sum(p, axis=1, keepdims=True)
            m_scr[rows, :] = m_new
            a_scr[rows, :] = alpha
            p_scr[rows, :] = p.astype(BF16)
        acc_scr[...] = a_scr[...] * acc_scr[...] + lax.dot_general(
            p_scr[...], vt_ref[0, :, pl.ds(k0, kt_len)], (((1,), (1,)), ((), ())), preferred_element_type=F32)
        return carry

    lax.fori_loop(0, hi, body, 0)
    o = acc_scr[...] / l_scr[...]
    for r in range(REP):
        o_r = o[r * QT:(r + 1) * QT]
        o_ref[:, r * HD:(r + 1) * HD] = jnp.where(g == 0, o_r[:, :HD], o_r[:, HD:])


def _flash_prompt(zq, kv_t, kslot, vslot, bias, sel, n_seq):
    n_tok = zq.shape[0]
    t = kv_t.shape[2]
    nqb = t // QT
    kt_len = bias.shape[3]
    rows = REP * QT
    return pl.pallas_call(
        functools.partial(_flash_prompt_kernel, n_delta=bias.shape[0], kt_len=kt_len),
        grid=(HKV, n_seq, nqb),
        in_specs=[pl.BlockSpec((QT, REP * HD), lambda g, b, i: (b * nqb + i, g)),
                  pl.BlockSpec((1, HD, t), lambda g, b, i: (b, HKV * kslot + g, 0)),
                  pl.BlockSpec((1, LANES, t), lambda g, b, i: (b, vslot, 0)),
                  pl.BlockSpec((bias.shape[0], REP, QT, kt_len), lambda g, b, i: (0, g, 0, 0)),
                  pl.BlockSpec((QT, HKV * NSP), lambda g, b, i: (b * nqb + i, 0))],
        out_specs=pl.BlockSpec((QT, REP * HD), lambda g, b, i: (b * nqb + i, g)),
        out_shape=jax.ShapeDtypeStruct((n_tok, D_A), F32),
        scratch_shapes=[pltpu.VMEM((rows, kt_len), F32), pltpu.VMEM((rows, kt_len), BF16),
                        pltpu.VMEM((rows, LANES), F32), pltpu.VMEM((rows, LANES), F32),
                        pltpu.VMEM((rows, LANES), F32), pltpu.VMEM((rows, LANES), F32)],
        compiler_params=_cparams("parallel", "parallel", "parallel"),
    )(zq, kv_t, kv_t, bias, sel)


def _window_prompt_kernel(q_ref, kt_ref, vt_ref, bias_a_ref, bias_b_ref, o_ref, s_a, s_b, p_a, p_b, l_scr):
    g = pl.program_id(0)
    qb = pl.program_id(2)
    n_back = WINDOW // QT
    q = q_ref[...] * (HD ** -0.5 * LOG2E)
    qrows = jnp.concatenate([q[:, r * HD:(r + 1) * HD] for r in range(REP)], axis=0).astype(BF16)
    a0 = pl.multiple_of(jnp.maximum(qb - n_back, 0) * QT, QT)
    b0 = pl.multiple_of(qb * QT, QT)
    delta_a = jnp.minimum(qb, n_back)
    tile_b = jnp.where(qb >= n_back, 0, 1)
    s_a[...] = jnp.dot(qrows, kt_ref[0, :, pl.ds(a0, WINDOW)], preferred_element_type=F32)
    s_b[...] = jnp.dot(qrows, kt_ref[0, :, pl.ds(b0, QT)], preferred_element_type=F32)
    for c in range(REP * QT // ROWS):
        r, qc = divmod(c, QT // ROWS)
        rows, qrows_c = pl.ds(c * ROWS, ROWS), pl.ds(qc * ROWS, ROWS)
        sa = s_a[rows, :] + bias_a_ref[delta_a, r, qrows_c, :]
        sb = s_b[rows, :] + bias_b_ref[tile_b, r, qrows_c, :]
        fold = lambda x, y, op: functools.reduce(op, [x[:, i:i + LANES] for i in range(0, WINDOW, LANES)] + [y])
        m = jnp.broadcast_to(jnp.max(fold(sa, sb, jnp.maximum), axis=1, keepdims=True), (ROWS, LANES))
        pa = jnp.exp2(sa - jnp.concatenate([m] * (WINDOW // LANES), axis=1))
        pb = jnp.exp2(sb - m)
        l_scr[rows, :] = jnp.broadcast_to(jnp.sum(fold(pa, pb, jnp.add), axis=1, keepdims=True), (ROWS, LANES))
        p_a[rows, :] = pa.astype(BF16)
        p_b[rows, :] = pb.astype(BF16)
    nt = (((1,), (1,)), ((), ()))
    o = (lax.dot_general(p_a[...], vt_ref[0, :, pl.ds(a0, WINDOW)], nt, preferred_element_type=F32)
         + lax.dot_general(p_b[...], vt_ref[0, :, pl.ds(b0, QT)], nt, preferred_element_type=F32)) / l_scr[...]
    for r in range(REP):
        o_r = o[r * QT:(r + 1) * QT]
        o_ref[:, r * HD:(r + 1) * HD] = jnp.where(g == 0, o_r[:, :HD], o_r[:, HD:])


def _window_prompt(zq, kv_t, kslot, vslot, bias_a, bias_b, n_seq):
    n_tok = zq.shape[0]
    t = kv_t.shape[2]
    nqb = t // QT
    rows = REP * QT
    return pl.pallas_call(
        _window_prompt_kernel,
        grid=(HKV, n_seq, nqb),
        in_specs=[pl.BlockSpec((QT, REP * HD), lambda g, b, i: (b * nqb + i, g)),
                  pl.BlockSpec((1, HD, t), lambda g, b, i: (b, HKV * kslot + g, 0)),
                  pl.BlockSpec((1, LANES, t), lambda g, b, i: (b, vslot, 0)),
                  pl.BlockSpec((bias_a.shape[0], REP, QT, WINDOW), lambda g, b, i: (0, g, 0, 0)),
                  pl.BlockSpec((bias_b.shape[0], REP, QT, QT), lambda g, b, i: (0, g, 0, 0))],
        out_specs=pl.BlockSpec((QT, REP * HD), lambda g, b, i: (b * nqb + i, g)),
        out_shape=jax.ShapeDtypeStruct((n_tok, D_A), F32),
        scratch_shapes=[pltpu.VMEM((rows, WINDOW), F32), pltpu.VMEM((rows, QT), F32), pltpu.VMEM((rows, WINDOW), BF16),
                        pltpu.VMEM((rows, QT), BF16), pltpu.VMEM((rows, LANES), F32)],
        compiler_params=_cparams("parallel", "parallel", "parallel"),
    )(zq, kv_t, kv_t, bias_a, bias_b)


def _nsa_sample_kernel(*refs, n_pages, nq, n_new, past, n_item, n_carried):
    n_in = 1 + n_item * n_pages
    all_pages = refs[1:n_in]
    q_ref, kvn_ref, wn_ref, wbuf_ref, ckv_ref, bcmp_ref, bslc_ref, bwin_ref, ov_ref, exp_ref = refs[n_in:n_in + 10]
    ocmp_ref, oslc_ref, owin_ref, wout_ref = refs[n_in + 10 + n_carried:n_in + 14 + n_carried]
    scratch = refs[n_in + 14 + n_carried:]
    for it in range(n_item):
        _nsa_sample_item(all_pages[it * n_pages:(it + 1) * n_pages], q_ref.at[it], kvn_ref.at[it], wn_ref.at[it],
                         wbuf_ref.at[it], ckv_ref.at[it], bcmp_ref, bslc_ref, bwin_ref, ov_ref, exp_ref,
                         ocmp_ref.at[it], oslc_ref.at[it], owin_ref.at[it], wout_ref.at[it],
                         scratch[2 * it], scratch[2 * it + 1], nq=nq, n_new=n_new, past=past)


def _nsa_sample_item(page_refs, q_ref, kvn_ref, wn_ref, wbuf_ref, ckv_ref, bcmp_ref, bslc_ref, bwin_ref, ov_ref,
                     exp_ref, ocmp_ref, oslc_ref, owin_ref, wout_ref, new_kv, new_w, *, nq, n_new, past):
    new_kv[nq:, :] = jnp.zeros((LANES - nq, 4 * LANES), F32)
    new_kv[:nq, :] = kvn_ref[...]
    new_w[nq:, :] = jnp.zeros((LANES - nq, 2 * LANES), F32)
    new_w[:nq, :] = wn_ref[...]
    win_buf = wbuf_ref.shape[1]
    shifted = pltpu.roll(wbuf_ref[...], win_buf - n_new, 1)
    new_t = jnp.concatenate([new_w[:, :LANES].T, new_w[:, LANES:].T], axis=0)
    new_t = pltpu.roll(new_t, LANES - n_new, 1)
    tail = lax.broadcasted_iota(jnp.int32, (2 * LANES, LANES), 1) >= LANES - n_new
    wout_ref[:, :win_buf - LANES] = shifted[:, :win_buf - LANES]
    wout_ref[:, win_buf - LANES:] = jnp.where(tail, new_t, shifted[:, win_buf - LANES:])

    q = q_ref[...] * (HD ** -0.5)
    ck, cv = ckv_ref[0], ckv_ref[1]
    qpos = past + lax.broadcasted_iota(jnp.int32, (nq, 1), 0)

    def attend(qpad, old_k, old_v, new_k, new_v, bias):
        s = jnp.concatenate([jnp.dot(qpad, k_t.astype(BF16), preferred_element_type=F32) for k_t in old_k]
                            + [_dot_nt(qpad, new_k)], axis=1) + bias
        e = jnp.exp(s - jnp.max(s, axis=1, keepdims=True))
        o, c0 = _dot(e[:, s.shape[1] - LANES:], new_v), 0
        for v_t in old_v:
            o = o + _dot_nt(e[:, c0:c0 + v_t.shape[1]], v_t)
            c0 += v_t.shape[1]
        return o / jnp.sum(e, axis=1, keepdims=True)

    picked = []
    n_blocks = -(-(past + nq) // SLC_LEN)
    for psum in _cmp_branch(q, ck, cv, bcmp_ref, ocmp_ref):
        sel = _select_blocks(_dot_f32(psum, ov_ref[...]), qpos, 1, n_blocks)
        picked += [jnp.dot(sel.astype(BF16), exp_ref[...], preferred_element_type=F32)] * REP
    qpad = jnp.concatenate([_padded_queries(q, g) for g in range(HKV)], axis=0)
    bslc = jnp.where(jnp.concatenate(picked, axis=0) > 0.5, bslc_ref[...].reshape(HQ * nq, -1), NEG)
    o = attend(qpad, [pr[0, 2 * LANES:3 * LANES, :] for pr in page_refs],
               [pr[0, 3 * LANES:4 * LANES, :] for pr in page_refs],
               new_kv[:, 2 * LANES:3 * LANES], new_kv[:, 3 * LANES:4 * LANES], bslc)
    _store_all_heads(oslc_ref, o, nq)
    o = attend(qpad, [wbuf_ref[:LANES, :]], [wbuf_ref[LANES:, :]], new_w[:, :LANES], new_w[:, LANES:],
               bwin_ref[...].reshape(HQ * nq, -1))
    _store_all_heads(owin_ref, o, nq)


def _nsa_sample(zq, kvn, wn, n_new, cache_t, page_base, page_table, win_t, win_base, ckv,
                bias_cmp, bias_slc, bias_win, ov, expand, carried):
    n_seq, nq, _ = zq.shape
    n_pages = page_table.shape[1]
    page = cache_t.shape[2]
    past = n_pages * page
    win_buf = win_t.shape[2]
    ncp = ckv.shape[2]
    n_item = _pick_tile(n_seq, (2, 1))
    seq3 = lambda b, pt: (b, 0, 0)
    const = lambda nd: (lambda b, pt: (0,) * nd)
    page_spec = lambda it, p: pl.BlockSpec((1, 4 * LANES, page),
                                           lambda b, pt: (page_base + pt[b * n_item + it, p], 0, 0))
    in_specs = [page_spec(it, p) for it in range(n_item) for p in range(n_pages)] + [
        pl.BlockSpec((n_item, nq, D_A), seq3),
        pl.BlockSpec((n_item, nq, 4 * LANES), seq3),
        pl.BlockSpec((n_item, nq, 2 * LANES), seq3),
        pl.BlockSpec((n_item, 2 * LANES, win_buf), lambda b, pt: (win_base // n_item + b, 0, 0)),
        pl.BlockSpec((n_item, 2, ncp, LANES), lambda b, pt: (b, 0, 0, 0)),
        pl.BlockSpec(bias_cmp.shape, const(3)),
        pl.BlockSpec(bias_slc.shape, const(3)),
        pl.BlockSpec(bias_win.shape, const(3)),
        pl.BlockSpec(ov.shape, const(2)),
        pl.BlockSpec(expand.shape, const(2))]
    out = pl.BlockSpec((n_item, nq, D_A), seq3)
    wout = pl.BlockSpec((n_item, 2 * LANES, win_buf), lambda b, pt: (win_base // n_item + b, 0, 0))
    c_args, c_specs, n_carried = _carry_args(carried)
    n_in = 1 + n_item * n_pages + 10
    return pl.pallas_call(
        functools.partial(_nsa_sample_kernel, n_pages=n_pages, nq=nq, n_new=n_new, past=past, n_item=n_item,
                          n_carried=n_carried),
        grid_spec=pltpu.PrefetchScalarGridSpec(
            num_scalar_prefetch=1, grid=(n_seq // n_item,), in_specs=in_specs + c_specs,
            out_specs=[out, out, out, wout],
            scratch_shapes=[pltpu.VMEM((LANES, 4 * LANES), F32), pltpu.VMEM((LANES, 2 * LANES), F32)] * n_item),
        out_shape=[jax.ShapeDtypeStruct((n_seq, nq, D_A), F32)] * 3 + [jax.ShapeDtypeStruct(win_t.shape, F32)],
        input_output_aliases={n_in: 3} if n_carried else {},
        compiler_params=_cparams("parallel"),
    )(page_table, *([cache_t] * (n_item * n_pages)), zq, kvn, wn, win_t, ckv, bias_cmp, bias_slc, bias_win, ov, expand,
      *c_args)


def _merge_kernel(x_ref, mout_ref, ocmp_ref, oslc_ref, owin_ref, zg_ref, zgm_ref, gexp_ref, wb_ref, wo_ref,
                  lg_ref, lb_ref, o_ref, *, alpha):
    d = x_ref.shape[1]
    gate = jax.nn.sigmoid(zg_ref[...])
    terms, rest = [], gate
    for _ in range(3):
        terms.append(rest.astype(BF16))
        rest = rest - terms[-1].astype(F32)
    spread = jnp.dot(jnp.concatenate(terms, axis=1), gexp_ref[...], preferred_element_type=F32)
    a = None
    for br, src in enumerate((ocmp_ref, oslc_ref, owin_ref)):
        term = spread[:, br * D_A:(br + 1) * D_A] * src[...]
        a = term if a is None else a + term
    u = (jax.nn.sigmoid(zgm_ref[:, :d]) * _dot(mout_ref[...], wb_ref[0])
         + jax.nn.sigmoid(zgm_ref[:, d:]) * _dot(a, wb_ref[1]))
    y = _dot(u, wo_ref[...])
    o_ref[...] = _layer_norm(alpha * x_ref[...] + y, lg_ref[...], lb_ref[...])


def _merge(x, mout, ocmp, oslc, owin, zg, zgm, gexp, wb, wo, lg, lb, alpha):
    n_tok, d = x.shape
    tm = _pick_tile(n_tok, (256, 128, 64, 32, 16, 8))
    row = lambda i: (i, 0)
    c2 = lambda i: (0, 0)
    c3 = lambda i: (0, 0, 0)
    return pl.pallas_call(
        functools.partial(_merge_kernel, alpha=alpha),
        grid=(n_tok // tm,),
        in_specs=[pl.BlockSpec((tm, d), row), pl.BlockSpec((tm, D_M), row), pl.BlockSpec((tm, D_A), row),
                  pl.BlockSpec((tm, D_A), row), pl.BlockSpec((tm, D_A), row), pl.BlockSpec((tm, LANES), row),
                  pl.BlockSpec((tm, 2 * d), row), pl.BlockSpec(gexp.shape, c2), pl.BlockSpec(wb.shape, c3),
                  pl.BlockSpec(wo.shape, c2), pl.BlockSpec((1, d), c2), pl.BlockSpec((1, d), c2)],
        out_specs=pl.BlockSpec((tm, d), row),
        out_shape=jax.ShapeDtypeStruct((n_tok, d), F32),
        compiler_params=_cparams("parallel"),
    )(x, mout, ocmp, oslc, owin, zg, zgm, gexp, wb, wo, lg.reshape(1, d), lb.reshape(1, d))


def _route(aff, sel):
    def top2_sum(a, b, c, d):
        x, x2, y, y2 = jnp.maximum(a, b), jnp.minimum(a, b), jnp.maximum(c, d), jnp.minimum(c, d)
        return jnp.maximum(x, y) + jnp.maximum(jnp.minimum(x, y), jnp.maximum(x2, y2))

    gsum = [top2_sum(*sel[EXP_PER_GROUP * gi:EXP_PER_GROUP * (gi + 1)]) for gi in range(N_GROUPS)]
    gmax = functools.reduce(jnp.maximum, gsum)
    chosen, taken = [], None
    for gi in range(N_GROUPS):
        is_best = gsum[gi] == gmax if taken is None else (gsum[gi] == gmax) & jnp.logical_not(taken)
        taken = is_best if taken is None else taken | is_best
        members = range(EXP_PER_GROUP * gi, EXP_PER_GROUP * (gi + 1))
        for e in members:
            ahead = None
            for e2 in members:
                if e2 == e:
                    continue
                before = (sel[e2] >= sel[e]) if e2 < e else (sel[e2] > sel[e])
                cnt = jnp.where(before, 1.0, 0.0)
                ahead = cnt if ahead is None else ahead + cnt
            chosen.append(is_best & (ahead < TOP_K))
    picked = [jnp.where(c, a, 0.0) for c, a in zip(chosen, aff)]
    total = functools.reduce(lambda a, b: a + b, picked)
    return [p / total for p in picked]


def _moe_kernel(x_ref, wr_ref, rb_ref, wg_ref, wu_ref, wd_ref, lg_ref, lb_ref, o_ref, xb, comb, acc, *, alpha):
    step = pl.program_id(1)
    tm = x_ref.shape[0]
    lane = lax.broadcasted_iota(jnp.int32, (tm, LANES), 1)

    @pl.when(step == 0)
    def _():
        x16 = x_ref[...].astype(BF16)
        xb[...] = x16
        aff_t = jax.nn.sigmoid(_dot_nt(wr_ref[...], x16))
        sel_t = aff_t + rb_ref[...]
        weights = _route([aff_t[i:i + 1, :] for i in range(N_EXP)], [sel_t[i:i + 1, :] for i in range(N_EXP)])
        expert = lax.broadcasted_iota(jnp.int32, (N_EXP, tm), 0)
        comb_t = jnp.zeros((N_EXP, tm), F32)
        for i, w in enumerate(weights):
            comb_t = jnp.where(expert == i, w, comb_t)
        comb[...] = jnp.concatenate([comb_t, jnp.zeros((LANES - N_EXP, tm), F32)], axis=0).T
        acc[...] = jnp.zeros_like(acc)

    x16 = xb[...]
    hs = []
    for j in range(EXP_PER_STEP):
        w_e = jnp.sum(jnp.where(lane == EXP_PER_STEP * step + j, comb[...], 0.0), axis=1, keepdims=True)
        hg = jnp.dot(x16, wg_ref[0, j], preferred_element_type=F32)
        hu = jnp.dot(x16, wu_ref[0, j], preferred_element_type=F32)
        hs.append((hg * jax.nn.sigmoid(hg) * hu * w_e).astype(BF16))
    d = acc.shape[1]
    acc[...] += jnp.dot(jnp.concatenate(hs, axis=1), wd_ref[0].reshape(EXP_PER_STEP * D_EXP, d),
                        preferred_element_type=F32)

    @pl.when(step == pl.num_programs(1) - 1)
    def _():
        o_ref[...] = _layer_norm(alpha * x_ref[...] + acc[...], lg_ref[...], lb_ref[...])


def _moe(x, wr, rb, wg, wu, wd, layer, lg, lb, alpha):
    n_tok, d = x.shape
    tm = _pick_tile(n_tok, (1024, 512, 256, 128, 64, 32, 16, 8))
    row = lambda i, e: (i, 0)
    c2 = lambda i, e: (0, 0)
    wsel = lambda i, e: (layer, e, 0, 0)
    return pl.pallas_call(
        functools.partial(_moe_kernel, alpha=alpha),
        grid=(n_tok // tm, N_EXP // EXP_PER_STEP),
        in_specs=[pl.BlockSpec((tm, d), row), pl.BlockSpec(wr.shape, c2), pl.BlockSpec(rb.shape, c2),
                  pl.BlockSpec((1, EXP_PER_STEP, d, D_EXP), wsel), pl.BlockSpec((1, EXP_PER_STEP, d, D_EXP), wsel),
                  pl.BlockSpec((1, EXP_PER_STEP, D_EXP, d), wsel), pl.BlockSpec((1, d), c2), pl.BlockSpec((1, d), c2)],
        out_specs=pl.BlockSpec((tm, d), row),
        out_shape=jax.ShapeDtypeStruct((n_tok, d), F32),
        scratch_shapes=[pltpu.VMEM((tm, d), BF16), pltpu.VMEM((tm, LANES), F32), pltpu.VMEM((tm, d), F32)],
        compiler_params=_cparams("parallel", "arbitrary"),
    )(x, wr, rb, wg, wu, wd, lg.reshape(1, d), lb.reshape(1, d))


def _bucket_np(dist):
    n = np.maximum(dist, 0)
    exact = N_BUCKETS // 2
    nf = np.maximum(n, 1).astype(np.float32)
    large = exact + (np.log(nf / np.float32(exact)) / np.float32(math.log(REL_MAX_DIST / exact))
                     * np.float32(N_BUCKETS - exact)).astype(np.int32)
    return np.where(n < exact, n, np.minimum(large, N_BUCKETS - 1)).astype(np.int32)


def _bias_table(rel_bias, dist, valid):
    tab = rel_bias.astype(F32)[jnp.asarray(_bucket_np(dist))]
    tab = jnp.where(jnp.asarray(valid)[..., None], tab, NEG)
    return jnp.moveaxis(tab, -1, 0)


def _bias_tiles_kernel(rb_ref, o_ref, *, tile_step, key_stride, key_off, hi_valid, upper, lead):
    t = pl.program_id(0)
    shape = o_ref.shape[-2:]
    dist = (tile_step * t - key_off + lax.broadcasted_iota(jnp.int32, shape, 0)
            - key_stride * lax.broadcasted_iota(jnp.int32, shape, 1))
    valid = dist >= 0 if hi_valid is None else (dist >= 0) & (dist < hi_valid)
    acc = [jnp.full(shape, rb_ref[N_BUCKETS - 1, h], F32) for h in range(HQ)]
    for b in range(N_BUCKETS - 2, -1, -1):
        below = dist < upper[b]
        acc = [jnp.where(below, rb_ref[b, h], a) for h, a in enumerate(acc)]
    for h in range(HQ):
        if lead:
            o_ref[0, h] = jnp.where(valid, acc[h], NEG)
        else:
            o_ref[h] = jnp.where(valid, acc[h], NEG)


def _bias_tiles(rel_bias, n_tiles, n_keys, *, tile_step, key_stride, key_off, hi_valid, lead, scale=1.0):
    buckets = _bucket_np(np.arange(8 * REL_MAX_DIST))
    upper = tuple(int(np.searchsorted(buckets, b, side='right')) for b in range(N_BUCKETS - 1))
    if lead:
        out_spec = pl.BlockSpec((1, HQ, QT, n_keys), lambda t: (t, 0, 0, 0))
        out_shape = jax.ShapeDtypeStruct((n_tiles, HQ, QT, n_keys), F32)
    else:
        out_spec = pl.BlockSpec((HQ, QT, n_keys), lambda t: (0, t, 0))
        out_shape = jax.ShapeDtypeStruct((HQ, n_tiles * QT, n_keys), F32)
    return pl.pallas_call(
        functools.partial(_bias_tiles_kernel, tile_step=tile_step, key_stride=key_stride, key_off=key_off,
                          hi_valid=hi_valid, upper=upper, lead=lead),
        grid=(n_tiles,),
        in_specs=[pl.BlockSpec(memory_space=pltpu.SMEM)],
        out_specs=out_spec, out_shape=out_shape,
        compiler_params=_cparams("parallel"),
    )(rel_bias.astype(F32) * scale)


def _overlap_np(ncp, n_cmp, n_slc):
    c0 = np.arange(ncp)[:, None] * CMP_STRIDE
    j0 = np.arange(NSP)[None, :] * SLC_LEN
    ov = (c0 < j0 + SLC_LEN) & (c0 + CMP_LEN > j0)
    ov &= (np.arange(ncp)[:, None] < n_cmp) & (np.arange(NSP)[None, :] < n_slc)
    return ov.astype(np.float32)


def kernel(x_prompt, x_sample, cache_kv, state_win_kv, state_mlstm_C, state_mlstm_n, state_mlstm_m, page_table,
           w_in, b_in, w_cmp1, w_cmp2, mh_norm_g, w_branch, w_out, ln1_g, ln1_b, ln2_g, ln2_b,
           w_router, router_bias, w_gate_e, w_up_e, w_down_e, rel_bias):
    B, T, D = x_prompt.shape
    DB, TS, _ = x_sample.shape
    depth, n_pool, page = cache_kv.shape[:3]
    n_pages = page_table.shape[1]
    past = n_pages * page
    win_buf = state_win_kv.shape[2]
    alpha = (2 * depth) ** 0.25
    assert T % KT == 0 and T // SLC_LEN <= NSP and page % CMP_STRIDE == 0 and past % SLC_LEN == 0
    assert win_buf == min(WINDOW, past) and T >= win_buf

    off = np.cumsum((0, D_M, D_M, D_M, D_M, H_M, H_M, D_A, 6 * HKV * HD, 3 * HQ, 2 * D)).tolist()
    seg = lambda a, i, j: a[..., off[i]:off[j]]
    small = lambda a: jnp.concatenate(
        [seg(a, 4, 6), seg(a, 8, 9), jnp.zeros(a.shape[:-1] + (LANES - 2 * H_M - 3 * HQ,), a.dtype)], axis=-1)
    kv_cmp, kv_mid = off[7] + 2 * HKV * HD, off[7] + 4 * HKV * HD
    groups_s = lambda a: (seg(a, 0, 4), small(a), seg(a, 6, 7), a[..., off[7]:kv_mid], a[..., kv_mid:off[8]], seg(a, 9, 10))
    groups_p = lambda a: (seg(a, 0, 4), small(a), seg(a, 6, 7), a[..., off[7]:kv_cmp], seg(a, 9, 10))
    ws_s = [w.astype(BF16) for w in groups_s(w_in)]
    bs_s = [b[:, None, :] for b in groups_s(b_in)]
    ws_p = [w.astype(BF16) for w in groups_p(w_in)]
    bs_p = [b[:, None, :] for b in groups_p(b_in)]
    w_in_t = jnp.transpose(w_in, (0, 2, 1))
    wt_p = w_in_t[:, off[7]:off[8]].astype(BF16)
    bt_p = b_in[:, off[7]:off[8], None]
    wb16, wo16 = w_branch.astype(BF16), w_out.astype(BF16)
    wg16, wu16, wd16 = w_gate_e.astype(BF16), w_up_e.astype(BF16), w_down_e.astype(BF16)
    wr_pad = jnp.transpose(w_router).astype(BF16)
    rb_pad = router_bias.astype(F32).reshape(N_EXP, 1)
    eye = jnp.eye(HKV, dtype=F32)
    w1 = w_cmp1.reshape(depth, 2, 2, CMP_STRIDE, HD, CMP_HID)
    w1p = jnp.einsum('lshrdf,gG->lsrgdhGf', w1, eye).reshape(depth, 2, CMP_STRIDE, HKV * HD, 2 * HKV * CMP_HID)
    w1p = w1p.astype(BF16)
    w2p = jnp.einsum('lsfd,gG->lsgfGd', w_cmp2, eye).reshape(depth, 2, HKV * CMP_HID, HKV * HD).astype(BF16)
    gexp = np.zeros((LANES, 3 * D_A), np.float32)
    for br in range(3):
        for h in range(HQ):
            gexp[2 * H_M + br * HQ + h, br * D_A + h * HD:br * D_A + (h + 1) * HD] = 1.0
    gexp = jnp.asarray(np.tile(gexp, (3, 1)), dtype=BF16)

    ncp_p, n_cmp_p = T // CMP_STRIDE, (T - CMP_LEN) // CMP_STRIDE + 1
    assert n_cmp_p * CMP_STRIDE + CMP_LEN - 1 > T - 1
    bias_cmp_p = _bias_tiles(rel_bias, T // QT, ncp_p, tile_step=QT, key_stride=CMP_STRIDE, key_off=CMP_LEN - 1,
                             hi_valid=None, lead=False)
    ov_t_p = jnp.asarray(_overlap_np(ncp_p, n_cmp_p, -(-T // SLC_LEN)).T)
    n_far = -(-(KT - 1 + REL_MAX_DIST) // QT)
    bias_slc_p = _bias_tiles(rel_bias, n_far + 1, KT, tile_step=QT, key_stride=1, key_off=0, hi_valid=None, lead=True,
                             scale=LOG2E)
    bias_win_a = _bias_tiles(rel_bias, WINDOW // QT + 1, WINDOW, tile_step=QT, key_stride=1, key_off=0,
                             hi_valid=WINDOW, lead=True, scale=LOG2E)
    bias_win_b = _bias_tiles(rel_bias, 2, QT, tile_step=-2 * WINDOW, key_stride=1, key_off=0,
                             hi_valid=WINDOW, lead=True, scale=LOG2E)

    ncp_s = past // CMP_STRIDE
    n_cmp_s = (past + TS - CMP_LEN) // CMP_STRIDE + 1
    assert n_cmp_s <= ncp_s and -(-(past + TS) // SLC_LEN) <= NSP and TS <= NQS
    qs = past + np.arange(NQS)[:, None]
    d_cmp_s = qs - (np.arange(ncp_s)[None, :] * CMP_STRIDE + CMP_LEN - 1)
    bias_cmp_s = _bias_table(rel_bias, d_cmp_s, (d_cmp_s >= 0) & (np.arange(ncp_s)[None, :] < n_cmp_s))
    ov_s = jnp.asarray(_overlap_np(ncp_s, n_cmp_s, -(-(past + TS) // SLC_LEN)))
    key_s = np.arange(past + LANES)[None, :]
    d_slc_s = qs - key_s
    bias_slc_s = _bias_table(rel_bias, d_slc_s, (d_slc_s >= 0) & (key_s < past + TS))
    expand_s = jnp.asarray((np.arange(NSP)[:, None] == key_s // SLC_LEN).astype(np.float32)).astype(BF16)
    idx_w = np.arange(win_buf + LANES)[None, :]
    d_win_s = qs - (past - win_buf + idx_w)
    bias_win_s = _bias_table(rel_bias, d_win_s, (d_win_s >= 0) & (d_win_s < WINDOW) & (idx_w < win_buf + TS))

    to_feature_major = lambda a: jnp.transpose(a, (0, 1, 3, 4, 5, 2))
    cache_t = to_feature_major(cache_kv).reshape(depth * n_pool, 4 * LANES, page)
    win_t = to_feature_major(state_win_kv).reshape(depth * DB, 2 * LANES, win_buf)
    from_feature_major = lambda a, slots: jnp.transpose(
        a.reshape(a.shape[:2] + (slots, HKV, HD, a.shape[-1])), (0, 1, 5, 2, 3, 4))

    LP = _pick_tile(T, (256, 128, 64))
    LS = NQS
    zeros_state = (jnp.zeros((B, H_M, DH_M, DH_M), F32), jnp.zeros((B, H_M, DH_M), F32), jnp.zeros((B, H_M), F32))
    state_c_all = state_mlstm_C.reshape(depth * DB, H_M, DH_M, DH_M)

    xp = x_prompt.reshape(B * T, D)
    xs = x_sample.reshape(DB * TS, D)
    outs = [[] for _ in range(10)]
    kv_all = jnp.zeros((depth * B, 4 * LANES, T), F32)
    win_all = jnp.zeros(win_t.shape, F32)
    c_all = jnp.zeros(state_c_all.shape, F32)
    for l in range(depth):
        zm, zg, zq, zc, zgm, kv_all, zw_t, z16_t = _inproj(
            xp, [w[l] for w in ws_p], [b[l] for b in bs_p], wt_p[l], bt_p[l], t_split=4 * LANES, n_seq=B,
            layer=l, depth=depth, carried=kv_all)
        mout, c_p, n_p, m_p = _mlstm(zm, zg, mh_norm_g[l], *zeros_state, n_seq=B, L=LP, l_valid=LP)
        part = _cmp_partial(zc.reshape(B * T // CMP_STRIDE, CMP_STRIDE * 2 * LANES), w1p[l])
        ckv = _cmp_finish_prompt(part, w2p[l], B)
        ocmp, sel = _nsa_cmp_prompt(zq, ckv, bias_cmp_p, ov_t_p, B)
        oslc = _flash_prompt(zq, z16_t, 2, 3, bias_slc_p, sel, B)
        owin = _window_prompt(zq, z16_t, 4, 5, bias_win_a, bias_win_b, B)
        x1 = _merge(xp, mout, ocmp, oslc, owin, zg, zgm, gexp, wb16[l], wo16[l], ln1_g[l], ln1_b[l], alpha)
        xp = _moe(x1, wr_pad, rb_pad, wg16, wu16, wd16, l, ln2_g[l], ln2_b[l], alpha)
        outs[1].append(zw_t[:, :, T - win_buf:])
        outs[2].append(c_p)
        outs[3].append(n_p)
        outs[4].append(m_p)
        zm, zg, zq, zkv, zw, zgm = _inproj(xs, [w[l] for w in ws_s], [b[l] for b in bs_s])
        padt = lambda a: jnp.pad(a.reshape(DB, TS, -1), ((0, 0), (0, LS - TS), (0, 0))).reshape(DB * LS, -1)
        mout, c_all, n_s, m_s = _mlstm(padt(zm), padt(zg), mh_norm_g[l], state_c_all, state_mlstm_n[l],
                                       state_mlstm_m[l], n_seq=DB, L=LS, l_valid=TS, c_base=l * DB, c_layers=depth,
                                       carried=c_all)
        mout = mout.reshape(DB, LS, D_M)[:, :TS].reshape(DB * TS, D_M)
        ckv = _cmp_paged(cache_t, w1p[l].reshape(2, CMP_STRIDE // 2, 2 * LANES, -1), w2p[l], page_table, l * n_pool)
        padq = lambda a: jnp.pad(a.reshape(DB, TS, -1), ((0, 0), (0, NQS - TS), (0, 0)))
        *o3, win_all = _nsa_sample(padq(zq), padq(zkv), padq(zw), TS, cache_t, l * n_pool, page_table, win_t, l * DB,
                                   ckv, bias_cmp_s, bias_slc_s, bias_win_s, ov_s, expand_s, win_all)
        ocmp, oslc, owin = [o[:, :TS].reshape(DB * TS, D_A) for o in o3]
        x1 = _merge(xs, mout, ocmp, oslc, owin, zg, zgm, gexp, wb16[l], wo16[l], ln1_g[l], ln1_b[l], alpha)
        xs = _moe(x1, wr_pad, rb_pad, wg16, wu16, wd16, l, ln2_g[l], ln2_b[l], alpha)
        outs[5].append(zkv.reshape(DB, TS, 4, HKV, HD))
        outs[8].append(n_s)
        outs[9].append(m_s)
    stacked = [jnp.stack(o) if o else None for o in outs]
    stacked[0] = from_feature_major(kv_all.reshape(depth, B, 4 * LANES, T), 4)
    stacked[1] = from_feature_major(stacked[1], 2)
    stacked[6] = from_feature_major(win_all.reshape(depth, DB, 2 * LANES, win_buf), 2)
    stacked[7] = c_all.reshape(depth, DB, H_M, DH_M, DH_M)
    return (xp.reshape(B, T, D), xs.reshape(DB, TS, D)) + tuple(stacked)
```

```python
import functools
import math

import numpy as np
import jax
import jax.numpy as jnp
from jax import lax
from jax.experimental import pallas as pl
from jax.experimental.pallas import tpu as pltpu

F32 = jnp.float32
BF16 = jnp.bfloat16
HIGHEST = lax.Precision.HIGHEST

H_M, DH_M = 4, 128
D_M = H_M * DH_M
HQ, HKV, HD = 8, 2, 64
REP = HQ // HKV
D_A = HQ * HD
CMP_LEN, CMP_STRIDE, CMP_HID = 32, 16, 256
SLC_LEN, N_SEL, WINDOW = 64, 16, 512
N_BUCKETS, REL_MAX_DIST = 32, 128
N_EXP, N_GROUPS, TOP_K, D_EXP = 16, 4, 2, 256
EXP_PER_GROUP = N_EXP // N_GROUPS
EXP_PER_STEP = 4
LN_EPS = 1e-5
NEG = -1e30

LANES = 128
QT = 128
KT = 512
LOG2E = math.log2(math.e)
ROWS = 32
NSP = 64
NQS = 8
VMEM_LIMIT = 56 * 1024 * 1024


def _cparams(*sem):
    return pltpu.CompilerParams(dimension_semantics=sem, vmem_limit_bytes=VMEM_LIMIT)


def _dot(a, b):
    return jnp.dot(a.astype(BF16), b.astype(BF16), preferred_element_type=F32)


def _dot_nt(a, b):
    return lax.dot_general(a.astype(BF16), b.astype(BF16), (((1,), (1,)), ((), ())), preferred_element_type=F32)


def _dot_tn(a, b):
    return lax.dot_general(a.astype(BF16), b.astype(BF16), (((0,), (0,)), ((), ())), preferred_element_type=F32)


def _dot_f32(a, b):
    return jnp.dot(a, b, precision=HIGHEST, preferred_element_type=F32)


def _dot_nt_f32(a, b):
    return lax.dot_general(a, b, (((1,), (1,)), ((), ())), precision=HIGHEST, preferred_element_type=F32)


def _pick_tile(n, cands):
    for c in cands:
        if n % c == 0:
            return c
    raise ValueError(f"no tile for {n}")


def _layer_norm(y, g, b):
    mu = jnp.mean(y, axis=-1, keepdims=True)
    yc = y - mu
    var = jnp.mean(yc * yc, axis=-1, keepdims=True)
    return yc * lax.rsqrt(var + LN_EPS) * g + b


def _inproj_kernel(x_ref, *refs, n_plain, t_split, n_carried):
    n_w = n_plain + (1 if t_split else 0)
    w_refs, b_refs, o_refs = refs[:n_plain], refs[n_w:n_w + n_plain], refs[2 * n_w + n_carried:]
    x = x_ref[...].astype(BF16)
    for w_ref, b_ref, o_ref in zip(w_refs, b_refs, o_refs):
        o_ref[...] = (jnp.dot(x, w_ref[...], preferred_element_type=F32) + b_ref[...]).astype(o_ref.dtype)
    if t_split:
        wt_ref, bt_ref = refs[n_plain], refs[n_w + n_plain]
        zt = lax.dot_general(wt_ref[...], x, (((1,), (1,)), ((), ())), preferred_element_type=F32) + bt_ref[...]
        lo_ref, hi_ref, all16_ref = o_refs[n_plain:]
        lo_ref[0] = zt[:t_split]
        hi_ref[0] = zt[t_split:]
        all16_ref[0] = zt.astype(BF16)


def _carry_args(carried):
    if carried is None:
        return [], [], 0
    return [carried], [pl.BlockSpec(memory_space=pl.ANY)], 1


def _inproj(x, ws, bs, wt=None, bt=None, t_split=0, n_seq=1, layer=0, depth=1, carried=None, dtypes=None):
    n_tok, d = x.shape
    t = n_tok // n_seq
    tm = _pick_tile(t, (256, 128, 64, 32, 16, 8))
    nt = t // tm
    full = lambda i: (0, 0)
    row = lambda i: (i, 0)
    tspec = lambda n, base: pl.BlockSpec((1, n, tm), lambda i: (base + i // nt, 0, i % nt))
    extra_w = [] if wt is None else [wt]
    extra_b = [] if wt is None else [bt]
    n_t = 0 if wt is None else wt.shape[0]
    t_specs = [] if wt is None else [tspec(t_split, layer * n_seq), tspec(n_t - t_split, 0), tspec(n_t, 0)]
    t_shapes = [] if wt is None else [jax.ShapeDtypeStruct((depth * n_seq, t_split, t), F32),
                                      jax.ShapeDtypeStruct((n_seq, n_t - t_split, t), F32),
                                      jax.ShapeDtypeStruct((n_seq, n_t, t), BF16)]
    c_args, c_specs, n_carried = _carry_args(carried)
    n_in = 1 + 2 * (len(ws) + len(extra_w))
    return pl.pallas_call(
        functools.partial(_inproj_kernel, n_plain=len(ws), t_split=t_split if wt is not None else 0,
                          n_carried=n_carried),
        grid=(n_tok // tm,),
        in_specs=[pl.BlockSpec((tm, d), row)]
        + [pl.BlockSpec(w.shape, full) for w in (*ws, *extra_w)]
        + [pl.BlockSpec(b.shape, full) for b in (*bs, *extra_b)] + c_specs,
        out_specs=[pl.BlockSpec((tm, w.shape[1]), row) for w in ws] + t_specs,
        out_shape=[jax.ShapeDtypeStruct((n_tok, w.shape[1]), dt) for w, dt in zip(ws, dtypes or [F32] * len(ws))]
        + t_shapes,
        input_output_aliases={n_in: len(ws)} if n_carried else {},
        compiler_params=_cparams("parallel"),
    )(x, *ws, *extra_w, *bs, *extra_b, *c_args)


def _log_sigmoid(x):
    return jnp.minimum(x, 0.0) - jnp.log(1.0 + jnp.exp(-jnp.abs(x)))


def _mlstm_kernel(q_ref, k_ref, v_ref, o_ref, g_ref, ng_ref, c0_ref, n0_ref, m0_ref, *rest, L, l_valid):
    mout_ref, c_ref, n_ref, m_ref, cs, ns, ms = rest[-7:]
    c = pl.program_id(1)

    @pl.when(c == 0)
    def _():
        cs[...] = c0_ref[0]
        ns[...] = n0_ref[0]
        ms[...] = m0_ref[0]

    g = g_ref[...]
    row = lax.broadcasted_iota(jnp.int32, (L, L), 0)
    col = lax.broadcasted_iota(jnp.int32, (L, L), 1)
    causal = row >= col
    fcum = _dot_f32(causal.astype(F32), _log_sigmoid(g))
    lane = lax.broadcasted_iota(jnp.int32, (L, LANES), 1)
    y = jnp.where(lane < H_M, g - pltpu.roll(fcum, LANES - H_M, 1), fcum)
    yt = y.T
    rowv = lax.broadcasted_iota(jnp.int32, (L, LANES), 0)
    r = l_valid - 1
    wide = (lambda x: jnp.concatenate([x] * (L // LANES), axis=1)) if L >= LANES else (lambda x: x[:, :L])
    for h in range(H_M):
        hs = slice(h * DH_M, (h + 1) * DH_M)
        f_col = jnp.broadcast_to(fcum[:, H_M + h:H_M + h + 1], (L, LANES))
        a_row = yt[h:h + 1, :]
        i_col = jnp.broadcast_to(g[:, h:h + 1], (L, LANES))
        m0 = ms[h]
        n0 = ns[h]
        c0 = cs[h]
        dm = jnp.where(causal, wide(f_col) + a_row, NEG)
        b = f_col + m0
        mrow = jnp.maximum(b, jnp.max(dm, axis=1, keepdims=True))
        w = jnp.exp(dm - wide(mrow))
        dec = jnp.exp(b - mrow)
        q = q_ref[:, hs]
        k = k_ref[:, hs] * (DH_M ** -0.5)
        v = v_ref[:, hs]
        s = _dot_nt(q, k) * w
        num = dec * _dot_nt(q, c0) + _dot(s, v)
        den = dec * jnp.sum(q * n0, axis=1, keepdims=True) + jnp.sum(s, axis=1, keepdims=True)
        hh = num / jnp.maximum(jnp.abs(den), jnp.exp(-mrow))
        mu = jnp.mean(hh, axis=1, keepdims=True)
        hc = hh - mu
        var = jnp.mean(hc * hc, axis=1, keepdims=True)
        hn = hc * lax.rsqrt(var + LN_EPS) * ng_ref[:, hs]
        mout_ref[:, hs] = (hn * jax.nn.sigmoid(o_ref[:, hs])).astype(mout_ref.dtype)
        f_r = f_col[r:r + 1, :]
        m_r = mrow[r:r + 1, :]
        w_last = jnp.where(rowv <= r, jnp.exp(f_r - f_col + i_col - m_r), 0.0)
        d_last = dec[r:r + 1, :]
        cs[h] = d_last * c0 + _dot_tn(v * w_last, k)
        ns[h] = d_last * n0 + jnp.sum(k * w_last, axis=0, keepdims=True)
        ms[h] = m_r

    @pl.when(c == pl.num_programs(1) - 1)
    def _():
        c_ref[0] = cs[...]
        n_ref[0] = ns[...]
        m_ref[0] = ms[...]


def _mlstm(zm, zg, norm_g, c0, n0, m0, *, n_seq, L, l_valid, c_base=0, c_layers=1, carried=None):
    n_tok = zm.shape[0]
    t = n_tok // n_seq
    nc = t // L
    n0 = n0.reshape(n_seq, H_M, 1, DH_M)
    m0 = jnp.broadcast_to(m0.reshape(n_seq, H_M, 1, 1), (n_seq, H_M, 1, LANES))
    colblk = lambda j: (lambda b, c: (b * nc + c, j))
    st4 = lambda b, c: (b, 0, 0, 0)
    c_spec = pl.BlockSpec((1, H_M, DH_M, DH_M), lambda b, c: (c_base + b, 0, 0, 0))
    c_args, c_specs, n_carried = _carry_args(carried)
    mout, c_new, n_new, m_new = pl.pallas_call(
        functools.partial(_mlstm_kernel, L=L, l_valid=l_valid),
        grid=(n_seq, nc),
        in_specs=[pl.BlockSpec((L, D_M), colblk(0)), pl.BlockSpec((L, D_M), colblk(1)),
                  pl.BlockSpec((L, D_M), colblk(2)), pl.BlockSpec((L, D_M), colblk(3)),
                  pl.BlockSpec((L, LANES), colblk(0)),
                  pl.BlockSpec((1, D_M), lambda b, c: (0, 0)),
                  c_spec,
                  pl.BlockSpec((1, H_M, 1, DH_M), st4),
                  pl.BlockSpec((1, H_M, 1, LANES), st4)] + c_specs,
        out_specs=[pl.BlockSpec((L, D_M), colblk(0)),
                   c_spec,
                   pl.BlockSpec((1, H_M, 1, DH_M), st4),
                   pl.BlockSpec((1, H_M, 1, LANES), st4)],
        out_shape=[jax.ShapeDtypeStruct((n_tok, D_M), BF16),
                   jax.ShapeDtypeStruct((c_layers * n_seq, H_M, DH_M, DH_M), F32),
                   jax.ShapeDtypeStruct((n_seq, H_M, 1, DH_M), F32),
                   jax.ShapeDtypeStruct((n_seq, H_M, 1, LANES), F32)],
        scratch_shapes=[pltpu.VMEM((H_M, DH_M, DH_M), F32), pltpu.VMEM((H_M, 1, DH_M), F32),
                        pltpu.VMEM((H_M, 1, LANES), F32)],
        input_output_aliases={9: 1} if n_carried else {},
        compiler_params=_cparams("parallel", "arbitrary"),
    )(zm, zm, zm, zm, zg, norm_g.reshape(1, D_M), c0, n0, m0, *c_args)
    return mout, c_new, n_new.reshape(n_seq, H_M, DH_M), m_new[:, :, 0, 0]


def _cmp_partial_kernel(x_ref, w_ref, o_ref):
    @pl.when(pl.program_id(2) == 0)
    def _():
        o_ref[...] = jnp.zeros_like(o_ref)

    o_ref[0] += _dot(x_ref[...], w_ref[0, 0])


def _cmp_partial(rows16, w1p):
    n_half = rows16.shape[0]
    tm = _pick_tile(n_half, (2048, 1024, 512, 256, 128, 64, 32, 16, 8))
    hid2 = 2 * HKV * CMP_HID
    return pl.pallas_call(
        _cmp_partial_kernel,
        grid=(2, n_half // tm, CMP_STRIDE),
        in_specs=[pl.BlockSpec((tm, LANES), lambda s, i, r: (i, 2 * r + s)),
                  pl.BlockSpec((1, 1, LANES, hid2), lambda s, i, r: (s, r, 0, 0))],
        out_specs=pl.BlockSpec((1, tm, hid2), lambda s, i, r: (s, i, 0)),
        out_shape=jax.ShapeDtypeStruct((2, n_half, hid2), F32),
        compiler_params=_cparams("parallel", "parallel", "arbitrary"),
    )(rows16, w1p)


def _gelu_tanh(x):
    return 0.5 * x * (1.0 + jnp.tanh(math.sqrt(2.0 / math.pi) * (x + 0.044715 * (x * x * x))))


def _cmp_finish_kernel(p_ref, w2_ref, o_ref):
    half = HKV * CMP_HID
    for s in range(2):
        p = p_ref[s]
        n = p.shape[0]
        hid = _gelu_tanh(p[:, :half] + pltpu.roll(p[:, half:], n - 1, 0))
        o_ref[0, s] = _dot(hid, w2_ref[s])


def _cmp_finish_prompt(part, w2p, n_seq):
    n_half = part.shape[1] // n_seq
    hid2 = part.shape[2]
    return pl.pallas_call(
        _cmp_finish_kernel,
        grid=(n_seq,),
        in_specs=[pl.BlockSpec((2, n_half, hid2), lambda b: (0, b, 0)),
                  pl.BlockSpec(w2p.shape, lambda b: (0, 0, 0))],
        out_specs=pl.BlockSpec((1, 2, n_half, LANES), lambda b: (b, 0, 0, 0)),
        out_shape=jax.ShapeDtypeStruct((n_seq, 2, n_half, LANES), F32),
        compiler_params=_cparams("parallel"),
    )(part, w2p)


def _cmp_paged_kernel(*refs, n_pages, n_item, page):
    page_refs = refs[1:1 + n_item * n_pages]
    perm_ref, w1_ref, w2_ref, o_ref, tok = refs[1 + n_item * n_pages:]
    half_per_page = page // CMP_STRIDE
    n = n_pages * half_per_page
    half = HKV * CMP_HID
    for s in range(2):
        for i, pr in enumerate(page_refs):
            xt = _dot_nt(perm_ref[...], pr[0, s * LANES:(s + 1) * LANES, :])
            tok[:, i * half_per_page:(i + 1) * half_per_page, :] = xt.reshape(CMP_STRIDE, half_per_page, LANES)
        acc = None
        for r in range(0, CMP_STRIDE, 2):
            part = _dot(jnp.concatenate([tok[r], tok[r + 1]], axis=1), w1_ref[s, r // 2])
            acc = part if acc is None else acc + part
        hids = []
        for it in range(n_item):
            p = acc[it * n:(it + 1) * n]
            hids.append(_gelu_tanh(p[:, :half] + pltpu.roll(p[:, half:], n - 1, 0)))
        out = _dot(jnp.concatenate(hids, axis=0) if n_item > 1 else hids[0], w2_ref[s])
        for it in range(n_item):
            o_ref[it, s] = out[it * n:(it + 1) * n]


def _cmp_paged(cache_t, w1p, w2p, page_table, page_base):
    n_seq, n_pages = page_table.shape
    page = cache_t.shape[2]
    half_per_page = page // CMP_STRIDE
    n_half = n_pages * half_per_page
    n_item = _pick_tile(n_seq, (4, 2, 1))
    tok = np.arange(page)
    perm = np.zeros((page, page), np.float32)
    perm[(tok % CMP_STRIDE) * half_per_page + tok // CMP_STRIDE, tok] = 1.0
    page_spec = lambda it, p: pl.BlockSpec((1, 2 * LANES, page),
                                           lambda b, pt: (page_base + pt[b * n_item + it, p], 0, 0))
    return pl.pallas_call(
        functools.partial(_cmp_paged_kernel, n_pages=n_pages, n_item=n_item, page=page),
        grid_spec=pltpu.PrefetchScalarGridSpec(
            num_scalar_prefetch=1, grid=(n_seq // n_item,),
            in_specs=[page_spec(it, p) for it in range(n_item) for p in range(n_pages)]
            + [pl.BlockSpec((page, page), lambda b, pt: (0, 0)),
               pl.BlockSpec(w1p.shape, lambda b, pt: (0, 0, 0, 0)),
               pl.BlockSpec(w2p.shape, lambda b, pt: (0, 0, 0))],
            out_specs=pl.BlockSpec((n_item, 2, n_half, LANES), lambda b, pt: (b, 0, 0, 0)),
            scratch_shapes=[pltpu.VMEM((CMP_STRIDE, n_item * n_half, LANES), F32)]),
        out_shape=jax.ShapeDtypeStruct((n_seq, 2, n_half, LANES), F32),
        compiler_params=_cparams("parallel"),
    )(page_table, *([cache_t] * (n_item * n_pages)), jnp.asarray(perm, dtype=BF16), w1p, w2p)


def _cmp_branch(q, ck, cv, bias_ref, ocmp_ref):
    nq = q.shape[0]
    qpad = jnp.concatenate([_padded_queries(q, g) for g in range(HKV)], axis=0)
    bias = bias_ref[...].reshape(HQ * nq, -1)
    s = _dot_nt(qpad, ck) + bias
    e = jnp.exp(s - jnp.max(s, axis=1, keepdims=True))
    p = jnp.where(bias > 0.5 * NEG, e / jnp.sum(e, axis=1, keepdims=True), 0.0)
    _store_all_heads(ocmp_ref, _dot(p, cv), nq)
    psums = []
    for g in range(HKV):
        heads = [p[(g * REP + r) * nq:(g * REP + r + 1) * nq] for r in range(REP)]
        psums.append(functools.reduce(lambda a, b: a + b, heads))
    return psums


def _select_blocks(imp, qpos, blk_axis, n_blocks):
    blk = lax.broadcasted_iota(jnp.int32, imp.shape, blk_axis)
    cur = (qpos // SLC_LEN) == blk
    avail = blk * SLC_LEN <= qpos
    imp = jnp.where(cur, -NEG, jnp.where(avail, imp, NEG))
    cnt = jnp.zeros(imp.shape, F32)
    for j in range(n_blocks):
        other = lax.slice_in_dim(imp, j, j + 1, axis=blk_axis)
        cnt = cnt + jnp.where(blk > j, jnp.where(other >= imp, 1.0, 0.0), jnp.where(other > imp, 1.0, 0.0))
    return jnp.where((cnt < N_SEL) & (imp > 0.5 * NEG), 1.0, 0.0)


def _padded_queries(q, g, dtype=BF16):
    nq = q.shape[0]
    zero = jnp.zeros((nq, HD), F32)
    parts = []
    for r in range(REP):
        h = g * REP + r
        piece = q[:, h * HD:(h + 1) * HD]
        parts.append(jnp.concatenate([piece, zero] if g == 0 else [zero, piece], axis=1))
    return jnp.concatenate(parts, axis=0).astype(dtype)


def _store_all_heads(o_ref, o, nq):
    for h in range(HQ):
        g = h // REP
        o_ref[:, h * HD:(h + 1) * HD] = o[h * nq:(h + 1) * nq, g * HD:(g + 1) * HD].astype(o_ref.dtype)


def _nsa_cmp_prompt_kernel(q_ref, ckv_ref, bias_ref, ovt_ref, ocmp_ref, sel_ref):
    qb = pl.program_id(1)
    q = q_ref[...] * (HD ** -0.5)
    ck, cv = ckv_ref[0, 0], ckv_ref[0, 1]
    qpos = qb * QT + lax.broadcasted_iota(jnp.int32, (1, QT), 1)
    for g, psum in enumerate(_cmp_branch(q, ck, cv, bias_ref, ocmp_ref)):
        imp_t = _dot_nt_f32(ovt_ref[...], psum)
        sel_ref[:, g * NSP:(g + 1) * NSP] = _select_blocks(imp_t, qpos, 0, NSP).T


def _nsa_cmp_prompt(zq, ckv, bias_cmp, ov_t, n_seq):
    n_tok = zq.shape[0]
    nqb = n_tok // n_seq // QT
    ncp = ckv.shape[2]
    tok = lambda b, i: (b * nqb + i, 0)
    return pl.pallas_call(
        _nsa_cmp_prompt_kernel,
        grid=(n_seq, nqb),
        in_specs=[pl.BlockSpec((QT, D_A), tok),
                  pl.BlockSpec((1, 2, ncp, LANES), lambda b, i: (b, 0, 0, 0)),
                  pl.BlockSpec((HQ, QT, ncp), lambda b, i: (0, i, 0)),
                  pl.BlockSpec(ov_t.shape, lambda b, i: (0, 0))],
        out_specs=[pl.BlockSpec((QT, D_A), tok), pl.BlockSpec((QT, HKV * NSP), tok)],
        out_shape=[jax.ShapeDtypeStruct((n_tok, D_A), BF16), jax.ShapeDtypeStruct((n_tok, HKV * NSP), F32)],
        compiler_params=_cparams("parallel", "parallel"),
    )(zq, ckv, bias_cmp, ov_t)


def _flash_prompt_kernel(q_ref, kt_ref, vt_ref, bias_ref, sel_ref, o_ref, s_scr, p_scr, m_scr, l_scr, a_scr, acc_scr,
                         *, n_delta, kt_len):
    g = pl.program_id(0)
    qb = pl.program_id(2)
    q = q_ref[...] * (HD ** -0.5 * LOG2E)
    sel = sel_ref[...]
    unpicked = (1.0 - jnp.where(g == 0, sel[:, :NSP], sel[:, NSP:])) * NEG
    parts = [jnp.concatenate([q[:, r * HD:(r + 1) * HD], unpicked], axis=1) for r in range(REP)]
    qrows = jnp.concatenate(parts, axis=0).astype(BF16)
    hi = (qb * QT) // kt_len + 1

    m_scr[...] = jnp.full(m_scr.shape, NEG, F32)
    l_scr[...] = jnp.zeros(l_scr.shape, F32)
    acc_scr[...] = jnp.zeros(acc_scr.shape, F32)

    def body(kt, carry):
        k0 = pl.multiple_of(kt * kt_len, kt_len)
        delta = jnp.minimum(qb - (kt_len // QT) * kt, n_delta - 1)
        blk = lax.broadcasted_iota(jnp.int32, (NSP, kt_len), 0)
        key = lax.broadcasted_iota(jnp.int32, (NSP, kt_len), 1)
        one_hot = jnp.where(blk == (k0 + key) // SLC_LEN, 1.0, 0.0).astype(BF16)
        k_t = jnp.concatenate([kt_ref[0, :, pl.ds(k0, kt_len)], one_hot], axis=0)
        s_scr[...] = jnp.dot(qrows, k_t, preferred_element_type=F32)
        for c in range(REP * QT // ROWS):
            r, qc = divmod(c, QT // ROWS)
            rows = pl.ds(c * ROWS, ROWS)
            s = s_scr[rows, :] + bias_ref[delta, r, pl.ds(qc * ROWS, ROWS), :]
            m_old = m_scr[rows, :]
            m_new = jnp.maximum(m_old, jnp.max(s, axis=1, keepdims=True))
            p = jnp.exp2(s - jnp.concatenate([m_new] * (kt_len // LANES), axis=1))
            alpha = jnp.exp2(m_old - m_new)
            l_scr[rows, :] = alpha * l_scr[rows, :] + jnp.sum(p, axis=1, keepdims=True)
            m_scr[rows, :] = m_new
            a_scr[rows, :] = alpha
            p_scr[rows, :] = p.astype(BF16)
        acc_scr[...] = a_scr[...] * acc_scr[...] + lax.dot_general(
            p_scr[...], vt_ref[0, :, pl.ds(k0, kt_len)], (((1,), (1,)), ((), ())), preferred_element_type=F32)
        return carry

    lax.fori_loop(0, hi, body, 0)
    o = acc_scr[...] / l_scr[...]
    for r in range(REP):
        o_r = o[r * QT:(r + 1) * QT]
        o_ref[:, r * HD:(r + 1) * HD] = jnp.where(g == 0, o_r[:, :HD], o_r[:, HD:]).astype(o_ref.dtype)


def _flash_prompt(zq, kv_t, kslot, vslot, bias, sel, n_seq):
    n_tok = zq.shape[0]
    t = kv_t.shape[2]
    nqb = t // QT
    kt_len = bias.shape[3]
    rows = REP * QT
    return pl.pallas_call(
        functools.partial(_flash_prompt_kernel, n_delta=bias.shape[0], kt_len=kt_len),
        grid=(HKV, n_seq, nqb),
        in_specs=[pl.BlockSpec((QT, REP * HD), lambda g, b, i: (b * nqb + i, g)),
                  pl.BlockSpec((1, HD, t), lambda g, b, i: (b, HKV * kslot + g, 0)),
                  pl.BlockSpec((1, LANES, t), lambda g, b, i: (b, vslot, 0)),
                  pl.BlockSpec((bias.shape[0], REP, QT, kt_len), lambda g, b, i: (0, g, 0, 0)),
                  pl.BlockSpec((QT, HKV * NSP), lambda g, b, i: (b * nqb + i, 0))],
        out_specs=pl.BlockSpec((QT, REP * HD), lambda g, b, i: (b * nqb + i, g)),
        out_shape=jax.ShapeDtypeStruct((n_tok, D_A), BF16),
        scratch_shapes=[pltpu.VMEM((rows, kt_len), F32), pltpu.VMEM((rows, kt_len), BF16),
                        pltpu.VMEM((rows, LANES), F32), pltpu.VMEM((rows, LANES), F32),
                        pltpu.VMEM((rows, LANES), F32), pltpu.VMEM((rows, LANES), F32)],
        compiler_params=_cparams("parallel", "parallel", "parallel"),
    )(zq, kv_t, kv_t, bias, sel)


def _window_prompt_kernel(q_ref, kt_ref, vt_ref, bias_a_ref, bias_b_ref, o_ref, s_a, s_b, p_a, p_b, l_scr):
    g = pl.program_id(0)
    qb = pl.program_id(2)
    n_back = WINDOW // QT
    q = q_ref[...] * (HD ** -0.5 * LOG2E)
    qrows = jnp.concatenate([q[:, r * HD:(r + 1) * HD] for r in range(REP)], axis=0).astype(BF16)
    a0 = pl.multiple_of(jnp.maximum(qb - n_back, 0) * QT, QT)
    b0 = pl.multiple_of(qb * QT, QT)
    delta_a = jnp.minimum(qb, n_back)
    tile_b = jnp.where(qb >= n_back, 0, 1)
    s_a[...] = jnp.dot(qrows, kt_ref[0, :, pl.ds(a0, WINDOW)], preferred_element_type=F32)
    s_b[...] = jnp.dot(qrows, kt_ref[0, :, pl.ds(b0, QT)], preferred_element_type=F32)
    for c in range(REP * QT // ROWS):
        r, qc = divmod(c, QT // ROWS)
        rows, qrows_c = pl.ds(c * ROWS, ROWS), pl.ds(qc * ROWS, ROWS)
        sa = s_a[rows, :] + bias_a_ref[delta_a, r, qrows_c, :]
        sb = s_b[rows, :] + bias_b_ref[tile_b, r, qrows_c, :]
        fold = lambda x, y, op: functools.reduce(op, [x[:, i:i + LANES] for i in range(0, WINDOW, LANES)] + [y])
        m = jnp.broadcast_to(jnp.max(fold(sa, sb, jnp.maximum), axis=1, keepdims=True), (ROWS, LANES))
        pa = jnp.exp2(sa - jnp.concatenate([m] * (WINDOW // LANES), axis=1))
        pb = jnp.exp2(sb - m)
        l_scr[rows, :] = jnp.broadcast_to(jnp.sum(fold(pa, pb, jnp.add), axis=1, keepdims=True), (ROWS, LANES))
        p_a[rows, :] = pa.astype(BF16)
        p_b[rows, :] = pb.astype(BF16)
    nt = (((1,), (1,)), ((), ()))
    o = (lax.dot_general(p_a[...], vt_ref[0, :, pl.ds(a0, WINDOW)], nt, preferred_element_type=F32)
         + lax.dot_general(p_b[...], vt_ref[0, :, pl.ds(b0, QT)], nt, preferred_element_type=F32)) / l_scr[...]
    for r in range(REP):
        o_r = o[r * QT:(r + 1) * QT]
        o_ref[:, r * HD:(r + 1) * HD] = jnp.where(g == 0, o_r[:, :HD], o_r[:, HD:]).astype(o_ref.dtype)


def _window_prompt(zq, kv_t, kslot, vslot, bias_a, bias_b, n_seq):
    n_tok = zq.shape[0]
    t = kv_t.shape[2]
    nqb = t // QT
    rows = REP * QT
    return pl.pallas_call(
        _window_prompt_kernel,
        grid=(HKV, n_seq, nqb),
        in_specs=[pl.BlockSpec((QT, REP * HD), lambda g, b, i: (b * nqb + i, g)),
                  pl.BlockSpec((1, HD, t), lambda g, b, i: (b, HKV * kslot + g, 0)),
                  pl.BlockSpec((1, LANES, t), lambda g, b, i: (b, vslot, 0)),
                  pl.BlockSpec((bias_a.shape[0], REP, QT, WINDOW), lambda g, b, i: (0, g, 0, 0)),
                  pl.BlockSpec((bias_b.shape[0], REP, QT, QT), lambda g, b, i: (0, g, 0, 0))],
        out_specs=pl.BlockSpec((QT, REP * HD), lambda g, b, i: (b * nqb + i, g)),
        out_shape=jax.ShapeDtypeStruct((n_tok, D_A), BF16),
        scratch_shapes=[pltpu.VMEM((rows, WINDOW), F32), pltpu.VMEM((rows, QT), F32), pltpu.VMEM((rows, WINDOW), BF16),
                        pltpu.VMEM((rows, QT), BF16), pltpu.VMEM((rows, LANES), F32)],
        compiler_params=_cparams("parallel", "parallel", "parallel"),
    )(zq, kv_t, kv_t, bias_a, bias_b)


def _nsa_sample_kernel(*refs, n_pages, nq, n_new, past, n_item, n_carried):
    n_in = 1 + n_item * n_pages
    all_pages = refs[1:n_in]
    q_ref, kvn_ref, wn_ref, wbuf_ref, ckv_ref, bcmp_ref, bslc_ref, bwin_ref, ov_ref, exp_ref = refs[n_in:n_in + 10]
    ocmp_ref, oslc_ref, owin_ref, wout_ref = refs[n_in + 10 + n_carried:n_in + 14 + n_carried]
    scratch = refs[n_in + 14 + n_carried:]
    for it in range(n_item):
        _nsa_sample_item(all_pages[it * n_pages:(it + 1) * n_pages], q_ref.at[it], kvn_ref.at[it], wn_ref.at[it],
                         wbuf_ref.at[it], ckv_ref.at[it], bcmp_ref, bslc_ref, bwin_ref, ov_ref, exp_ref,
                         ocmp_ref.at[it], oslc_ref.at[it], owin_ref.at[it], wout_ref.at[it],
                         scratch[2 * it], scratch[2 * it + 1], nq=nq, n_new=n_new, past=past)


def _nsa_sample_item(page_refs, q_ref, kvn_ref, wn_ref, wbuf_ref, ckv_ref, bcmp_ref, bslc_ref, bwin_ref, ov_ref,
                     exp_ref, ocmp_ref, oslc_ref, owin_ref, wout_ref, new_kv, new_w, *, nq, n_new, past):
    new_kv[nq:, :] = jnp.zeros((LANES - nq, 4 * LANES), F32)
    new_kv[:nq, :] = kvn_ref[...]
    new_w[nq:, :] = jnp.zeros((LANES - nq, 2 * LANES), F32)
    new_w[:nq, :] = wn_ref[...]
    win_buf = wbuf_ref.shape[1]
    shifted = pltpu.roll(wbuf_ref[...], win_buf - n_new, 1)
    new_t = jnp.concatenate([new_w[:, :LANES].T, new_w[:, LANES:].T], axis=0)
    new_t = pltpu.roll(new_t, LANES - n_new, 1)
    tail = lax.broadcasted_iota(jnp.int32, (2 * LANES, LANES), 1) >= LANES - n_new
    wout_ref[:, :win_buf - LANES] = shifted[:, :win_buf - LANES]
    wout_ref[:, win_buf - LANES:] = jnp.where(tail, new_t, shifted[:, win_buf - LANES:])

    q = q_ref[...] * (HD ** -0.5)
    ck, cv = ckv_ref[0], ckv_ref[1]
    qpos = past + lax.broadcasted_iota(jnp.int32, (nq, 1), 0)

    def attend(qpad, old_k, old_v, new_k, new_v, bias):
        s = jnp.concatenate([jnp.dot(qpad, k_t.astype(BF16), preferred_element_type=F32) for k_t in old_k]
                            + [_dot_nt(qpad, new_k)], axis=1) + bias
        e = jnp.exp(s - jnp.max(s, axis=1, keepdims=True))
        o, c0 = _dot(e[:, s.shape[1] - LANES:], new_v), 0
        for v_t in old_v:
            o = o + _dot_nt(e[:, c0:c0 + v_t.shape[1]], v_t)
            c0 += v_t.shape[1]
        return o / jnp.sum(e, axis=1, keepdims=True)

    picked = []
    n_blocks = -(-(past + nq) // SLC_LEN)
    for psum in _cmp_branch(q, ck, cv, bcmp_ref, ocmp_ref):
        sel = _select_blocks(_dot_f32(psum, ov_ref[...]), qpos, 1, n_blocks)
        picked += [jnp.dot(sel.astype(BF16), exp_ref[...], preferred_element_type=F32)] * REP
    qpad = jnp.concatenate([_padded_queries(q, g) for g in range(HKV)], axis=0)
    bslc = jnp.where(jnp.concatenate(picked, axis=0) > 0.5, bslc_ref[...].reshape(HQ * nq, -1), NEG)
    o = attend(qpad, [pr[0, 2 * LANES:3 * LANES, :] for pr in page_refs],
               [pr[0, 3 * LANES:4 * LANES, :] for pr in page_refs],
               new_kv[:, 2 * LANES:3 * LANES], new_kv[:, 3 * LANES:4 * LANES], bslc)
    _store_all_heads(oslc_ref, o, nq)
    o = attend(qpad, [wbuf_ref[:LANES, :]], [wbuf_ref[LANES:, :]], new_w[:, :LANES], new_w[:, LANES:],
               bwin_ref[...].reshape(HQ * nq, -1))
    _store_all_heads(owin_ref, o, nq)


def _nsa_sample(zq, kvn, wn, n_new, cache_t, page_base, page_table, win_t, win_base, ckv,
                bias_cmp, bias_slc, bias_win, ov, expand, carried):
    n_seq, nq, _ = zq.shape
    n_pages = page_table.shape[1]
    page = cache_t.shape[2]
    past = n_pages * page
    win_buf = win_t.shape[2]
    ncp = ckv.shape[2]
    n_item = _pick_tile(n_seq, (2, 1))
    seq3 = lambda b, pt: (b, 0, 0)
    const = lambda nd: (lambda b, pt: (0,) * nd)
    page_spec = lambda it, p: pl.BlockSpec((1, 4 * LANES, page),
                                           lambda b, pt: (page_base + pt[b * n_item + it, p], 0, 0))
    in_specs = [page_spec(it, p) for it in range(n_item) for p in range(n_pages)] + [
        pl.BlockSpec((n_item, nq, D_A), seq3),
        pl.BlockSpec((n_item, nq, 4 * LANES), seq3),
        pl.BlockSpec((n_item, nq, 2 * LANES), seq3),
        pl.BlockSpec((n_item, 2 * LANES, win_buf), lambda b, pt: (win_base // n_item + b, 0, 0)),
        pl.BlockSpec((n_item, 2, ncp, LANES), lambda b, pt: (b, 0, 0, 0)),
        pl.BlockSpec(bias_cmp.shape, const(3)),
        pl.BlockSpec(bias_slc.shape, const(3)),
        pl.BlockSpec(bias_win.shape, const(3)),
        pl.BlockSpec(ov.shape, const(2)),
        pl.BlockSpec(expand.shape, const(2))]
    out = pl.BlockSpec((n_item, nq, D_A), seq3)
    wout = pl.BlockSpec((n_item, 2 * LANES, win_buf), lambda b, pt: (win_base // n_item + b, 0, 0))
    c_args, c_specs, n_carried = _carry_args(carried)
    n_in = 1 + n_item * n_pages + 10
    return pl.pallas_call(
        functools.partial(_nsa_sample_kernel, n_pages=n_pages, nq=nq, n_new=n_new, past=past, n_item=n_item,
                          n_carried=n_carried),
        grid_spec=pltpu.PrefetchScalarGridSpec(
            num_scalar_prefetch=1, grid=(n_seq // n_item,), in_specs=in_specs + c_specs,
            out_specs=[out, out, out, wout],
            scratch_shapes=[pltpu.VMEM((LANES, 4 * LANES), F32), pltpu.VMEM((LANES, 2 * LANES), F32)] * n_item),
        out_shape=[jax.ShapeDtypeStruct((n_seq, nq, D_A), F32)] * 3 + [jax.ShapeDtypeStruct(win_t.shape, F32)],
        input_output_aliases={n_in: 3} if n_carried else {},
        compiler_params=_cparams("parallel"),
    )(page_table, *([cache_t] * (n_item * n_pages)), zq, kvn, wn, win_t, ckv, bias_cmp, bias_slc, bias_win, ov, expand,
      *c_args)


def _merge_kernel(x_ref, mout_ref, ocmp_ref, oslc_ref, owin_ref, zg_ref, zgm_ref, gexp_ref, wb_ref, wo_ref,
                  lg_ref, lb_ref, o_ref, *, alpha):
    d = x_ref.shape[1]
    gate = jax.nn.sigmoid(zg_ref[...])
    terms, rest = [], gate
    for _ in range(3):
        terms.append(rest.astype(BF16))
        rest = rest - terms[-1].astype(F32)
    spread = jnp.dot(jnp.concatenate(terms, axis=1), gexp_ref[...], preferred_element_type=F32)
    a = None
    for br, src in enumerate((ocmp_ref, oslc_ref, owin_ref)):
        term = spread[:, br * D_A:(br + 1) * D_A] * src[...]
        a = term if a is None else a + term
    u = (jax.nn.sigmoid(zgm_ref[:, :d].astype(F32)) * _dot(mout_ref[...], wb_ref[0])
         + jax.nn.sigmoid(zgm_ref[:, d:].astype(F32)) * _dot(a, wb_ref[1]))
    y = _dot(u, wo_ref[...])
    o_ref[...] = _layer_norm(alpha * x_ref[...] + y, lg_ref[...], lb_ref[...])


def _merge(x, mout, ocmp, oslc, owin, zg, zgm, gexp, wb, wo, lg, lb, alpha):
    n_tok, d = x.shape
    tm = _pick_tile(n_tok, (256, 128, 64, 32, 16, 8))
    row = lambda i: (i, 0)
    c2 = lambda i: (0, 0)
    c3 = lambda i: (0, 0, 0)
    return pl.pallas_call(
        functools.partial(_merge_kernel, alpha=alpha),
        grid=(n_tok // tm,),
        in_specs=[pl.BlockSpec((tm, d), row), pl.BlockSpec((tm, D_M), row), pl.BlockSpec((tm, D_A), row),
                  pl.BlockSpec((tm, D_A), row), pl.BlockSpec((tm, D_A), row), pl.BlockSpec((tm, LANES), row),
                  pl.BlockSpec((tm, 2 * d), row), pl.BlockSpec(gexp.shape, c2), pl.BlockSpec(wb.shape, c3),
                  pl.BlockSpec(wo.shape, c2), pl.BlockSpec((1, d), c2), pl.BlockSpec((1, d), c2)],
        out_specs=pl.BlockSpec((tm, d), row),
        out_shape=jax.ShapeDtypeStruct((n_tok, d), F32),
        compiler_params=_cparams("parallel"),
    )(x, mout, ocmp, oslc, owin, zg, zgm, gexp, wb, wo, lg.reshape(1, d), lb.reshape(1, d))


def _route(aff, sel):
    def top2_sum(a, b, c, d):
        x, x2, y, y2 = jnp.maximum(a, b), jnp.minimum(a, b), jnp.maximum(c, d), jnp.minimum(c, d)
        return jnp.maximum(x, y) + jnp.maximum(jnp.minimum(x, y), jnp.maximum(x2, y2))

    gsum = [top2_sum(*sel[EXP_PER_GROUP * gi:EXP_PER_GROUP * (gi + 1)]) for gi in range(N_GROUPS)]
    gmax = functools.reduce(jnp.maximum, gsum)
    chosen, taken = [], None
    for gi in range(N_GROUPS):
        is_best = gsum[gi] == gmax if taken is None else (gsum[gi] == gmax) & jnp.logical_not(taken)
        taken = is_best if taken is None else taken | is_best
        members = range(EXP_PER_GROUP * gi, EXP_PER_GROUP * (gi + 1))
        for e in members:
            ahead = None
            for e2 in members:
                if e2 == e:
                    continue
                before = (sel[e2] >= sel[e]) if e2 < e else (sel[e2] > sel[e])
                cnt = jnp.where(before, 1.0, 0.0)
                ahead = cnt if ahead is None else ahead + cnt
            chosen.append(is_best & (ahead < TOP_K))
    picked = [jnp.where(c, a, 0.0) for c, a in zip(chosen, aff)]
    total = functools.reduce(lambda a, b: a + b, picked)
    return [p / total for p in picked]


def _moe_kernel(x_ref, wr_ref, rb_ref, wg_ref, wu_ref, wd_ref, lg_ref, lb_ref, o_ref, xb, comb, acc, *, alpha):
    step = pl.program_id(1)
    tm = x_ref.shape[0]
    lane = lax.broadcasted_iota(jnp.int32, (tm, LANES), 1)

    @pl.when(step == 0)
    def _():
        x16 = x_ref[...].astype(BF16)
        xb[...] = x16
        aff_t = jax.nn.sigmoid(_dot_nt(wr_ref[...], x16))
        sel_t = aff_t + rb_ref[...]
        weights = _route([aff_t[i:i + 1, :] for i in range(N_EXP)], [sel_t[i:i + 1, :] for i in range(N_EXP)])
        expert = lax.broadcasted_iota(jnp.int32, (N_EXP, tm), 0)
        comb_t = jnp.zeros((N_EXP, tm), F32)
        for i, w in enumerate(weights):
            comb_t = jnp.where(expert == i, w, comb_t)
        comb[...] = jnp.concatenate([comb_t, jnp.zeros((LANES - N_EXP, tm), F32)], axis=0).T
        acc[...] = jnp.zeros_like(acc)

    x16 = xb[...]
    hs = []
    for j in range(EXP_PER_STEP):
        w_e = jnp.sum(jnp.where(lane == EXP_PER_STEP * step + j, comb[...], 0.0), axis=1, keepdims=True)
        hg = jnp.dot(x16, wg_ref[0, j], preferred_element_type=F32)
        hu = jnp.dot(x16, wu_ref[0, j], preferred_element_type=F32)
        hs.append((hg * jax.nn.sigmoid(hg) * hu * w_e).astype(BF16))
    d = acc.shape[1]
    acc[...] += jnp.dot(jnp.concatenate(hs, axis=1), wd_ref[0].reshape(EXP_PER_STEP * D_EXP, d),
                        preferred_element_type=F32)

    @pl.when(step == pl.num_programs(1) - 1)
    def _():
        o_ref[...] = _layer_norm(alpha * x_ref[...] + acc[...], lg_ref[...], lb_ref[...])


def _moe(x, wr, rb, wg, wu, wd, layer, lg, lb, alpha):
    n_tok, d = x.shape
    tm = _pick_tile(n_tok, (1024, 512, 256, 128, 64, 32, 16, 8))
    row = lambda i, e: (i, 0)
    c2 = lambda i, e: (0, 0)
    wsel = lambda i, e: (layer, e, 0, 0)
    return pl.pallas_call(
        functools.partial(_moe_kernel, alpha=alpha),
        grid=(n_tok // tm, N_EXP // EXP_PER_STEP),
        in_specs=[pl.BlockSpec((tm, d), row), pl.BlockSpec(wr.shape, c2), pl.BlockSpec(rb.shape, c2),
                  pl.BlockSpec((1, EXP_PER_STEP, d, D_EXP), wsel), pl.BlockSpec((1, EXP_PER_STEP, d, D_EXP), wsel),
                  pl.BlockSpec((1, EXP_PER_STEP, D_EXP, d), wsel), pl.BlockSpec((1, d), c2), pl.BlockSpec((1, d), c2)],
        out_specs=pl.BlockSpec((tm, d), row),
        out_shape=jax.ShapeDtypeStruct((n_tok, d), F32),
        scratch_shapes=[pltpu.VMEM((tm, d), BF16), pltpu.VMEM((tm, LANES), F32), pltpu.VMEM((tm, d), F32)],
        compiler_params=_cparams("parallel", "arbitrary"),
    )(x, wr, rb, wg, wu, wd, lg.reshape(1, d), lb.reshape(1, d))


def _bucket_np(dist):
    n = np.maximum(dist, 0)
    exact = N_BUCKETS // 2
    nf = np.maximum(n, 1).astype(np.float32)
    large = exact + (np.log(nf / np.float32(exact)) / np.float32(math.log(REL_MAX_DIST / exact))
                     * np.float32(N_BUCKETS - exact)).astype(np.int32)
    return np.where(n < exact, n, np.minimum(large, N_BUCKETS - 1)).astype(np.int32)


def _bias_table(rel_bias, dist, valid):
    tab = rel_bias.astype(F32)[jnp.asarray(_bucket_np(dist))]
    tab = jnp.where(jnp.asarray(valid)[..., None], tab, NEG)
    return jnp.moveaxis(tab, -1, 0)


def _bias_tiles_kernel(rb_ref, o_ref, *, tile_step, key_stride, key_off, hi_valid, upper, lead):
    t = pl.program_id(0)
    shape = o_ref.shape[-2:]
    dist = (tile_step * t - key_off + lax.broadcasted_iota(jnp.int32, shape, 0)
            - key_stride * lax.broadcasted_iota(jnp.int32, shape, 1))
    valid = dist >= 0 if hi_valid is None else (dist >= 0) & (dist < hi_valid)
    acc = [jnp.full(shape, rb_ref[N_BUCKETS - 1, h], F32) for h in range(HQ)]
    for b in range(N_BUCKETS - 2, -1, -1):
        below = dist < upper[b]
        acc = [jnp.where(below, rb_ref[b, h], a) for h, a in enumerate(acc)]
    for h in range(HQ):
        if lead:
            o_ref[0, h] = jnp.where(valid, acc[h], NEG)
        else:
            o_ref[h] = jnp.where(valid, acc[h], NEG)


def _bias_tiles(rel_bias, n_tiles, n_keys, *, tile_step, key_stride, key_off, hi_valid, lead, scale=1.0):
    buckets = _bucket_np(np.arange(8 * REL_MAX_DIST))
    upper = tuple(int(np.searchsorted(buckets, b, side='right')) for b in range(N_BUCKETS - 1))
    if lead:
        out_spec = pl.BlockSpec((1, HQ, QT, n_keys), lambda t: (t, 0, 0, 0))
        out_shape = jax.ShapeDtypeStruct((n_tiles, HQ, QT, n_keys), F32)
    else:
        out_spec = pl.BlockSpec((HQ, QT, n_keys), lambda t: (0, t, 0))
        out_shape = jax.ShapeDtypeStruct((HQ, n_tiles * QT, n_keys), F32)
    return pl.pallas_call(
        functools.partial(_bias_tiles_kernel, tile_step=tile_step, key_stride=key_stride, key_off=key_off,
                          hi_valid=hi_valid, upper=upper, lead=lead),
        grid=(n_tiles,),
        in_specs=[pl.BlockSpec(memory_space=pltpu.SMEM)],
        out_specs=out_spec, out_shape=out_shape,
        compiler_params=_cparams("parallel"),
    )(rel_bias.astype(F32) * scale)


def _overlap_np(ncp, n_cmp, n_slc):
    c0 = np.arange(ncp)[:, None] * CMP_STRIDE
    j0 = np.arange(NSP)[None, :] * SLC_LEN
    ov = (c0 < j0 + SLC_LEN) & (c0 + CMP_LEN > j0)
    ov &= (np.arange(ncp)[:, None] < n_cmp) & (np.arange(NSP)[None, :] < n_slc)
    return ov.astype(np.float32)


def kernel(x_prompt, x_sample, cache_kv, state_win_kv, state_mlstm_C, state_mlstm_n, state_mlstm_m, page_table,
           w_in, b_in, w_cmp1, w_cmp2, mh_norm_g, w_branch, w_out, ln1_g, ln1_b, ln2_g, ln2_b,
           w_router, router_bias, w_gate_e, w_up_e, w_down_e, rel_bias):
    B, T, D = x_prompt.shape
    DB, TS, _ = x_sample.shape
    depth, n_pool, page = cache_kv.shape[:3]
    n_pages = page_table.shape[1]
    past = n_pages * page
    win_buf = state_win_kv.shape[2]
    alpha = (2 * depth) ** 0.25
    assert T % KT == 0 and T // SLC_LEN <= NSP and page % CMP_STRIDE == 0 and past % SLC_LEN == 0
    assert win_buf == min(WINDOW, past) and T >= win_buf

    off = np.cumsum((0, D_M, D_M, D_M, D_M, H_M, H_M, D_A, 6 * HKV * HD, 3 * HQ, 2 * D)).tolist()
    seg = lambda a, i, j: a[..., off[i]:off[j]]
    small = lambda a: jnp.concatenate(
        [seg(a, 4, 6), seg(a, 8, 9), jnp.zeros(a.shape[:-1] + (LANES - 2 * H_M - 3 * HQ,), a.dtype)], axis=-1)
    kv_cmp, kv_mid = off[7] + 2 * HKV * HD, off[7] + 4 * HKV * HD
    groups_s = lambda a: (seg(a, 0, 4), small(a), seg(a, 6, 7), a[..., off[7]:kv_mid], a[..., kv_mid:off[8]], seg(a, 9, 10))
    groups_p = lambda a: (seg(a, 0, 4), small(a), seg(a, 6, 7), a[..., off[7]:kv_cmp], seg(a, 9, 10))
    ws_s = [w.astype(BF16) for w in groups_s(w_in)]
    bs_s = [b[:, None, :] for b in groups_s(b_in)]
    ws_p = [w.astype(BF16) for w in groups_p(w_in)]
    bs_p = [b[:, None, :] for b in groups_p(b_in)]
    w_in_t = jnp.transpose(w_in, (0, 2, 1))
    wt_p = w_in_t[:, off[7]:off[8]].astype(BF16)
    bt_p = b_in[:, off[7]:off[8], None]
    wb16, wo16 = w_branch.astype(BF16), w_out.astype(BF16)
    wg16, wu16, wd16 = w_gate_e.astype(BF16), w_up_e.astype(BF16), w_down_e.astype(BF16)
    wr_pad = jnp.transpose(w_router).astype(BF16)
    rb_pad = router_bias.astype(F32).reshape(N_EXP, 1)
    eye = jnp.eye(HKV, dtype=F32)
    w1 = w_cmp1.reshape(depth, 2, 2, CMP_STRIDE, HD, CMP_HID)
    w1p = jnp.einsum('lshrdf,gG->lsrgdhGf', w1, eye).reshape(depth, 2, CMP_STRIDE, HKV * HD, 2 * HKV * CMP_HID)
    w1p = w1p.astype(BF16)
    w2p = jnp.einsum('lsfd,gG->lsgfGd', w_cmp2, eye).reshape(depth, 2, HKV * CMP_HID, HKV * HD).astype(BF16)
    gexp = np.zeros((LANES, 3 * D_A), np.float32)
    for br in range(3):
        for h in range(HQ):
            gexp[2 * H_M + br * HQ + h, br * D_A + h * HD:br * D_A + (h + 1) * HD] = 1.0
    gexp = jnp.asarray(np.tile(gexp, (3, 1)), dtype=BF16)

    ncp_p, n_cmp_p = T // CMP_STRIDE, (T - CMP_LEN) // CMP_STRIDE + 1
    assert n_cmp_p * CMP_STRIDE + CMP_LEN - 1 > T - 1
    bias_cmp_p = _bias_tiles(rel_bias, T // QT, ncp_p, tile_step=QT, key_stride=CMP_STRIDE, key_off=CMP_LEN - 1,
                             hi_valid=None, lead=False)
    ov_t_p = jnp.asarray(_overlap_np(ncp_p, n_cmp_p, -(-T // SLC_LEN)).T)
    n_far = -(-(KT - 1 + REL_MAX_DIST) // QT)
    bias_slc_p = _bias_tiles(rel_bias, n_far + 1, KT, tile_step=QT, key_stride=1, key_off=0, hi_valid=None, lead=True,
                             scale=LOG2E)
    bias_win_a = _bias_tiles(rel_bias, WINDOW // QT + 1, WINDOW, tile_step=QT, key_stride=1, key_off=0,
                             hi_valid=WINDOW, lead=True, scale=LOG2E)
    bias_win_b = _bias_tiles(rel_bias, 2, QT, tile_step=-2 * WINDOW, key_stride=1, key_off=0,
                             hi_valid=WINDOW, lead=True, scale=LOG2E)

    ncp_s = past // CMP_STRIDE
    n_cmp_s = (past + TS - CMP_LEN) // CMP_STRIDE + 1
    assert n_cmp_s <= ncp_s and -(-(past + TS) // SLC_LEN) <= NSP and TS <= NQS
    qs = past + np.arange(NQS)[:, None]
    d_cmp_s = qs - (np.arange(ncp_s)[None, :] * CMP_STRIDE + CMP_LEN - 1)
    bias_cmp_s = _bias_table(rel_bias, d_cmp_s, (d_cmp_s >= 0) & (np.arange(ncp_s)[None, :] < n_cmp_s))
    ov_s = jnp.asarray(_overlap_np(ncp_s, n_cmp_s, -(-(past + TS) // SLC_LEN)))
    key_s = np.arange(past + LANES)[None, :]
    d_slc_s = qs - key_s
    bias_slc_s = _bias_table(rel_bias, d_slc_s, (d_slc_s >= 0) & (key_s < past + TS))
    expand_s = jnp.asarray((np.arange(NSP)[:, None] == key_s // SLC_LEN).astype(np.float32)).astype(BF16)
    idx_w = np.arange(win_buf + LANES)[None, :]
    d_win_s = qs - (past - win_buf + idx_w)
    bias_win_s = _bias_table(rel_bias, d_win_s, (d_win_s >= 0) & (d_win_s < WINDOW) & (idx_w < win_buf + TS))

    to_feature_major = lambda a: jnp.transpose(a, (0, 1, 3, 4, 5, 2))
    cache_t = to_feature_major(cache_kv).reshape(depth * n_pool, 4 * LANES, page)
    win_t = to_feature_major(state_win_kv).reshape(depth * DB, 2 * LANES, win_buf)
    from_feature_major = lambda a, slots: jnp.transpose(
        a.reshape(a.shape[:2] + (slots, HKV, HD, a.shape[-1])), (0, 1, 5, 2, 3, 4))

    LP = _pick_tile(T, (256, 128, 64))
    LS = NQS
    zeros_state = (jnp.zeros((B, H_M, DH_M, DH_M), F32), jnp.zeros((B, H_M, DH_M), F32), jnp.zeros((B, H_M), F32))
    state_c_all = state_mlstm_C.reshape(depth * DB, H_M, DH_M, DH_M)

    xp = x_prompt.reshape(B * T, D)
    xs = x_sample.reshape(DB * TS, D)
    outs = [[] for _ in range(10)]
    kv_all = jnp.zeros((depth * B, 4 * LANES, T), F32)
    win_all = jnp.zeros(win_t.shape, F32)
    c_all = jnp.zeros(state_c_all.shape, F32)
    for l in range(depth):
        zm, zg, zq, zc, zgm, kv_all, zw_t, z16_t = _inproj(
            xp, [w[l] for w in ws_p], [b[l] for b in bs_p], wt_p[l], bt_p[l], t_split=4 * LANES, n_seq=B,
            layer=l, depth=depth, carried=kv_all, dtypes=[F32, F32, F32, F32, BF16])
        mout, c_p, n_p, m_p = _mlstm(zm, zg, mh_norm_g[l], *zeros_state, n_seq=B, L=LP, l_valid=LP)
        part = _cmp_partial(zc.reshape(B * T // CMP_STRIDE, CMP_STRIDE * 2 * LANES), w1p[l])
        ckv = _cmp_finish_prompt(part, w2p[l], B)
        ocmp, sel = _nsa_cmp_prompt(zq, ckv, bias_cmp_p, ov_t_p, B)
        oslc = _flash_prompt(zq, z16_t, 2, 3, bias_slc_p, sel, B)
        owin = _window_prompt(zq, z16_t, 4, 5, bias_win_a, bias_win_b, B)
        x1 = _merge(xp, mout, ocmp, oslc, owin, zg, zgm, gexp, wb16[l], wo16[l], ln1_g[l], ln1_b[l], alpha)
        xp = _moe(x1, wr_pad, rb_pad, wg16, wu16, wd16, l, ln2_g[l], ln2_b[l], alpha)
        outs[1].append(zw_t[:, :, T - win_buf:])
        outs[2].append(c_p)
        outs[3].append(n_p)
        outs[4].append(m_p)
        zm, zg, zq, zkv, zw, zgm = _inproj(xs, [w[l] for w in ws_s], [b[l] for b in bs_s])
        padt = lambda a: jnp.pad(a.reshape(DB, TS, -1), ((0, 0), (0, LS - TS), (0, 0))).reshape(DB * LS, -1)
        mout, c_all, n_s, m_s = _mlstm(padt(zm), padt(zg), mh_norm_g[l], state_c_all, state_mlstm_n[l],
                                       state_mlstm_m[l], n_seq=DB, L=LS, l_valid=TS, c_base=l * DB, c_layers=depth,
                                       carried=c_all)
        mout = mout.reshape(DB, LS, D_M)[:, :TS].reshape(DB * TS, D_M)
        ckv = _cmp_paged(cache_t, w1p[l].reshape(2, CMP_STRIDE // 2, 2 * LANES, -1), w2p[l], page_table, l * n_pool)
        padq = lambda a: jnp.pad(a.reshape(DB, TS, -1), ((0, 0), (0, NQS - TS), (0, 0)))
        *o3, win_all = _nsa_sample(padq(zq), padq(zkv), padq(zw), TS, cache_t, l * n_pool, page_table, win_t, l * DB,
                                   ckv, bias_cmp_s, bias_slc_s, bias_win_s, ov_s, expand_s, win_all)
        ocmp, oslc, owin = [o[:, :TS].reshape(DB * TS, D_A) for o in o3]
        x1 = _merge(xs, mout, ocmp, oslc, owin, zg, zgm, gexp, wb16[l], wo16[l], ln1_g[l], ln1_b[l], alpha)
        xs = _moe(x1, wr_pad, rb_pad, wg16, wu16, wd16, l, ln2_g[l], ln2_b[l], alpha)
        outs[5].append(zkv.reshape(DB, TS, 4, HKV, HD))
        outs[8].append(n_s)
        outs[9].append(m_s)
    stacked = [jnp.stack(o) if o else None for o in outs]
    stacked[0] = from_feature_major(kv_all.reshape(depth, B, 4 * LANES, T), 4)
    stacked[1] = from_feature_major(stacked[1], 2)
    stacked[6] = from_feature_major(win_all.reshape(depth, DB, 2 * LANES, win_buf), 2)
    stacked[7] = c_all.reshape(depth, DB, H_M, DH_M, DH_M)
    return (xp.reshape(B, T, D), xs.reshape(DB, TS, D)) + tuple(stacked)
```

```python
import functools
import math

import numpy as np
import jax
import jax.numpy as jnp
from jax import lax
from jax.experimental import pallas as pl
from jax.experimental.pallas import tpu as pltpu

F32 = jnp.float32
BF16 = jnp.bfloat16
HIGHEST = lax.Precision.HIGHEST

H_M, DH_M = 4, 128
D_M = H_M * DH_M
HQ, HKV, HD = 8, 2, 64
REP = HQ // HKV
D_A = HQ * HD
CMP_LEN, CMP_STRIDE, CMP_HID = 32, 16, 256
SLC_LEN, N_SEL, WINDOW = 64, 16, 512
N_BUCKETS, REL_MAX_DIST = 32, 128
N_EXP, N_GROUPS, TOP_K, D_EXP = 16, 4, 2, 256
EXP_PER_GROUP = N_EXP // N_GROUPS
EXP_PER_STEP = 4
LN_EPS = 1e-5
NEG = -1e30

LANES = 128
QT = 128
KT = 512
LOG2E = math.log2(math.e)
ROWS = 32
NSP = 64
NQS = 8
VMEM_LIMIT = 56 * 1024 * 1024


def _cparams(*sem):
    return pltpu.CompilerParams(dimension_semantics=sem, vmem_limit_bytes=VMEM_LIMIT)


def _dot(a, b):
    return jnp.dot(a.astype(BF16), b.astype(BF16), preferred_element_type=F32)


def _dot_nt(a, b):
    return lax.dot_general(a.astype(BF16), b.astype(BF16), (((1,), (1,)), ((), ())), preferred_element_type=F32)


def _dot_tn(a, b):
    return lax.dot_general(a.astype(BF16), b.astype(BF16), (((0,), (0,)), ((), ())), preferred_element_type=F32)


def _dot_f32(a, b):
    return jnp.dot(a, b, precision=HIGHEST, preferred_element_type=F32)


def _dot_nt_f32(a, b):
    return lax.dot_general(a, b, (((1,), (1,)), ((), ())), precision=HIGHEST, preferred_element_type=F32)


def _pick_tile(n, cands):
    for c in cands:
        if n % c == 0:
            return c
    raise ValueError(f"no tile for {n}")


def _layer_norm(y, g, b):
    mu = jnp.mean(y, axis=-1, keepdims=True)
    yc = y - mu
    var = jnp.mean(yc * yc, axis=-1, keepdims=True)
    return yc * lax.rsqrt(var + LN_EPS) * g + b


def _inproj_kernel(x_ref, *refs, n_plain, t_split, n_carried):
    n_w = n_plain + (1 if t_split else 0)
    w_refs, b_refs, o_refs = refs[:n_plain], refs[n_w:n_w + n_plain], refs[2 * n_w + n_carried:]
    x = x_ref[...].astype(BF16)
    for w_ref, b_ref, o_ref in zip(w_refs, b_refs, o_refs):
        o_ref[...] = (jnp.dot(x, w_ref[...], preferred_element_type=F32) + b_ref[...]).astype(o_ref.dtype)
    if t_split:
        wt_ref, bt_ref = refs[n_plain], refs[n_w + n_plain]
        zt = lax.dot_general(wt_ref[...], x, (((1,), (1,)), ((), ())), preferred_element_type=F32) + bt_ref[...]
        lo_ref, hi_ref, all16_ref = o_refs[n_plain:]
        lo_ref[0] = zt[:t_split]
        hi_ref[0] = zt[t_split:]
        all16_ref[0] = zt.astype(BF16)


def _carry_args(carried):
    if carried is None:
        return [], [], 0
    return [carried], [pl.BlockSpec(memory_space=pl.ANY)], 1


def _inproj(x, ws, bs, wt=None, bt=None, t_split=0, n_seq=1, layer=0, depth=1, carried=None, dtypes=None):
    n_tok, d = x.shape
    t = n_tok // n_seq
    tm = _pick_tile(t, (256, 128, 64, 32, 16, 8))
    nt = t // tm
    full = lambda i: (0, 0)
    row = lambda i: (i, 0)
    tspec = lambda n, base: pl.BlockSpec((1, n, tm), lambda i: (base + i // nt, 0, i % nt))
    extra_w = [] if wt is None else [wt]
    extra_b = [] if wt is None else [bt]
    n_t = 0 if wt is None else wt.shape[0]
    t_specs = [] if wt is None else [tspec(t_split, layer * n_seq), tspec(n_t - t_split, 0), tspec(n_t, 0)]
    t_shapes = [] if wt is None else [jax.ShapeDtypeStruct((depth * n_seq, t_split, t), F32),
                                      jax.ShapeDtypeStruct((n_seq, n_t - t_split, t), F32),
                                      jax.ShapeDtypeStruct((n_seq, n_t, t), BF16)]
    c_args, c_specs, n_carried = _carry_args(carried)
    n_in = 1 + 2 * (len(ws) + len(extra_w))
    return pl.pallas_call(
        functools.partial(_inproj_kernel, n_plain=len(ws), t_split=t_split if wt is not None else 0,
                          n_carried=n_carried),
        grid=(n_tok // tm,),
        in_specs=[pl.BlockSpec((tm, d), row)]
        + [pl.BlockSpec(w.shape, full) for w in (*ws, *extra_w)]
        + [pl.BlockSpec(b.shape, full) for b in (*bs, *extra_b)] + c_specs,
        out_specs=[pl.BlockSpec((tm, w.shape[1]), row) for w in ws] + t_specs,
        out_shape=[jax.ShapeDtypeStruct((n_tok, w.shape[1]), dt) for w, dt in zip(ws, dtypes or [F32] * len(ws))]
        + t_shapes,
        input_output_aliases={n_in: len(ws)} if n_carried else {},
        compiler_params=_cparams("parallel"),
    )(x, *ws, *extra_w, *bs, *extra_b, *c_args)


def _log_sigmoid(x):
    return jnp.minimum(x, 0.0) - jnp.log(1.0 + jnp.exp(-jnp.abs(x)))


def _mlstm_kernel(q_ref, k_ref, v_ref, o_ref, g_ref, ng_ref, c0_ref, n0_ref, m0_ref, *rest, L, l_valid):
    mout_ref, c_ref, n_ref, m_ref, cs, ns, ms = rest[-7:]
    c = pl.program_id(1)

    @pl.when(c == 0)
    def _():
        cs[...] = c0_ref[0]
        ns[...] = n0_ref[0]
        ms[...] = m0_ref[0]

    g = g_ref[...]
    row = lax.broadcasted_iota(jnp.int32, (L, L), 0)
    col = lax.broadcasted_iota(jnp.int32, (L, L), 1)
    causal = row >= col
    fcum = _dot_f32(causal.astype(F32), _log_sigmoid(g))
    lane = lax.broadcasted_iota(jnp.int32, (L, LANES), 1)
    y = jnp.where(lane < H_M, g - pltpu.roll(fcum, LANES - H_M, 1), fcum)
    yt = y.T
    rowv = lax.broadcasted_iota(jnp.int32, (L, LANES), 0)
    r = l_valid - 1
    wide = (lambda x: jnp.concatenate([x] * (L // LANES), axis=1)) if L >= LANES else (lambda x: x[:, :L])
    for h in range(H_M):
        hs = slice(h * DH_M, (h + 1) * DH_M)
        f_col = jnp.broadcast_to(fcum[:, H_M + h:H_M + h + 1], (L, LANES))
        a_row = yt[h:h + 1, :]
        i_col = jnp.broadcast_to(g[:, h:h + 1], (L, LANES))
        m0 = ms[h]
        n0 = ns[h]
        c0 = cs[h]
        dm = jnp.where(causal, wide(f_col) + a_row, NEG)
        b = f_col + m0
        mrow = jnp.maximum(b, jnp.max(dm, axis=1, keepdims=True))
        w = jnp.exp(dm - wide(mrow))
        dec = jnp.exp(b - mrow)
        q = q_ref[:, hs]
        k = k_ref[:, hs] * (DH_M ** -0.5)
        v = v_ref[:, hs]
        s = _dot_nt(q, k) * w
        num = dec * _dot_nt(q, c0) + _dot(s, v)
        den = dec * jnp.sum(q * n0, axis=1, keepdims=True) + jnp.sum(s, axis=1, keepdims=True)
        hh = num / jnp.maximum(jnp.abs(den), jnp.exp(-mrow))
        mu = jnp.mean(hh, axis=1, keepdims=True)
        hc = hh - mu
        var = jnp.mean(hc * hc, axis=1, keepdims=True)
        hn = hc * lax.rsqrt(var + LN_EPS) * ng_ref[:, hs]
        mout_ref[:, hs] = (hn * jax.nn.sigmoid(o_ref[:, hs])).astype(mout_ref.dtype)
        f_r = f_col[r:r + 1, :]
        m_r = mrow[r:r + 1, :]
        w_last = jnp.where(rowv <= r, jnp.exp(f_r - f_col + i_col - m_r), 0.0)
        d_last = dec[r:r + 1, :]
        cs[h] = d_last * c0 + _dot_tn(v * w_last, k)
        ns[h] = d_last * n0 + jnp.sum(k * w_last, axis=0, keepdims=True)
        ms[h] = m_r

    @pl.when(c == pl.num_programs(1) - 1)
    def _():
        c_ref[0] = cs[...]
        n_ref[0] = ns[...]
        m_ref[0] = ms[...]


def _mlstm(zm, zg, norm_g, c0, n0, m0, *, n_seq, L, l_valid, c_base=0, c_layers=1, carried=None):
    n_tok = zm.shape[0]
    t = n_tok // n_seq
    nc = t // L
    n0 = n0.reshape(n_seq, H_M, 1, DH_M)
    m0 = jnp.broadcast_to(m0.reshape(n_seq, H_M, 1, 1), (n_seq, H_M, 1, LANES))
    colblk = lambda j: (lambda b, c: (b * nc + c, j))
    st4 = lambda b, c: (b, 0, 0, 0)
    c_spec = pl.BlockSpec((1, H_M, DH_M, DH_M), lambda b, c: (c_base + b, 0, 0, 0))
    c_args, c_specs, n_carried = _carry_args(carried)
    mout, c_new, n_new, m_new = pl.pallas_call(
        functools.partial(_mlstm_kernel, L=L, l_valid=l_valid),
        grid=(n_seq, nc),
        in_specs=[pl.BlockSpec((L, D_M), colblk(0)), pl.BlockSpec((L, D_M), colblk(1)),
                  pl.BlockSpec((L, D_M), colblk(2)), pl.BlockSpec((L, D_M), colblk(3)),
                  pl.BlockSpec((L, LANES), colblk(0)),
                  pl.BlockSpec((1, D_M), lambda b, c: (0, 0)),
                  c_spec,
                  pl.BlockSpec((1, H_M, 1, DH_M), st4),
                  pl.BlockSpec((1, H_M, 1, LANES), st4)] + c_specs,
        out_specs=[pl.BlockSpec((L, D_M), colblk(0)),
                   c_spec,
                   pl.BlockSpec((1, H_M, 1, DH_M), st4),
                   pl.BlockSpec((1, H_M, 1, LANES), st4)],
        out_shape=[jax.ShapeDtypeStruct((n_tok, D_M), BF16),
                   jax.ShapeDtypeStruct((c_layers * n_seq, H_M, DH_M, DH_M), F32),
                   jax.ShapeDtypeStruct((n_seq, H_M, 1, DH_M), F32),
                   jax.ShapeDtypeStruct((n_seq, H_M, 1, LANES), F32)],
        scratch_shapes=[pltpu.VMEM((H_M, DH_M, DH_M), F32), pltpu.VMEM((H_M, 1, DH_M), F32),
                        pltpu.VMEM((H_M, 1, LANES), F32)],
        input_output_aliases={9: 1} if n_carried else {},
        compiler_params=_cparams("parallel", "arbitrary"),
    )(zm, zm, zm, zm, zg, norm_g.reshape(1, D_M), c0, n0, m0, *c_args)
    return mout, c_new, n_new.reshape(n_seq, H_M, DH_M), m_new[:, :, 0, 0]


def _cmp_partial_kernel(x_ref, w_ref, o_ref):
    @pl.when(pl.program_id(2) == 0)
    def _():
        o_ref[...] = jnp.zeros_like(o_ref)

    o_ref[0] += _dot(x_ref[...], w_ref[0, 0])


def _cmp_partial(rows16, w1p):
    n_half = rows16.shape[0]
    tm = _pick_tile(n_half, (2048, 1024, 512, 256, 128, 64, 32, 16, 8))
    hid2 = 2 * HKV * CMP_HID
    return pl.pallas_call(
        _cmp_partial_kernel,
        grid=(2, n_half // tm, CMP_STRIDE),
        in_specs=[pl.BlockSpec((tm, LANES), lambda s, i, r: (i, 2 * r + s)),
                  pl.BlockSpec((1, 1, LANES, hid2), lambda s, i, r: (s, r, 0, 0))],
        out_specs=pl.BlockSpec((1, tm, hid2), lambda s, i, r: (s, i, 0)),
        out_shape=jax.ShapeDtypeStruct((2, n_half, hid2), F32),
        compiler_params=_cparams("parallel", "parallel", "arbitrary"),
    )(rows16, w1p)


def _gelu_tanh(x):
    return 0.5 * x * (1.0 + jnp.tanh(math.sqrt(2.0 / math.pi) * (x + 0.044715 * (x * x * x))))


def _cmp_finish_kernel(p_ref, w2_ref, o_ref):
    half = HKV * CMP_HID
    for s in range(2):
        p = p_ref[s]
        n = p.shape[0]
        hid = _gelu_tanh(p[:, :half] + pltpu.roll(p[:, half:], n - 1, 0))
        o_ref[0, s] = _dot(hid, w2_ref[s])


def _cmp_finish_prompt(part, w2p, n_seq):
    n_half = part.shape[1] // n_seq
    hid2 = part.shape[2]
    return pl.pallas_call(
        _cmp_finish_kernel,
        grid=(n_seq,),
        in_specs=[pl.BlockSpec((2, n_half, hid2), lambda b: (0, b, 0)),
                  pl.BlockSpec(w2p.shape, lambda b: (0, 0, 0))],
        out_specs=pl.BlockSpec((1, 2, n_half, LANES), lambda b: (b, 0, 0, 0)),
        out_shape=jax.ShapeDtypeStruct((n_seq, 2, n_half, LANES), F32),
        compiler_params=_cparams("parallel"),
    )(part, w2p)


def _cmp_paged_kernel(*refs, n_pages, n_item, page):
    page_refs = refs[1:1 + n_item * n_pages]
    perm_ref, w1_ref, w2_ref, o_ref, tok = refs[1 + n_item * n_pages:]
    half_per_page = page // CMP_STRIDE
    n = n_pages * half_per_page
    half = HKV * CMP_HID
    for s in range(2):
        for i, pr in enumerate(page_refs):
            xt = _dot_nt(perm_ref[...], pr[0, s * LANES:(s + 1) * LANES, :])
            tok[:, i * half_per_page:(i + 1) * half_per_page, :] = xt.reshape(CMP_STRIDE, half_per_page, LANES)
        acc = None
        for r in range(0, CMP_STRIDE, 2):
            part = _dot(jnp.concatenate([tok[r], tok[r + 1]], axis=1), w1_ref[s, r // 2])
            acc = part if acc is None else acc + part
        hids = []
        for it in range(n_item):
            p = acc[it * n:(it + 1) * n]
            hids.append(_gelu_tanh(p[:, :half] + pltpu.roll(p[:, half:], n - 1, 0)))
        out = _dot(jnp.concatenate(hids, axis=0) if n_item > 1 else hids[0], w2_ref[s])
        for it in range(n_item):
            o_ref[it, s] = out[it * n:(it + 1) * n]


def _cmp_paged(cache_t, w1p, w2p, page_table, page_base):
    n_seq, n_pages = page_table.shape
    page = cache_t.shape[2]
    half_per_page = page // CMP_STRIDE
    n_half = n_pages * half_per_page
    n_item = _pick_tile(n_seq, (4, 2, 1))
    tok = np.arange(page)
    perm = np.zeros((page, page), np.float32)
    perm[(tok % CMP_STRIDE) * half_per_page + tok // CMP_STRIDE, tok] = 1.0
    page_spec = lambda it, p: pl.BlockSpec((1, 2 * LANES, page),
                                           lambda b, pt: (page_base + pt[b * n_item + it, p], 0, 0))
    return pl.pallas_call(
        functools.partial(_cmp_paged_kernel, n_pages=n_pages, n_item=n_item, page=page),
        grid_spec=pltpu.PrefetchScalarGridSpec(
            num_scalar_prefetch=1, grid=(n_seq // n_item,),
            in_specs=[page_spec(it, p) for it in range(n_item) for p in range(n_pages)]
            + [pl.BlockSpec((page, page), lambda b, pt: (0, 0)),
               pl.BlockSpec(w1p.shape, lambda b, pt: (0, 0, 0, 0)),
               pl.BlockSpec(w2p.shape, lambda b, pt: (0, 0, 0))],
            out_specs=pl.BlockSpec((n_item, 2, n_half, LANES), lambda b, pt: (b, 0, 0, 0)),
            scratch_shapes=[pltpu.VMEM((CMP_STRIDE, n_item * n_half, LANES), F32)]),
        out_shape=jax.ShapeDtypeStruct((n_seq, 2, n_half, LANES), F32),
        compiler_params=_cparams("parallel"),
    )(page_table, *([cache_t] * (n_item * n_pages)), jnp.asarray(perm, dtype=BF16), w1p, w2p)


def _cmp_branch(q, ck, cv, bias_ref, ocmp_ref):
    nq = q.shape[0]
    qpad = jnp.concatenate([_padded_queries(q, g) for g in range(HKV)], axis=0)
    bias = bias_ref[...].reshape(HQ * nq, -1)
    s = _dot_nt(qpad, ck) + bias
    e = jnp.exp(s - jnp.max(s, axis=1, keepdims=True))
    p = jnp.where(bias > 0.5 * NEG, e / jnp.sum(e, axis=1, keepdims=True), 0.0)
    _store_all_heads(ocmp_ref, _dot(p, cv), nq)
    psums = []
    for g in range(HKV):
        heads = [p[(g * REP + r) * nq:(g * REP + r + 1) * nq] for r in range(REP)]
        psums.append(functools.reduce(lambda a, b: a + b, heads))
    return psums


def _select_blocks(imp, qpos, blk_axis, n_blocks):
    blk = lax.broadcasted_iota(jnp.int32, imp.shape, blk_axis)
    cur = (qpos // SLC_LEN) == blk
    avail = blk * SLC_LEN <= qpos
    imp = jnp.where(cur, -NEG, jnp.where(avail, imp, NEG))
    cnt = jnp.zeros(imp.shape, F32)
    for j in range(n_blocks):
        other = lax.slice_in_dim(imp, j, j + 1, axis=blk_axis)
        cnt = cnt + jnp.where(blk > j, jnp.where(other >= imp, 1.0, 0.0), jnp.where(other > imp, 1.0, 0.0))
    return jnp.where((cnt < N_SEL) & (imp > 0.5 * NEG), 1.0, 0.0)


def _padded_queries(q, g, dtype=BF16):
    nq = q.shape[0]
    zero = jnp.zeros((nq, HD), F32)
    parts = []
    for r in range(REP):
        h = g * REP + r
        piece = q[:, h * HD:(h + 1) * HD]
        parts.append(jnp.concatenate([piece, zero] if g == 0 else [zero, piece], axis=1))
    return jnp.concatenate(parts, axis=0).astype(dtype)


def _store_all_heads(o_ref, o, nq):
    for h in range(HQ):
        g = h // REP
        o_ref[:, h * HD:(h + 1) * HD] = o[h * nq:(h + 1) * nq, g * HD:(g + 1) * HD].astype(o_ref.dtype)


def _nsa_cmp_prompt_kernel(q_ref, ckv_ref, bias_ref, ovt_ref, ocmp_ref, sel_ref):
    qb = pl.program_id(1)
    q = q_ref[...] * (HD ** -0.5)
    ck, cv = ckv_ref[0, 0], ckv_ref[0, 1]
    qpos = qb * QT + lax.broadcasted_iota(jnp.int32, (1, QT), 1)
    for g, psum in enumerate(_cmp_branch(q, ck, cv, bias_ref, ocmp_ref)):
        imp_t = _dot_nt_f32(ovt_ref[...], psum)
        sel_ref[:, g * NSP:(g + 1) * NSP] = _select_blocks(imp_t, qpos, 0, NSP).T


def _nsa_cmp_prompt(zq, ckv, bias_cmp, ov_t, n_seq):
    n_tok = zq.shape[0]
    nqb = n_tok // n_seq // QT
    ncp = ckv.shape[2]
    tok = lambda b, i: (b * nqb + i, 0)
    return pl.pallas_call(
        _nsa_cmp_prompt_kernel,
        grid=(n_seq, nqb),
        in_specs=[pl.BlockSpec((QT, D_A), tok),
                  pl.BlockSpec((1, 2, ncp, LANES), lambda b, i: (b, 0, 0, 0)),
                  pl.BlockSpec((HQ, QT, ncp), lambda b, i: (0, i, 0)),
                  pl.BlockSpec(ov_t.shape, lambda b, i: (0, 0))],
        out_specs=[pl.BlockSpec((QT, D_A), tok), pl.BlockSpec((QT, HKV * NSP), tok)],
        out_shape=[jax.ShapeDtypeStruct((n_tok, D_A), BF16), jax.ShapeDtypeStruct((n_tok, HKV * NSP), F32)],
        compiler_params=_cparams("parallel", "parallel"),
    )(zq, ckv, bias_cmp, ov_t)


def _flash_prompt_kernel(q_ref, kt_ref, vt_ref, bias_ref, sel_ref, o_ref, s_scr, p_scr, m_scr, l_scr, a_scr, acc_scr,
                         *, n_delta, kt_len):
    g = pl.program_id(0)
    qb = pl.program_id(2)
    q = q_ref[...] * (HD ** -0.5 * LOG2E)
    sel = sel_ref[...]
    unpicked = (1.0 - jnp.where(g == 0, sel[:, :NSP], sel[:, NSP:])) * NEG
    parts = [jnp.concatenate([q[:, r * HD:(r + 1) * HD], unpicked], axis=1) for r in range(REP)]
    qrows = jnp.concatenate(parts, axis=0).astype(BF16)
    hi = (qb * QT) // kt_len + 1

    m_scr[...] = jnp.full(m_scr.shape, NEG, F32)
    l_scr[...] = jnp.zeros(l_scr.shape, F32)
    acc_scr[...] = jnp.zeros(acc_scr.shape, F32)

    def body(kt, carry):
        k0 = pl.multiple_of(kt * kt_len, kt_len)
        delta = jnp.minimum(qb - (kt_len // QT) * kt, n_delta - 1)
        blk = lax.broadcasted_iota(jnp.int32, (NSP, kt_len), 0)
        key = lax.broadcasted_iota(jnp.int32, (NSP, kt_len), 1)
        one_hot = jnp.where(blk == (k0 + key) // SLC_LEN, 1.0, 0.0).astype(BF16)
        k_t = jnp.concatenate([kt_ref[0, :, pl.ds(k0, kt_len)], one_hot], axis=0)
        s_scr[...] = jnp.dot(qrows, k_t, preferred_element_type=F32)
        for c in range(REP * QT // ROWS):
            r, qc = divmod(c, QT // ROWS)
            rows = pl.ds(c * ROWS, ROWS)
            s = s_scr[rows, :] + bias_ref[delta, r, pl.ds(qc * ROWS, ROWS), :]
            m_old = m_scr[rows, :]
            m_new = jnp.maximum(m_old, jnp.max(s, axis=1, keepdims=True))
            p = jnp.exp2(s - jnp.concatenate([m_new] * (kt_len // LANES), axis=1))
            alpha = jnp.exp2(m_old - m_new)
            l_scr[rows, :] = alpha * l_scr[rows, :] + jnp.sum(p, axis=1, keepdims=True)
            m_scr[rows, :] = m_new
            a_scr[rows, :] = alpha
            p_scr[rows, :] = p.astype(BF16)
        acc_scr[...] = a_scr[...] * acc_scr[...] + lax.dot_general(
            p_scr[...], vt_ref[0, :, pl.ds(k0, kt_len)], (((1,), (1,)), ((), ())), preferred_element_type=F32)
        return carry

    lax.fori_loop(0, hi, body, 0)
    o = acc_scr[...] / l_scr[...]
    for r in range(REP):
        o_r = o[r * QT:(r + 1) * QT]
        o_ref[:, r * HD:(r + 1) * HD] = jnp.where(g == 0, o_r[:, :HD], o_r[:, HD:]).astype(o_ref.dtype)


def _flash_prompt(zq, kv_t, kslot, vslot, bias, sel, n_seq):
    n_tok = zq.shape[0]
    t = kv_t.shape[2]
    nqb = t // QT
    kt_len = bias.shape[3]
    rows = REP * QT
    return pl.pallas_call(
        functools.partial(_flash_prompt_kernel, n_delta=bias.shape[0], kt_len=kt_len),
        grid=(HKV, n_seq, nqb),
        in_specs=[pl.BlockSpec((QT, REP * HD), lambda g, b, i: (b * nqb + i, g)),
                  pl.BlockSpec((1, HD, t), lambda g, b, i: (b, HKV * kslot + g, 0)),
                  pl.BlockSpec((1, LANES, t), lambda g, b, i: (b, vslot, 0)),
                  pl.BlockSpec((bias.shape[0], REP, QT, kt_len), lambda g, b, i: (0, g, 0, 0)),
                  pl.BlockSpec((QT, HKV * NSP), lambda g, b, i: (b * nqb + i, 0))],
        out_specs=pl.BlockSpec((QT, REP * HD), lambda g, b, i: (b * nqb + i, g)),
        out_shape=jax.ShapeDtypeStruct((n_tok, D_A), BF16),
        scratch_shapes=[pltpu.VMEM((rows, kt_len), F32), pltpu.VMEM((rows, kt_len), BF16),
                        pltpu.VMEM((rows, LANES), F32), pltpu.VMEM((rows, LANES), F32),
                        pltpu.VMEM((rows, LANES), F32), pltpu.VMEM((rows, LANES), F32)],
        compiler_params=_cparams("parallel", "parallel", "parallel"),
    )(zq, kv_t, kv_t, bias, sel)


def _window_prompt_kernel(q_ref, kt_ref, vt_ref, bias_a_ref, bias_b_ref, o_ref, s_a, s_b, p_a, p_b, l_scr):
    g = pl.program_id(0)
    qb = pl.program_id(2)
    n_back = WINDOW // QT
    q = q_ref[...] * (HD ** -0.5 * LOG2E)
    qrows = jnp.concatenate([q[:, r * HD:(r + 1) * HD] for r in range(REP)], axis=0).astype(BF16)
    a0 = pl.multiple_of(jnp.maximum(qb - n_back, 0) * QT, QT)
    b0 = pl.multiple_of(qb * QT, QT)
    delta_a = jnp.minimum(qb, n_back)
    tile_b = jnp.where(qb >= n_back, 0, 1)
    s_a[...] = jnp.dot(qrows, kt_ref[0, :, pl.ds(a0, WINDOW)], preferred_element_type=F32)
    s_b[...] = jnp.dot(qrows, kt_ref[0, :, pl.ds(b0, QT)], preferred_element_type=F32)
    for c in range(REP * QT // ROWS):
        r, qc = divmod(c, QT // ROWS)
        rows, qrows_c = pl.ds(c * ROWS, ROWS), pl.ds(qc * ROWS, ROWS)
        sa = s_a[rows, :] + bias_a_ref[delta_a, r, qrows_c, :]
        sb = s_b[rows, :] + bias_b_ref[tile_b, r, qrows_c, :]
        fold = lambda x, y, op: functools.reduce(op, [x[:, i:i + LANES] for i in range(0, WINDOW, LANES)] + [y])
        m = jnp.broadcast_to(jnp.max(fold(sa, sb, jnp.maximum), axis=1, keepdims=True), (ROWS, LANES))
        pa = jnp.exp2(sa - jnp.concatenate([m] * (WINDOW // LANES), axis=1))
        pb = jnp.exp2(sb - m)
        l_scr[rows, :] = jnp.broadcast_to(jnp.sum(fold(pa, pb, jnp.add), axis=1, keepdims=True), (ROWS, LANES))
        p_a[rows, :] = pa.astype(BF16)
        p_b[rows, :] = pb.astype(BF16)
    nt = (((1,), (1,)), ((), ()))
    o = (lax.dot_general(p_a[...], vt_ref[0, :, pl.ds(a0, WINDOW)], nt, preferred_element_type=F32)
         + lax.dot_general(p_b[...], vt_ref[0, :, pl.ds(b0, QT)], nt, preferred_element_type=F32)) / l_scr[...]
    for r in range(REP):
        o_r = o[r * QT:(r + 1) * QT]
        o_ref[:, r * HD:(r + 1) * HD] = jnp.where(g == 0, o_r[:, :HD], o_r[:, HD:]).astype(o_ref.dtype)


def _window_prompt(zq, kv_t, kslot, vslot, bias_a, bias_b, n_seq):
    n_tok = zq.shape[0]
    t = kv_t.shape[2]
    nqb = t // QT
    rows = REP * QT
    return pl.pallas_call(
        _window_prompt_kernel,
        grid=(HKV, n_seq, nqb),
        in_specs=[pl.BlockSpec((QT, REP * HD), lambda g, b, i: (b * nqb + i, g)),
                  pl.BlockSpec((1, HD, t), lambda g, b, i: (b, HKV * kslot + g, 0)),
                  pl.BlockSpec((1, LANES, t), lambda g, b, i: (b, vslot, 0)),
                  pl.BlockSpec((bias_a.shape[0], REP, QT, WINDOW), lambda g, b, i: (0, g, 0, 0)),
                  pl.BlockSpec((bias_b.shape[0], REP, QT, QT), lambda g, b, i: (0, g, 0, 0))],
        out_specs=pl.BlockSpec((QT, REP * HD), lambda g, b, i: (b * nqb + i, g)),
        out_shape=jax.ShapeDtypeStruct((n_tok, D_A), BF16),
        scratch_shapes=[pltpu.VMEM((rows, WINDOW), F32), pltpu.VMEM((rows, QT), F32), pltpu.VMEM((rows, WINDOW), BF16),
                        pltpu.VMEM((rows, QT), BF16), pltpu.VMEM((rows, LANES), F32)],
        compiler_params=_cparams("parallel", "parallel", "parallel"),
    )(zq, kv_t, kv_t, bias_a, bias_b)


def _nsa_sample_kernel(*refs, n_pages, nq, n_new, past, n_item, n_carried):
    n_in = 1 + n_item * n_pages
    all_pages = refs[1:n_in]
    q_ref, kvn_ref, wn_ref, wbuf_ref, ckv_ref, bcmp_ref, bslc_ref, bwin_ref, ov_ref, exp_ref = refs[n_in:n_in + 10]
    ocmp_ref, oslc_ref, owin_ref, wout_ref = refs[n_in + 10 + n_carried:n_in + 14 + n_carried]
    scratch = refs[n_in + 14 + n_carried:]
    for it in range(n_item):
        _nsa_sample_item(all_pages[it * n_pages:(it + 1) * n_pages], q_ref.at[it], kvn_ref.at[it], wn_ref.at[it],
                         wbuf_ref.at[it], ckv_ref.at[it], bcmp_ref, bslc_ref, bwin_ref, ov_ref, exp_ref,
                         ocmp_ref.at[it], oslc_ref.at[it], owin_ref.at[it], wout_ref.at[it],
                         scratch[2 * it], scratch[2 * it + 1], nq=nq, n_new=n_new, past=past)


def _nsa_sample_item(page_refs, q_ref, kvn_ref, wn_ref, wbuf_ref, ckv_ref, bcmp_ref, bslc_ref, bwin_ref, ov_ref,
                     exp_ref, ocmp_ref, oslc_ref, owin_ref, wout_ref, new_kv, new_w, *, nq, n_new, past):
    new_kv[nq:, :] = jnp.zeros((LANES - nq, 4 * LANES), F32)
    new_kv[:nq, :] = kvn_ref[...]
    new_w[nq:, :] = jnp.zeros((LANES - nq, 2 * LANES), F32)
    new_w[:nq, :] = wn_ref[...]
    win_buf = wbuf_ref.shape[1]
    shifted = pltpu.roll(wbuf_ref[...], win_buf - n_new, 1)
    new_t = jnp.concatenate([new_w[:, :LANES].T, new_w[:, LANES:].T], axis=0)
    new_t = pltpu.roll(new_t, LANES - n_new, 1)
    tail = lax.broadcasted_iota(jnp.int32, (2 * LANES, LANES), 1) >= LANES - n_new
    wout_ref[:, :win_buf - LANES] = shifted[:, :win_buf - LANES]
    wout_ref[:, win_buf - LANES:] = jnp.where(tail, new_t, shifted[:, win_buf - LANES:])

    q = q_ref[...] * (HD ** -0.5)
    ck, cv = ckv_ref[0], ckv_ref[1]
    qpos = past + lax.broadcasted_iota(jnp.int32, (nq, 1), 0)

    def attend(qpad, old_k, old_v, new_k, new_v, bias):
        s = jnp.concatenate([jnp.dot(qpad, k_t.astype(BF16), preferred_element_type=F32) for k_t in old_k]
                            + [_dot_nt(qpad, new_k)], axis=1) + bias
        e = jnp.exp(s - jnp.max(s, axis=1, keepdims=True))
        o, c0 = _dot(e[:, s.shape[1] - LANES:], new_v), 0
        for v_t in old_v:
            o = o + _dot_nt(e[:, c0:c0 + v_t.shape[1]], v_t)
            c0 += v_t.shape[1]
        return o / jnp.sum(e, axis=1, keepdims=True)

    picked = []
    n_blocks = -(-(past + nq) // SLC_LEN)
    for psum in _cmp_branch(q, ck, cv, bcmp_ref, ocmp_ref):
        sel = _select_blocks(_dot_f32(psum, ov_ref[...]), qpos, 1, n_blocks)
        picked += [jnp.dot(sel.astype(BF16), exp_ref[...], preferred_element_type=F32)] * REP
    qpad = jnp.concatenate([_padded_queries(q, g) for g in range(HKV)], axis=0)
    bslc = jnp.where(jnp.concatenate(picked, axis=0) > 0.5, bslc_ref[...].reshape(HQ * nq, -1), NEG)
    o = attend(qpad, [pr[0, 2 * LANES:3 * LANES, :] for pr in page_refs],
               [pr[0, 3 * LANES:4 * LANES, :] for pr in page_refs],
               new_kv[:, 2 * LANES:3 * LANES], new_kv[:, 3 * LANES:4 * LANES], bslc)
    _store_all_heads(oslc_ref, o, nq)
    o = attend(qpad, [wbuf_ref[:LANES, :]], [wbuf_ref[LANES:, :]], new_w[:, :LANES], new_w[:, LANES:],
               bwin_ref[...].reshape(HQ * nq, -1))
    _store_all_heads(owin_ref, o, nq)


def _nsa_sample(zq, kvn, wn, n_new, cache_t, page_base, page_table, win_t, win_base, ckv,
                bias_cmp, bias_slc, bias_win, ov, expand, carried):
    n_seq, nq, _ = zq.shape
    n_pages = page_table.shape[1]
    page = cache_t.shape[2]
    past = n_pages * page
    win_buf = win_t.shape[2]
    ncp = ckv.shape[2]
    n_item = _pick_tile(n_seq, (4, 2, 1))
    seq3 = lambda b, pt: (b, 0, 0)
    const = lambda nd: (lambda b, pt: (0,) * nd)
    page_spec = lambda it, p: pl.BlockSpec((1, 4 * LANES, page),
                                           lambda b, pt: (page_base + pt[b * n_item + it, p], 0, 0))
    in_specs = [page_spec(it, p) for it in range(n_item) for p in range(n_pages)] + [
        pl.BlockSpec((n_item, nq, D_A), seq3),
        pl.BlockSpec((n_item, nq, 4 * LANES), seq3),
        pl.BlockSpec((n_item, nq, 2 * LANES), seq3),
        pl.BlockSpec((n_item, 2 * LANES, win_buf), lambda b, pt: (win_base // n_item + b, 0, 0)),
        pl.BlockSpec((n_item, 2, ncp, LANES), lambda b, pt: (b, 0, 0, 0)),
        pl.BlockSpec(bias_cmp.shape, const(3)),
        pl.BlockSpec(bias_slc.shape, const(3)),
        pl.BlockSpec(bias_win.shape, const(3)),
        pl.BlockSpec(ov.shape, const(2)),
        pl.BlockSpec(expand.shape, const(2))]
    out = pl.BlockSpec((n_item, nq, D_A), seq3)
    wout = pl.BlockSpec((n_item, 2 * LANES, win_buf), lambda b, pt: (win_base // n_item + b, 0, 0))
    c_args, c_specs, n_carried = _carry_args(carried)
    n_in = 1 + n_item * n_pages + 10
    return pl.pallas_call(
        functools.partial(_nsa_sample_kernel, n_pages=n_pages, nq=nq, n_new=n_new, past=past, n_item=n_item,
                          n_carried=n_carried),
        grid_spec=pltpu.PrefetchScalarGridSpec(
            num_scalar_prefetch=1, grid=(n_seq // n_item,), in_specs=in_specs + c_specs,
            out_specs=[out, out, out, wout],
            scratch_shapes=[pltpu.VMEM((LANES, 4 * LANES), F32), pltpu.VMEM((LANES, 2 * LANES), F32)] * n_item),
        out_shape=[jax.ShapeDtypeStruct((n_seq, nq, D_A), F32)] * 3 + [jax.ShapeDtypeStruct(win_t.shape, F32)],
        input_output_aliases={n_in: 3} if n_carried else {},
        compiler_params=_cparams("parallel"),
    )(page_table, *([cache_t] * (n_item * n_pages)), zq, kvn, wn, win_t, ckv, bias_cmp, bias_slc, bias_win, ov, expand,
      *c_args)


def _merge_kernel(x_ref, mout_ref, ocmp_ref, oslc_ref, owin_ref, zg_ref, zgm_ref, gexp_ref, wb_ref, wo_ref,
                  lg_ref, lb_ref, o_ref, *, alpha):
    d = x_ref.shape[1]
    gate = jax.nn.sigmoid(zg_ref[...])
    terms, rest = [], gate
    for _ in range(3):
        terms.append(rest.astype(BF16))
        rest = rest - terms[-1].astype(F32)
    spread = jnp.dot(jnp.concatenate(terms, axis=1), gexp_ref[...], preferred_element_type=F32)
    a = None
    for br, src in enumerate((ocmp_ref, oslc_ref, owin_ref)):
        term = spread[:, br * D_A:(br + 1) * D_A] * src[...]
        a = term if a is None else a + term
    u = (jax.nn.sigmoid(zgm_ref[:, :d].astype(F32)) * _dot(mout_ref[...], wb_ref[0])
         + jax.nn.sigmoid(zgm_ref[:, d:].astype(F32)) * _dot(a, wb_ref[1]))
    y = _dot(u, wo_ref[...])
    o_ref[...] = _layer_norm(alpha * x_ref[...] + y, lg_ref[...], lb_ref[...])


def _merge(x, mout, ocmp, oslc, owin, zg, zgm, gexp, wb, wo, lg, lb, alpha):
    n_tok, d = x.shape
    tm = _pick_tile(n_tok, (256, 128, 64, 32, 16, 8))
    row = lambda i: (i, 0)
    c2 = lambda i: (0, 0)
    c3 = lambda i: (0, 0, 0)
    return pl.pallas_call(
        functools.partial(_merge_kernel, alpha=alpha),
        grid=(n_tok // tm,),
        in_specs=[pl.BlockSpec((tm, d), row), pl.BlockSpec((tm, D_M), row), pl.BlockSpec((tm, D_A), row),
                  pl.BlockSpec((tm, D_A), row), pl.BlockSpec((tm, D_A), row), pl.BlockSpec((tm, LANES), row),
                  pl.BlockSpec((tm, 2 * d), row), pl.BlockSpec(gexp.shape, c2), pl.BlockSpec(wb.shape, c3),
                  pl.BlockSpec(wo.shape, c2), pl.BlockSpec((1, d), c2), pl.BlockSpec((1, d), c2)],
        out_specs=pl.BlockSpec((tm, d), row),
        out_shape=jax.ShapeDtypeStruct((n_tok, d), F32),
        compiler_params=_cparams("parallel"),
    )(x, mout, ocmp, oslc, owin, zg, zgm, gexp, wb, wo, lg.reshape(1, d), lb.reshape(1, d))


def _route(aff, sel):
    def top2_sum(a, b, c, d):
        x, x2, y, y2 = jnp.maximum(a, b), jnp.minimum(a, b), jnp.maximum(c, d), jnp.minimum(c, d)
        return jnp.maximum(x, y) + jnp.maximum(jnp.minimum(x, y), jnp.maximum(x2, y2))

    gsum = [top2_sum(*sel[EXP_PER_GROUP * gi:EXP_PER_GROUP * (gi + 1)]) for gi in range(N_GROUPS)]
    gmax = functools.reduce(jnp.maximum, gsum)
    chosen, taken = [], None
    for gi in range(N_GROUPS):
        is_best = gsum[gi] == gmax if taken is None else (gsum[gi] == gmax) & jnp.logical_not(taken)
        taken = is_best if taken is None else taken | is_best
        members = range(EXP_PER_GROUP * gi, EXP_PER_GROUP * (gi + 1))
        for e in members:
            ahead = None
            for e2 in members:
                if e2 == e:
                    continue
                before = (sel[e2] >= sel[e]) if e2 < e else (sel[e2] > sel[e])
                cnt = jnp.where(before, 1.0, 0.0)
                ahead = cnt if ahead is None else ahead + cnt
            chosen.append(is_best & (ahead < TOP_K))
    picked = [jnp.where(c, a, 0.0) for c, a in zip(chosen, aff)]
    total = functools.reduce(lambda a, b: a + b, picked)
    return [p / total for p in picked]


def _moe_kernel(x_ref, wr_ref, rb_ref, wg_ref, wu_ref, wd_ref, lg_ref, lb_ref, o_ref, xb, comb, acc, *, alpha):
    step = pl.program_id(1)
    tm = x_ref.shape[0]
    lane = lax.broadcasted_iota(jnp.int32, (tm, LANES), 1)

    @pl.when(step == 0)
    def _():
        x16 = x_ref[...].astype(BF16)
        xb[...] = x16
        aff_t = jax.nn.sigmoid(_dot_nt(wr_ref[...], x16))
        sel_t = aff_t + rb_ref[...]
        weights = _route([aff_t[i:i + 1, :] for i in range(N_EXP)], [sel_t[i:i + 1, :] for i in range(N_EXP)])
        expert = lax.broadcasted_iota(jnp.int32, (N_EXP, tm), 0)
        comb_t = jnp.zeros((N_EXP, tm), F32)
        for i, w in enumerate(weights):
            comb_t = jnp.where(expert == i, w, comb_t)
        comb[...] = jnp.concatenate([comb_t, jnp.zeros((LANES - N_EXP, tm), F32)], axis=0).T
        acc[...] = jnp.zeros_like(acc)

    x16 = xb[...]
    hs = []
    for j in range(EXP_PER_STEP):
        w_e = jnp.sum(jnp.where(lane == EXP_PER_STEP * step + j, comb[...], 0.0), axis=1, keepdims=True)
        hg = jnp.dot(x16, wg_ref[0, j], preferred_element_type=F32)
        hu = jnp.dot(x16, wu_ref[0, j], preferred_element_type=F32)
        hs.append((hg * jax.nn.sigmoid(hg) * hu * w_e).astype(BF16))
    d = acc.shape[1]
    acc[...] += jnp.dot(jnp.concatenate(hs, axis=1), wd_ref[0].reshape(EXP_PER_STEP * D_EXP, d),
                        preferred_element_type=F32)

    @pl.when(step == pl.num_programs(1) - 1)
    def _():
        o_ref[...] = _layer_norm(alpha * x_ref[...] + acc[...], lg_ref[...], lb_ref[...])


def _moe(x, wr, rb, wg, wu, wd, layer, lg, lb, alpha):
    n_tok, d = x.shape
    tm = _pick_tile(n_tok, (1024, 512, 256, 128, 64, 32, 16, 8))
    row = lambda i, e: (i, 0)
    c2 = lambda i, e: (0, 0)
    wsel = lambda i, e: (layer, e, 0, 0)
    return pl.pallas_call(
        functools.partial(_moe_kernel, alpha=alpha),
        grid=(n_tok // tm, N_EXP // EXP_PER_STEP),
        in_specs=[pl.BlockSpec((tm, d), row), pl.BlockSpec(wr.shape, c2), pl.BlockSpec(rb.shape, c2),
                  pl.BlockSpec((1, EXP_PER_STEP, d, D_EXP), wsel), pl.BlockSpec((1, EXP_PER_STEP, d, D_EXP), wsel),
                  pl.BlockSpec((1, EXP_PER_STEP, D_EXP, d), wsel), pl.BlockSpec((1, d), c2), pl.BlockSpec((1, d), c2)],
        out_specs=pl.BlockSpec((tm, d), row),
        out_shape=jax.ShapeDtypeStruct((n_tok, d), F32),
        scratch_shapes=[pltpu.VMEM((tm, d), BF16), pltpu.VMEM((tm, LANES), F32), pltpu.VMEM((tm, d), F32)],
        compiler_params=_cparams("parallel", "arbitrary"),
    )(x, wr, rb, wg, wu, wd, lg.reshape(1, d), lb.reshape(1, d))


def _bucket_np(dist):
    n = np.maximum(dist, 0)
    exact = N_BUCKETS // 2
    nf = np.maximum(n, 1).astype(np.float32)
    large = exact + (np.log(nf / np.float32(exact)) / np.float32(math.log(REL_MAX_DIST / exact))
                     * np.float32(N_BUCKETS - exact)).astype(np.int32)
    return np.where(n < exact, n, np.minimum(large, N_BUCKETS - 1)).astype(np.int32)


def _bias_table(rel_bias, dist, valid):
    tab = rel_bias.astype(F32)[jnp.asarray(_bucket_np(dist))]
    tab = jnp.where(jnp.asarray(valid)[..., None], tab, NEG)
    return jnp.moveaxis(tab, -1, 0)


def _bias_tiles_kernel(rb_ref, o_ref, *, tile_step, key_stride, key_off, hi_valid, upper, lead):
    t = pl.program_id(0)
    shape = o_ref.shape[-2:]
    dist = (tile_step * t - key_off + lax.broadcasted_iota(jnp.int32, shape, 0)
            - key_stride * lax.broadcasted_iota(jnp.int32, shape, 1))
    valid = dist >= 0 if hi_valid is None else (dist >= 0) & (dist < hi_valid)
    acc = [jnp.full(shape, rb_ref[N_BUCKETS - 1, h], F32) for h in range(HQ)]
    for b in range(N_BUCKETS - 2, -1, -1):
        below = dist < upper[b]
        acc = [jnp.where(below, rb_ref[b, h], a) for h, a in enumerate(acc)]
    for h in range(HQ):
        if lead:
            o_ref[0, h] = jnp.where(valid, acc[h], NEG)
        else:
            o_ref[h] = jnp.where(valid, acc[h], NEG)


def _bias_tiles(rel_bias, n_tiles, n_keys, *, tile_step, key_stride, key_off, hi_valid, lead, scale=1.0):
    buckets = _bucket_np(np.arange(8 * REL_MAX_DIST))
    upper = tuple(int(np.searchsorted(buckets, b, side='right')) for b in range(N_BUCKETS - 1))
    if lead:
        out_spec = pl.BlockSpec((1, HQ, QT, n_keys), lambda t: (t, 0, 0, 0))
        out_shape = jax.ShapeDtypeStruct((n_tiles, HQ, QT, n_keys), F32)
    else:
        out_spec = pl.BlockSpec((HQ, QT, n_keys), lambda t: (0, t, 0))
        out_shape = jax.ShapeDtypeStruct((HQ, n_tiles * QT, n_keys), F32)
    return pl.pallas_call(
        functools.partial(_bias_tiles_kernel, tile_step=tile_step, key_stride=key_stride, key_off=key_off,
                          hi_valid=hi_valid, upper=upper, lead=lead),
        grid=(n_tiles,),
        in_specs=[pl.BlockSpec(memory_space=pltpu.SMEM)],
        out_specs=out_spec, out_shape=out_shape,
        compiler_params=_cparams("parallel"),
    )(rel_bias.astype(F32) * scale)


def _overlap_np(ncp, n_cmp, n_slc):
    c0 = np.arange(ncp)[:, None] * CMP_STRIDE
    j0 = np.arange(NSP)[None, :] * SLC_LEN
    ov = (c0 < j0 + SLC_LEN) & (c0 + CMP_LEN > j0)
    ov &= (np.arange(ncp)[:, None] < n_cmp) & (np.arange(NSP)[None, :] < n_slc)
    return ov.astype(np.float32)


def kernel(x_prompt, x_sample, cache_kv, state_win_kv, state_mlstm_C, state_mlstm_n, state_mlstm_m, page_table,
           w_in, b_in, w_cmp1, w_cmp2, mh_norm_g, w_branch, w_out, ln1_g, ln1_b, ln2_g, ln2_b,
           w_router, router_bias, w_gate_e, w_up_e, w_down_e, rel_bias):
    B, T, D = x_prompt.shape
    DB, TS, _ = x_sample.shape
    depth, n_pool, page = cache_kv.shape[:3]
    n_pages = page_table.shape[1]
    past = n_pages * page
    win_buf = state_win_kv.shape[2]
    alpha = (2 * depth) ** 0.25
    assert T % KT == 0 and T // SLC_LEN <= NSP and page % CMP_STRIDE == 0 and past % SLC_LEN == 0
    assert win_buf == min(WINDOW, past) and T >= win_buf

    off = np.cumsum((0, D_M, D_M, D_M, D_M, H_M, H_M, D_A, 6 * HKV * HD, 3 * HQ, 2 * D)).tolist()
    seg = lambda a, i, j: a[..., off[i]:off[j]]
    small = lambda a: jnp.concatenate(
        [seg(a, 4, 6), seg(a, 8, 9), jnp.zeros(a.shape[:-1] + (LANES - 2 * H_M - 3 * HQ,), a.dtype)], axis=-1)
    kv_cmp, kv_mid = off[7] + 2 * HKV * HD, off[7] + 4 * HKV * HD
    groups_s = lambda a: (seg(a, 0, 4), small(a), seg(a, 6, 7), a[..., off[7]:kv_mid], a[..., kv_mid:off[8]], seg(a, 9, 10))
    groups_p = lambda a: (seg(a, 0, 4), small(a), seg(a, 6, 7), a[..., off[7]:kv_cmp], seg(a, 9, 10))
    ws_s = [w.astype(BF16) for w in groups_s(w_in)]
    bs_s = [b[:, None, :] for b in groups_s(b_in)]
    ws_p = [w.astype(BF16) for w in groups_p(w_in)]
    bs_p = [b[:, None, :] for b in groups_p(b_in)]
    w_in_t = jnp.transpose(w_in, (0, 2, 1))
    wt_p = w_in_t[:, off[7]:off[8]].astype(BF16)
    bt_p = b_in[:, off[7]:off[8], None]
    wb16, wo16 = w_branch.astype(BF16), w_out.astype(BF16)
    wg16, wu16, wd16 = w_gate_e.astype(BF16), w_up_e.astype(BF16), w_down_e.astype(BF16)
    wr_pad = jnp.transpose(w_router).astype(BF16)
    rb_pad = router_bias.astype(F32).reshape(N_EXP, 1)
    eye = jnp.eye(HKV, dtype=F32)
    w1 = w_cmp1.reshape(depth, 2, 2, CMP_STRIDE, HD, CMP_HID)
    w1p = jnp.einsum('lshrdf,gG->lsrgdhGf', w1, eye).reshape(depth, 2, CMP_STRIDE, HKV * HD, 2 * HKV * CMP_HID)
    w1p = w1p.astype(BF16)
    w2p = jnp.einsum('lsfd,gG->lsgfGd', w_cmp2, eye).reshape(depth, 2, HKV * CMP_HID, HKV * HD).astype(BF16)
    gexp = np.zeros((LANES, 3 * D_A), np.float32)
    for br in range(3):
        for h in range(HQ):
            gexp[2 * H_M + br * HQ + h, br * D_A + h * HD:br * D_A + (h + 1) * HD] = 1.0
    gexp = jnp.asarray(np.tile(gexp, (3, 1)), dtype=BF16)

    ncp_p, n_cmp_p = T // CMP_STRIDE, (T - CMP_LEN) // CMP_STRIDE + 1
    assert n_cmp_p * CMP_STRIDE + CMP_LEN - 1 > T - 1
    bias_cmp_p = _bias_tiles(rel_bias, T // QT, ncp_p, tile_step=QT, key_stride=CMP_STRIDE, key_off=CMP_LEN - 1,
                             hi_valid=None, lead=False)
    ov_t_p = jnp.asarray(_overlap_np(ncp_p, n_cmp_p, -(-T // SLC_LEN)).T)
    n_far = -(-(KT - 1 + REL_MAX_DIST) // QT)
    bias_slc_p = _bias_tiles(rel_bias, n_far + 1, KT, tile_step=QT, key_stride=1, key_off=0, hi_valid=None, lead=True,
                             scale=LOG2E)
    bias_win_a = _bias_tiles(rel_bias, WINDOW // QT + 1, WINDOW, tile_step=QT, key_stride=1, key_off=0,
                             hi_valid=WINDOW, lead=True, scale=LOG2E)
    bias_win_b = _bias_tiles(rel_bias, 2, QT, tile_step=-2 * WINDOW, key_stride=1, key_off=0,
                             hi_valid=WINDOW, lead=True, scale=LOG2E)

    ncp_s = past // CMP_STRIDE
    n_cmp_s = (past + TS - CMP_LEN) // CMP_STRIDE + 1
    assert n_cmp_s <= ncp_s and -(-(past + TS) // SLC_LEN) <= NSP and TS <= NQS
    qs = past + np.arange(NQS)[:, None]
    d_cmp_s = qs - (np.arange(ncp_s)[None, :] * CMP_STRIDE + CMP_LEN - 1)
    bias_cmp_s = _bias_table(rel_bias, d_cmp_s, (d_cmp_s >= 0) & (np.arange(ncp_s)[None, :] < n_cmp_s))
    ov_s = jnp.asarray(_overlap_np(ncp_s, n_cmp_s, -(-(past + TS) // SLC_LEN)))
    key_s = np.arange(past + LANES)[None, :]
    d_slc_s = qs - key_s
    bias_slc_s = _bias_table(rel_bias, d_slc_s, (d_slc_s >= 0) & (key_s < past + TS))
    expand_s = jnp.asarray((np.arange(NSP)[:, None] == key_s // SLC_LEN).astype(np.float32)).astype(BF16)
    idx_w = np.arange(win_buf + LANES)[None, :]
    d_win_s = qs - (past - win_buf + idx_w)
    bias_win_s = _bias_table(rel_bias, d_win_s, (d_win_s >= 0) & (d_win_s < WINDOW) & (idx_w < win_buf + TS))

    to_feature_major = lambda a: jnp.transpose(a, (0, 1, 3, 4, 5, 2))
    cache_t = to_feature_major(cache_kv).reshape(depth * n_pool, 4 * LANES, page)
    win_t = to_feature_major(state_win_kv).reshape(depth * DB, 2 * LANES, win_buf)
    from_feature_major = lambda a, slots: jnp.transpose(
        a.reshape(a.shape[:2] + (slots, HKV, HD, a.shape[-1])), (0, 1, 5, 2, 3, 4))

    LP = _pick_tile(T, (256, 128, 64))
    LS = NQS
    zeros_state = (jnp.zeros((B, H_M, DH_M, DH_M), F32), jnp.zeros((B, H_M, DH_M), F32), jnp.zeros((B, H_M), F32))
    state_c_all = state_mlstm_C.reshape(depth * DB, H_M, DH_M, DH_M)

    xp = x_prompt.reshape(B * T, D)
    xs = x_sample.reshape(DB * TS, D)
    outs = [[] for _ in range(10)]
    kv_all = jnp.zeros((depth * B, 4 * LANES, T), F32)
    win_all = jnp.zeros(win_t.shape, F32)
    c_all = jnp.zeros(state_c_all.shape, F32)
    for l in range(depth):
        zm, zg, zq, zc, zgm, kv_all, zw_t, z16_t = _inproj(
            xp, [w[l] for w in ws_p], [b[l] for b in bs_p], wt_p[l], bt_p[l], t_split=4 * LANES, n_seq=B,
            layer=l, depth=depth, carried=kv_all, dtypes=[F32, F32, F32, F32, BF16])
        mout, c_p, n_p, m_p = _mlstm(zm, zg, mh_norm_g[l], *zeros_state, n_seq=B, L=LP, l_valid=LP)
        part = _cmp_partial(zc.reshape(B * T // CMP_STRIDE, CMP_STRIDE * 2 * LANES), w1p[l])
        ckv = _cmp_finish_prompt(part, w2p[l], B)
        ocmp, sel = _nsa_cmp_prompt(zq, ckv, bias_cmp_p, ov_t_p, B)
        oslc = _flash_prompt(zq, z16_t, 2, 3, bias_slc_p, sel, B)
        owin = _window_prompt(zq, z16_t, 4, 5, bias_win_a, bias_win_b, B)
        x1 = _merge(xp, mout, ocmp, oslc, owin, zg, zgm, gexp, wb16[l], wo16[l], ln1_g[l], ln1_b[l], alpha)
        xp = _moe(x1, wr_pad, rb_pad, wg16, wu16, wd16, l, ln2_g[l], ln2_b[l], alpha)
        outs[1].append(zw_t[:, :, T - win_buf:])
        outs[2].append(c_p)
        outs[3].append(n_p)
        outs[4].append(m_p)
        zm, zg, zq, zkv, zw, zgm = _inproj(xs, [w[l] for w in ws_s], [b[l] for b in bs_s])
        padt = lambda a: jnp.pad(a.reshape(DB, TS, -1), ((0, 0), (0, LS - TS), (0, 0))).reshape(DB * LS, -1)
        mout, c_all, n_s, m_s = _mlstm(padt(zm), padt(zg), mh_norm_g[l], state_c_all, state_mlstm_n[l],
                                       state_mlstm_m[l], n_seq=DB, L=LS, l_valid=TS, c_base=l * DB, c_layers=depth,
                                       carried=c_all)
        mout = mout.reshape(DB, LS, D_M)[:, :TS].reshape(DB * TS, D_M)
        ckv = _cmp_paged(cache_t, w1p[l].reshape(2, CMP_STRIDE // 2, 2 * LANES, -1), w2p[l], page_table, l * n_pool)
        padq = lambda a: jnp.pad(a.reshape(DB, TS, -1), ((0, 0), (0, NQS - TS), (0, 0)))
        *o3, win_all = _nsa_sample(padq(zq), padq(zkv), padq(zw), TS, cache_t, l * n_pool, page_table, win_t, l * DB,
                                   ckv, bias_cmp_s, bias_slc_s, bias_win_s, ov_s, expand_s, win_all)
        ocmp, oslc, owin = [o[:, :TS].reshape(DB * TS, D_A) for o in o3]
        x1 = _merge(xs, mout, ocmp, oslc, owin, zg, zgm, gexp, wb16[l], wo16[l], ln1_g[l], ln1_b[l], alpha)
        xs = _moe(x1, wr_pad, rb_pad, wg16, wu16, wd16, l, ln2_g[l], ln2_b[l], alpha)
        outs[5].append(zkv.reshape(DB, TS, 4, HKV, HD))
        outs[8].append(n_s)
        outs[9].append(m_s)
    stacked = [jnp.stack(o) if o else None for o in outs]
    stacked[0] = from_feature_major(kv_all.reshape(depth, B, 4 * LANES, T), 4)
    stacked[1] = from_feature_major(stacked[1], 2)
    stacked[6] = from_feature_major(win_all.reshape(depth, DB, 2 * LANES, win_buf), 2)
    stacked[7] = c_all.reshape(depth, DB, H_M, DH_M, DH_M)
    return (xp.reshape(B, T, D), xs.reshape(DB, TS, D)) + tuple(stacked)
```
